```python
import jax, jax.numpy as jnp
from jax import lax
import numpy as np

D_MODEL = 2048
BATCH = 8
SEQ = 8192
DEPTH = 4

HEAD_DIM = 128
ROPE_THETA = 10000.0
GRID_W = 64
BLOCK = 128
EPS = 1e-6
NEG_INF = -1e30
A_HEADS = 4
A_KV = 2
A_WINDOW = 128
B_HEADS = 4
B_KV = 4
B_PATTERNS = ((128, 1), (512, 4), (2048, 16))
C_HEADS = 4
C_KV = 2
D_HEADS = 4
D_KV = 4
NA_ROWS = 8
NA_COLS = 16

MIXER_HEADS = ((A_HEADS, A_KV), (B_HEADS, B_KV), (C_HEADS, C_KV), (D_HEADS, D_KV))
N_BRANCH = 4
BRANCH_W = A_HEADS * HEAD_DIM
IN_COLS = sum((2 * h + 2 * kv) * HEAD_DIM for h, kv in MIXER_HEADS)

kernel_name = "hybrid_gated_parallel_mixers_encoder"


def rms_norm(x, g):
    xf = x.astype(jnp.float32)
    y = xf * lax.rsqrt(jnp.mean(xf * xf, axis=-1, keepdims=True) + EPS)
    return (y * g.astype(jnp.float32)).astype(x.dtype)


def rope_tables(pos, dim):
    inv = ROPE_THETA ** (-jnp.arange(0, dim, 2, dtype=jnp.float32) / dim)
    ang = pos.astype(jnp.float32)[:, None] * inv[None, :]
    ang = jnp.concatenate([ang, ang], axis=-1)
    return (jnp.cos(ang), jnp.sin(ang))


def apply_rope(x, cos, sin):
    xf = x.astype(jnp.float32)
    x1, x2 = jnp.split(xf, 2, axis=-1)
    return (xf * cos + jnp.concatenate([-x2, x1], axis=-1) * sin).astype(x.dtype)


def apply_axial_rope(x, cos_r, sin_r, cos_c, sin_c):
    half = x.shape[-1] // 2
    return jnp.concatenate([apply_rope(x[..., :half], cos_r, sin_r),
                            apply_rope(x[..., half:], cos_c, sin_c)], axis=-1)


def column_split_points():
    points, total = [], 0
    for heads, kv in MIXER_HEADS:
        for width in (heads * HEAD_DIM, kv * HEAD_DIM, kv * HEAD_DIM, heads * HEAD_DIM):
            total += width
            points.append(total)
    return points[:-1]


def to_q_heads(t, n_kv):
    b, s, _ = t.shape
    return t.reshape(b, s, n_kv, -1, HEAD_DIM).transpose(0, 2, 3, 1, 4)


def to_kv_heads(t):
    b, s, _ = t.shape
    return t.reshape(b, s, -1, HEAD_DIM).transpose(0, 2, 1, 3)


def merge_heads(y):
    return jnp.moveaxis(y, -2, 1).reshape(y.shape[0], y.shape[-2], -1)


def banded_attention(q, k, v, reach, sink=None):
    n, hkv, g, length, hd = q.shape
    blk = min(BLOCK, length)
    nb = -(-length // blk)
    lp = nb * blk
    qb = jnp.pad(q, ((0, 0), (0, 0), (0, 0), (0, lp - length), (0, 0))).reshape(n, hkv, g, nb, blk, hd)
    pad_kv = ((0, 0), (0, 0), (reach, lp - length + reach), (0, 0))
    kp = jnp.pad(k, pad_kv)
    vp = jnp.pad(v, pad_kv)
    span = blk + 2 * reach
    idx = jnp.arange(nb)[:, None] * blk + jnp.arange(span)[None, :]
    kband = kp[:, :, idx]
    vband = vp[:, :, idx]
    kpos = (idx - reach)[:, None, :]
    qpos = (jnp.arange(nb)[:, None] * blk + jnp.arange(blk)[None, :])[:, :, None]
    mask = (jnp.abs(qpos - kpos) <= reach) & (kpos >= 0) & (kpos < length)
    s = jnp.einsum('nhgbqd,nhbkd->nhgbqk', qb, kband).astype(jnp.float32) * (hd ** -0.5)
    s = jnp.where(mask, s, NEG_INF)
    m = jnp.max(s, axis=-1)
    if sink is not None:
        sink_b = sink.astype(jnp.float32)[None, :, :, None, None]
        m = jnp.maximum(m, sink_b)
    p = jnp.exp(s - m[..., None])
    den = jnp.sum(p, axis=-1)
    if sink is not None:
        den = den + jnp.exp(sink_b - m)
    o = jnp.einsum('nhgbqk,nhbkd->nhgbqd', p.astype(v.dtype), vband).astype(jnp.float32) / den[..., None]
    o = o.reshape(n, hkv, g, lp, hd)[:, :, :, :length].astype(q.dtype)
    lse = (m + jnp.log(den)).reshape(n, hkv, g, lp)[..., :length]
    return o, lse


def dilated_attention(q, k, v):
    bsz, hkv, g, seq, hd = q.shape
    outs, lses = [], []
    for window, dil in B_PATTERNS:
        reach = (window // 2) // dil
        length = seq // dil
        qd = q.reshape(bsz, hkv, g, length, dil, hd).transpose(0, 4, 1, 2, 3, 5).reshape(bsz * dil, hkv, g, length, hd)
        kd = k.reshape(bsz, hkv, length, dil, hd).transpose(0, 3, 1, 2, 4).reshape(bsz * dil, hkv, length, hd)
        vd = v.reshape(bsz, hkv, length, dil, hd).transpose(0, 3, 1, 2, 4).reshape(bsz * dil, hkv, length, hd)
        o, lse = banded_attention(qd, kd, vd, reach)
        outs.append(o.reshape(bsz, dil, hkv, g, length, hd).transpose(0, 2, 3, 4, 1, 5).reshape(bsz, hkv, g, seq, hd))
        lses.append(lse.reshape(bsz, dil, hkv, g, length).transpose(0, 2, 3, 4, 1).reshape(bsz, hkv, g, seq))
    w = jax.nn.softmax(jnp.stack(lses, axis=0), axis=0)
    return jnp.sum(w[..., None] * jnp.stack(outs, axis=0).astype(jnp.float32), axis=0).astype(q.dtype)


def dense_block_attention(q, k, v):
    bsz, hkv, g, seq, hd = q.shape
    nb = seq // BLOCK
    qb = jnp.moveaxis(q.reshape(bsz, hkv, g, nb, BLOCK, hd), 3, 0)

    def attend(qi):
        s = jnp.einsum('nhgqd,nhkd->nhgqk', qi, k).astype(jnp.float32) * (hd ** -0.5)
        p = jax.nn.softmax(s, axis=-1)
        return jnp.einsum('nhgqk,nhkd->nhgqd', p.astype(v.dtype), v)

    o = lax.map(attend, qb)
    return jnp.moveaxis(o, 0, 3).reshape(bsz, hkv, g, seq, hd)


def neighbourhood_attention(q, k, v, rel_bias, rows):
    bsz, heads, seq, hd = q.shape
    kr = min(NA_ROWS, rows)
    kc = min(NA_COLS, GRID_W)
    qg = q.reshape(bsz, heads, rows, GRID_W, hd)
    kg = k.reshape(bsz, heads, rows, GRID_W, hd)
    vg = v.reshape(bsz, heads, rows, GRID_W, hd)
    r = jnp.arange(rows)
    col = jnp.arange(GRID_W)
    row_start = jnp.clip(r - kr // 2, 0, rows - kr)
    row_idx = row_start[:, None] + jnp.arange(kr)[None, :]
    col_start = jnp.clip(col - kc // 2, 0, GRID_W - kc)
    col_mask = (col[None, :] >= col_start[:, None]) & (col[None, :] < col_start[:, None] + kc)
    kblk = kg[:, :, row_idx].reshape(bsz, heads, rows, kr * GRID_W, hd)
    vblk = vg[:, :, row_idx].reshape(bsz, heads, rows, kr * GRID_W, hd)
    dr = row_idx - r[:, None]
    dc = jnp.clip(col[None, :] - col[:, None], -(NA_COLS - 1), NA_COLS - 1)
    bias = rel_bias[:, (dr + NA_ROWS - 1)[:, None, :, None], (dc + NA_COLS - 1)[None, :, None, :]]
    bias = bias.reshape(heads, rows, GRID_W, kr * GRID_W).astype(jnp.float32)
    mask = jnp.broadcast_to(col_mask[:, None, :], (GRID_W, kr, GRID_W)).reshape(GRID_W, kr * GRID_W)
    s = jnp.einsum('bhrqd,bhrkd->bhrqk', qg, kblk).astype(jnp.float32) * (hd ** -0.5) + bias
    s = jnp.where(mask, s, NEG_INF)
    p = jax.nn.softmax(s, axis=-1)
    o = jnp.einsum('bhrqk,bhrkd->bhrqd', p.astype(v.dtype), vblk)
    return o.reshape(bsz, heads, seq, hd)


def _fwd_setup_inputs(seed: int = 0) -> dict:
    key = jax.random.key(seed)
    ks = jax.random.split(key, 14)
    nrm = jax.random.normal
    f32 = jnp.float32
    return {
        "x": nrm(ks[0], (BATCH, SEQ, D_MODEL), f32),
        "c": nrm(ks[1], (BATCH, D_MODEL), f32),
        "norm_g": 1.0 + 0.02 * nrm(ks[2], (DEPTH, D_MODEL), f32),
        "w_ada": nrm(ks[3], (DEPTH, D_MODEL, 3 * D_MODEL), f32) * D_MODEL ** -0.5,
        "b_ada": 0.01 * nrm(ks[4], (DEPTH, 3 * D_MODEL), f32),
        "w_in": nrm(ks[5], (DEPTH, D_MODEL, IN_COLS), f32) * D_MODEL ** -0.5,
        "a_sink": nrm(ks[6], (DEPTH, A_HEADS), f32),
        "c_q_norm": 1.0 + 0.02 * nrm(ks[7], (DEPTH, HEAD_DIM), f32),
        "c_k_norm": 1.0 + 0.02 * nrm(ks[8], (DEPTH, HEAD_DIM), f32),
        "d_rel_bias": 0.1 * nrm(ks[9], (DEPTH, D_HEADS, 2 * NA_ROWS - 1, 2 * NA_COLS - 1), f32),
        "w_gate_merge": nrm(ks[10], (DEPTH, D_MODEL, N_BRANCH * D_MODEL), f32) * D_MODEL ** -0.5,
        "w_branch": nrm(ks[11], (DEPTH, N_BRANCH, BRANCH_W, D_MODEL), f32) * BRANCH_W ** -0.5,
        "w_out": nrm(ks[12], (DEPTH, D_MODEL, D_MODEL), f32) * D_MODEL ** -0.5,
        "final_g": 1.0 + 0.02 * nrm(ks[13], (D_MODEL,), f32),
    }


def _fwd_reference(x, c, norm_g, w_ada, b_ada, w_in, a_sink, c_q_norm, c_k_norm, d_rel_bias, w_gate_merge, w_branch, w_out, final_g):
    bsz, seq, dm = x.shape
    rows = seq // GRID_W
    pos = jnp.arange(seq, dtype=jnp.int32)
    cos1, sin1 = rope_tables(pos, HEAD_DIM)
    axial = rope_tables(pos // GRID_W, HEAD_DIM // 2) + rope_tables(pos % GRID_W, HEAD_DIM // 2)
    split_points = column_split_points()
    cond = jax.nn.silu(c)
    for layer in range(DEPTH):
        shift, scale, gate = jnp.split(cond @ w_ada[layer] + b_ada[layer], 3, axis=-1)
        h = rms_norm(x, norm_g[layer]) * (1.0 + scale[:, None, :]) + shift[:, None, :]
        (qa, ka, va, ga, qb, kb, vb, gb, qc, kc, vc, gc, qd, kd, vd, gd) = jnp.split(h @ w_in[layer], split_points, axis=-1)
        ya, _ = banded_attention(apply_rope(to_q_heads(qa, A_KV), cos1, sin1),
                                 apply_rope(to_kv_heads(ka), cos1, sin1), to_kv_heads(va),
                                 A_WINDOW, a_sink[layer].reshape(A_KV, A_HEADS // A_KV))
        yb = dilated_attention(apply_rope(to_q_heads(qb, B_KV), cos1, sin1),
                               apply_rope(to_kv_heads(kb), cos1, sin1), to_kv_heads(vb))
        yc = dense_block_attention(apply_axial_rope(rms_norm(to_q_heads(qc, C_KV), c_q_norm[layer]), *axial),
                                   apply_axial_rope(rms_norm(to_kv_heads(kc), c_k_norm[layer]), *axial),
                                   to_kv_heads(vc))
        yd = neighbourhood_attention(to_kv_heads(qd), to_kv_heads(kd), to_kv_heads(vd), d_rel_bias[layer], rows)
        branches = jnp.stack([merge_heads(ya) * jax.nn.silu(ga), merge_heads(yb) * jax.nn.silu(gb),
                              merge_heads(yc) * jax.nn.silu(gc), merge_heads(yd) * jax.nn.silu(gd)], axis=2)
        proj = jnp.einsum('bsnw,nwd->bsnd', branches, w_branch[layer])
        merge_gate = jax.nn.sigmoid((h @ w_gate_merge[layer]).reshape(bsz, seq, N_BRANCH, dm))
        x = x + gate[:, None, :] * (jnp.sum(merge_gate * proj, axis=2) @ w_out[layer])
    return rms_norm(x, final_g)


import jax as _jax
import jax.numpy as _jnp

TWIN_FORMAT = 'train_step'
FWD_PARAMS = ['x', 'c', 'norm_g', 'w_ada', 'b_ada', 'w_in', 'a_sink', 'c_q_norm', 'c_k_norm', 'd_rel_bias', 'w_gate_merge', 'w_branch', 'w_out', 'final_g']
TWIN_WEIGHTS = ['norm_g', 'w_ada', 'b_ada', 'w_in', 'a_sink', 'c_q_norm', 'c_k_norm', 'd_rel_bias', 'w_gate_merge', 'w_branch', 'w_out', 'final_g']
TWIN_DIFF_INPUT = 'x'
TWIN_INPUTS = ['x', 'c', 'norm_g', 'w_ada', 'b_ada', 'w_in', 'a_sink', 'c_q_norm', 'c_k_norm', 'd_rel_bias', 'w_gate_merge', 'w_branch', 'w_out', 'final_g', 'loss_target', 'm_norm_g', 'm_w_ada', 'm_b_ada', 'm_w_in', 'm_a_sink', 'm_c_q_norm', 'm_c_k_norm', 'm_d_rel_bias', 'm_w_gate_merge', 'm_w_branch', 'm_w_out', 'm_final_g', 'v_norm_g', 'v_w_ada', 'v_b_ada', 'v_w_in', 'v_a_sink', 'v_c_q_norm', 'v_c_k_norm', 'v_d_rel_bias', 'v_w_gate_merge', 'v_w_branch', 'v_w_out', 'v_final_g']
TWIN_OUTPUTS = ['loss', 'grad_x', 'grad_norm_g', 'grad_w_ada', 'grad_b_ada', 'grad_w_in', 'grad_a_sink', 'grad_c_q_norm', 'grad_c_k_norm', 'grad_d_rel_bias', 'grad_w_gate_merge', 'grad_w_branch', 'grad_w_out', 'grad_final_g', 'delta_norm_g', 'delta_w_ada', 'delta_b_ada', 'delta_w_in', 'delta_a_sink', 'delta_c_q_norm', 'delta_c_k_norm', 'delta_d_rel_bias', 'delta_w_gate_merge', 'delta_w_branch', 'delta_w_out', 'delta_final_g', 'new_m_norm_g', 'new_m_w_ada', 'new_m_b_ada', 'new_m_w_in', 'new_m_a_sink', 'new_m_c_q_norm', 'new_m_c_k_norm', 'new_m_d_rel_bias', 'new_m_w_gate_merge', 'new_m_w_branch', 'new_m_w_out', 'new_m_final_g', 'new_v_norm_g', 'new_v_w_ada', 'new_v_b_ada', 'new_v_w_in', 'new_v_a_sink', 'new_v_c_q_norm', 'new_v_c_k_norm', 'new_v_d_rel_bias', 'new_v_w_gate_merge', 'new_v_w_branch', 'new_v_w_out', 'new_v_final_g']
TWIN_LEAF_KINDS = {'loss': 'loss', 'grad_x': 'grad_x', 'grad_norm_g': 'grad_w', 'grad_w_ada': 'grad_w', 'grad_b_ada': 'grad_w', 'grad_w_in': 'grad_w', 'grad_a_sink': 'grad_w', 'grad_c_q_norm': 'grad_w', 'grad_c_k_norm': 'grad_w', 'grad_d_rel_bias': 'grad_w', 'grad_w_gate_merge': 'grad_w', 'grad_w_branch': 'grad_w', 'grad_w_out': 'grad_w', 'grad_final_g': 'grad_w', 'delta_norm_g': 'delta_w', 'delta_w_ada': 'delta_w', 'delta_b_ada': 'delta_w', 'delta_w_in': 'delta_w', 'delta_a_sink': 'delta_w', 'delta_c_q_norm': 'delta_w', 'delta_c_k_norm': 'delta_w', 'delta_d_rel_bias': 'delta_w', 'delta_w_gate_merge': 'delta_w', 'delta_w_branch': 'delta_w', 'delta_w_out': 'delta_w', 'delta_final_g': 'delta_w', 'new_m_norm_g': 'new_m', 'new_m_w_ada': 'new_m', 'new_m_b_ada': 'new_m', 'new_m_w_in': 'new_m', 'new_m_a_sink': 'new_m', 'new_m_c_q_norm': 'new_m', 'new_m_c_k_norm': 'new_m', 'new_m_d_rel_bias': 'new_m', 'new_m_w_gate_merge': 'new_m', 'new_m_w_branch': 'new_m', 'new_m_w_out': 'new_m', 'new_m_final_g': 'new_m', 'new_v_norm_g': 'new_v', 'new_v_w_ada': 'new_v', 'new_v_b_ada': 'new_v', 'new_v_w_in': 'new_v', 'new_v_a_sink': 'new_v', 'new_v_c_q_norm': 'new_v', 'new_v_c_k_norm': 'new_v', 'new_v_d_rel_bias': 'new_v', 'new_v_w_gate_merge': 'new_v', 'new_v_w_branch': 'new_v', 'new_v_w_out': 'new_v', 'new_v_final_g': 'new_v'}


def _forward(args):
    return _fwd_reference(*[args[k] for k in FWD_PARAMS])


def _output_shape():
    def fwd():
        inp = _fwd_setup_inputs(0)
        return _fwd_reference(*[inp[k] for k in FWD_PARAMS])
    out = _jax.eval_shape(fwd)
    return out.shape, out.dtype

N_MICROBATCH = 1
ADAM_LR = 0.001
ADAM_B1 = 0.9
ADAM_B2 = 0.999
ADAM_EPS = 1e-08
ADAM_WD = 0.01
ADAM_STEP = 10
PER_EXAMPLE_BATCH_AXIS = {'x': 0, 'c': 0, 'loss_target': 0}
SHARED_INPUTS = []
_WEIGHT_DTYPES = {'norm_g': _jnp.float32, 'w_ada': _jnp.float32, 'b_ada': _jnp.float32, 'w_in': _jnp.float32, 'a_sink': _jnp.float32, 'c_q_norm': _jnp.float32, 'c_k_norm': _jnp.float32, 'd_rel_bias': _jnp.float32, 'w_gate_merge': _jnp.float32, 'w_branch': _jnp.float32, 'w_out': _jnp.float32, 'final_g': _jnp.float32}
MOMENT_SCALE = {'norm_g': 5.466832e-02, 'w_ada': 4.201310e-02, 'b_ada': 7.450612e-02, 'w_in': 3.707272e-02, 'a_sink': 2.075109e-03, 'c_q_norm': 1.241258e-02, 'c_k_norm': 1.276407e-02, 'd_rel_bias': 9.097258e-03, 'w_gate_merge': 9.886915e-03, 'w_branch': 2.208292e-02, 'w_out': 4.413326e-02, 'final_g': 3.217085e+01}


def _to_microbatches(a, axis):
    t = _jnp.moveaxis(a, axis, 0)
    t = t.reshape((N_MICROBATCH, t.shape[0] // N_MICROBATCH) + t.shape[1:])
    return _jnp.moveaxis(t, 1, axis + 1)


def setup_inputs(seed: int = 0) -> dict:
    inp = _fwd_setup_inputs(seed)
    key = _jax.random.fold_in(_jax.random.key(seed), 7919)
    shape, _ = _output_shape()
    out = dict(inp)
    out["loss_target"] = _jax.random.normal(_jax.random.fold_in(key, 0), shape, _jnp.float32)
    for i, name in enumerate(TWIN_WEIGHTS):
        w = inp[name].astype(_jnp.float32)
        if MOMENT_SCALE is None:
            s = _jnp.sqrt(_jnp.mean(_jnp.square(w)) + 1e-30)
        else:
            s = MOMENT_SCALE[name]
        km, kv = _jax.random.split(_jax.random.fold_in(key, i + 1))
        out[name] = w
        out["m_" + name] = s * _jax.random.normal(km, w.shape, _jnp.float32)
        out["v_" + name] = (s * s) * _jax.random.uniform(kv, w.shape, _jnp.float32, 0.5, 1.5)
    if N_MICROBATCH > 1:
        for name, axis in PER_EXAMPLE_BATCH_AXIS.items():
            out[name] = _to_microbatches(out[name], axis)
    return {'x': out['x'], 'c': out['c'], 'norm_g': out['norm_g'], 'w_ada': out['w_ada'], 'b_ada': out['b_ada'], 'w_in': out['w_in'], 'a_sink': out['a_sink'], 'c_q_norm': out['c_q_norm'], 'c_k_norm': out['c_k_norm'], 'd_rel_bias': out['d_rel_bias'], 'w_gate_merge': out['w_gate_merge'], 'w_branch': out['w_branch'], 'w_out': out['w_out'], 'final_g': out['final_g'], 'loss_target': out['loss_target'], 'm_norm_g': out['m_norm_g'], 'm_w_ada': out['m_w_ada'], 'm_b_ada': out['m_b_ada'], 'm_w_in': out['m_w_in'], 'm_a_sink': out['m_a_sink'], 'm_c_q_norm': out['m_c_q_norm'], 'm_c_k_norm': out['m_c_k_norm'], 'm_d_rel_bias': out['m_d_rel_bias'], 'm_w_gate_merge': out['m_w_gate_merge'], 'm_w_branch': out['m_w_branch'], 'm_w_out': out['m_w_out'], 'm_final_g': out['m_final_g'], 'v_norm_g': out['v_norm_g'], 'v_w_ada': out['v_w_ada'], 'v_b_ada': out['v_b_ada'], 'v_w_in': out['v_w_in'], 'v_a_sink': out['v_a_sink'], 'v_c_q_norm': out['v_c_q_norm'], 'v_c_k_norm': out['v_c_k_norm'], 'v_d_rel_bias': out['v_d_rel_bias'], 'v_w_gate_merge': out['v_w_gate_merge'], 'v_w_branch': out['v_w_branch'], 'v_w_out': out['v_w_out'], 'v_final_g': out['v_final_g']}


def _loss(weights, diff, rest, loss_target):
    with _jax.named_scope("forward"):
        args = {**rest, TWIN_DIFF_INPUT: diff, **{k: w.astype(_WEIGHT_DTYPES[k]) for k, w in weights.items()}}
        y = _forward(args)
    with _jax.named_scope("loss_head"):
        err = _jnp.square(y.astype(_jnp.float32) - loss_target)
        return 0.5 * _jnp.sum(_jnp.mean(err, axis=-1)) if err.ndim else 0.5 * err


def _adamw(w, g, m, v):
    m = ADAM_B1 * m + (1.0 - ADAM_B1) * g
    v = ADAM_B2 * v + (1.0 - ADAM_B2) * _jnp.square(g)
    m_hat = m / (1.0 - ADAM_B1 ** ADAM_STEP)
    v_hat = v / (1.0 - ADAM_B2 ** ADAM_STEP)
    delta = -ADAM_LR * (m_hat / (_jnp.sqrt(v_hat) + ADAM_EPS) + ADAM_WD * w)
    return delta, m, v


def reference(x, c, norm_g, w_ada, b_ada, w_in, a_sink, c_q_norm, c_k_norm, d_rel_bias, w_gate_merge, w_branch, w_out, final_g, loss_target, m_norm_g, m_w_ada, m_b_ada, m_w_in, m_a_sink, m_c_q_norm, m_c_k_norm, m_d_rel_bias, m_w_gate_merge, m_w_branch, m_w_out, m_final_g, v_norm_g, v_w_ada, v_b_ada, v_w_in, v_a_sink, v_c_q_norm, v_c_k_norm, v_d_rel_bias, v_w_gate_merge, v_w_branch, v_w_out, v_final_g):
    given = dict(x=x, c=c, norm_g=norm_g, w_ada=w_ada, b_ada=b_ada, w_in=w_in, a_sink=a_sink, c_q_norm=c_q_norm, c_k_norm=c_k_norm, d_rel_bias=d_rel_bias, w_gate_merge=w_gate_merge, w_branch=w_branch, w_out=w_out, final_g=final_g, loss_target=loss_target, m_norm_g=m_norm_g, m_w_ada=m_w_ada, m_b_ada=m_b_ada, m_w_in=m_w_in, m_a_sink=m_a_sink, m_c_q_norm=m_c_q_norm, m_c_k_norm=m_c_k_norm, m_d_rel_bias=m_d_rel_bias, m_w_gate_merge=m_w_gate_merge, m_w_branch=m_w_branch, m_w_out=m_w_out, m_final_g=m_final_g, v_norm_g=v_norm_g, v_w_ada=v_w_ada, v_b_ada=v_b_ada, v_w_in=v_w_in, v_a_sink=v_a_sink, v_c_q_norm=v_c_q_norm, v_c_k_norm=v_c_k_norm, v_d_rel_bias=v_d_rel_bias, v_w_gate_merge=v_w_gate_merge, v_w_branch=v_w_branch, v_w_out=v_w_out, v_final_g=v_final_g)
    weights = {n: given[n] for n in TWIN_WEIGHTS}
    shared = {n: given[n] for n in SHARED_INPUTS}
    per_example = {n: given[n] for n in ['x', 'c']}
    grad_fn = _jax.value_and_grad(_loss, argnums=(0, 1))

    def one_microbatch(ex, loss_target):
        ex = dict(ex)
        diff = ex.pop(TWIN_DIFF_INPUT)
        return grad_fn(weights, diff, {**shared, **ex}, loss_target)

    if N_MICROBATCH == 1:
        loss, (grad_w, grad_x) = one_microbatch(per_example, given["loss_target"])
    else:
        def body(carry, xs):
            loss_sum, grad_sum = carry
            l_k, (gw_k, gx_k) = one_microbatch(xs[0], xs[1])
            with _jax.named_scope("update"):
                return (loss_sum + l_k, _jax.tree.map(_jnp.add, grad_sum, gw_k)), gx_k

        init = (_jnp.zeros((), _jnp.float32), _jax.tree.map(_jnp.zeros_like, weights))
        (loss, grad_w), grad_x = _jax.lax.scan(body, init, (per_example, given["loss_target"]))
    with _jax.named_scope("update"):
        delta_w, new_m, new_v = {}, {}, {}
        for n in TWIN_WEIGHTS:
            delta_w[n], new_m[n], new_v[n] = _adamw(weights[n], grad_w[n], given["m_" + n], given["v_" + n])
    return (loss, grad_x, *[grad_w[n] for n in TWIN_WEIGHTS], *[delta_w[n] for n in TWIN_WEIGHTS],
            *[new_m[n] for n in TWIN_WEIGHTS], *[new_v[n] for n in TWIN_WEIGHTS])
```

```python
import functools

import jax
import jax.numpy as jnp
import numpy as np
from jax import lax
from jax.experimental import pallas as pl
from jax.experimental.pallas import tpu as pltpu

F32 = jnp.float32
BF16 = jnp.bfloat16

N_DEV = 8
DEPTH = 4
HD = 128
GRID_W = 64
EPS = 1e-6
NEG_INF = -1e30
ROPE_THETA = 10000.0
A_REACH = 128
B_PATTERNS = ((128, 1), (512, 4), (2048, 16))
NA_ROWS = 8
NA_COLS = 16
SCALE = HD ** -0.5
A_Q, A_K, A_V, A_G = 0, 4, 6, 8
B_Q, B_K, B_V, B_G = 12, 16, 20, 24
C_Q, C_K, C_V, C_G = 28, 32, 34, 36
D_Q, D_K, D_V, D_G = 40, 44, 48, 52
IN_HEADS = 56

ADAM_LR, ADAM_B1, ADAM_B2, ADAM_EPS, ADAM_WD, ADAM_STEP = 0.001, 0.9, 0.999, 1e-08, 0.01, 10

V7X_VMEM_LIMIT = 56 * 2 ** 20
ATT_TILE = 512


def _pc(body, **kw):
    return pl.pallas_call(body, **kw)


def _cparams(sem=None):
    if sem is None:
        return pltpu.CompilerParams(vmem_limit_bytes=V7X_VMEM_LIMIT)
    return pltpu.CompilerParams(dimension_semantics=sem, vmem_limit_bytes=V7X_VMEM_LIMIT)


def _tile(n, cap):
    if n <= cap:
        return n
    t = (cap // 128) * 128
    while n % t:
        t -= 128
    return t


def _sds(shape, dtype):
    return jax.ShapeDtypeStruct(tuple(shape), dtype)


def _exchange(name, ins, gather):
    n = len(ins)
    outs = [_sds((N_DEV,) + (a.shape if gather else a.shape[1:]), a.dtype) for a in ins]

    def body(*refs):
        in_refs, out_refs = refs[:n], refs[n:2 * n]
        send_sems, recv_sems, loc_sems = refs[2 * n:]
        x, y, c = lax.axis_index("x"), lax.axis_index("y"), lax.axis_index("c")
        me = 4 * x + 2 * y + c
        local = []
        for t in range(n):
            src = in_refs[t] if gather else in_refs[t].at[me]
            cp = pltpu.make_async_copy(src, out_refs[t].at[me], loc_sems.at[t])
            cp.start()
            local.append(cp)
        remote = []
        for p in range(1, N_DEV):
            tx = 1 - x if p & 4 else x
            ty = 1 - y if p & 2 else y
            tc = 1 - c if p & 1 else c
            peer = 4 * tx + 2 * ty + tc
            for t in range(n):
                src = in_refs[t] if gather else in_refs[t].at[peer]
                cp = pltpu.make_async_remote_copy(
                    src_ref=src, dst_ref=out_refs[t].at[me],
                    send_sem=send_sems.at[t, p - 1], recv_sem=recv_sems.at[t, p - 1],
                    device_id=(tx, ty, tc), device_id_type=pl.DeviceIdType.MESH)
                cp.start()
                remote.append(cp)
        for cp in remote:
            cp.wait()
        for cp in local:
            cp.wait()

    res = _pc(
        body, name=name, out_shape=outs,
        in_specs=[pl.BlockSpec(memory_space=pl.ANY)] * n,
        out_specs=[pl.BlockSpec(memory_space=pl.ANY)] * n,
        scratch_shapes=[pltpu.SemaphoreType.DMA((n, N_DEV - 1)), pltpu.SemaphoreType.DMA((n, N_DEV - 1)),
                        pltpu.SemaphoreType.DMA((n,))],
    )(*ins)
    return list(res)


def _mm(name, a, w, out_sds, grid, a_spec, w_spec, o_spec, dims, red_axis=None, add=None, add_spec=None):
    nred = grid[red_axis] if red_axis is not None else 1
    acc_shape = tuple(d for d in o_spec.block_shape if d is not None)

    def body(*refs):
        if add is not None:
            a_ref, w_ref, add_ref, o_ref = refs[:4]
            scr = refs[4:]
        else:
            a_ref, w_ref, o_ref = refs[:3]
            scr = refs[3:]
        prod = lax.dot_general(a_ref[...], w_ref[...], dims, preferred_element_type=F32)

        def finish(acc):
            if add is not None:
                acc = acc + add_ref[...].astype(F32)
            o_ref[...] = acc.astype(o_ref.dtype)

        if nred == 1:
            finish(prod)
        else:
            acc_ref = scr[0]
            k = pl.program_id(red_axis)

            @pl.when(k == 0)
            def _():
                acc_ref[...] = prod

            @pl.when(k > 0)
            def _():
                acc_ref[...] += prod

            @pl.when(k == nred - 1)
            def _():
                finish(acc_ref[...])

    sem = tuple("arbitrary" if i == red_axis else "parallel" for i in range(len(grid)))
    in_specs = [a_spec, w_spec] + ([add_spec] if add is not None else [])
    args = (a, w) + ((add,) if add is not None else ())
    return _pc(body, name=name, out_shape=out_sds, grid=grid, in_specs=in_specs, out_specs=o_spec,
               scratch_shapes=[pltpu.VMEM(acc_shape, F32)] if nred > 1 else [],
               compiler_params=_cparams(sem))(*args)


_NN = (((1,), (0,)), ((), ()))
_NT = (((1,), (1,)), ((), ()))
_TN = (((0,), (0,)), ((), ()))


def _mm_nn(name, a, w, out_dtype):
    T, K = a.shape
    G, _, n = w.shape
    tm, tn = min(T, 1024), _tile(n, 1024)
    nj = n // tn
    return _mm(name, a, w, _sds((T, G * n), out_dtype), (T // tm, G, nj),
               pl.BlockSpec((tm, K), lambda i, g, j: (i, 0)),
               pl.BlockSpec((None, K, tn), lambda i, g, j: (g, 0, j)),
               pl.BlockSpec((tm, tn), lambda i, g, j: (i, g * nj + j)), _NN)


def _mm_nt(name, dy, w, out_dtype, add=None):
    T = dy.shape[0]
    G, K, n = w.shape
    tm, tk = min(T, 1024), _tile(K, 1024)
    tn = _tile(n, 1024)
    nj = n // tn
    o_spec = pl.BlockSpec((tm, tk), lambda i, k, r: (i, k))
    return _mm(name, dy, w, _sds((T, K), out_dtype), (T // tm, K // tk, G * nj),
               pl.BlockSpec((tm, tn), lambda i, k, r: (i, r)),
               pl.BlockSpec((None, tk, tn), lambda i, k, r: (r // nj, k, r % nj)),
               o_spec, _NT, red_axis=2, add=add, add_spec=o_spec if add is not None else None)


def _mm_tn(name, a, dy, G, out_dtype):
    T, K = a.shape
    n = dy.shape[1] // G
    tt, tk, tn = min(T, 1024), _tile(K, 1024), _tile(n, 1024)
    nj = n // tn
    return _mm(name, a, dy, _sds((G, K, n), out_dtype), (K // tk, G, nj, T // tt),
               pl.BlockSpec((tt, tk), lambda k, g, j, t: (t, k)),
               pl.BlockSpec((tt, tn), lambda k, g, j, t: (t, g * nj + j)),
               pl.BlockSpec((None, tk, tn), lambda k, g, j, t: (g, k, j)), _TN, red_axis=3)


def _branch_nn(name, br, wb, out_dtype):
    T = br.shape[0]
    _, NB, W, n = wb.shape
    tm = min(T, 1024)
    return _mm(name, br, wb, _sds((T, NB * N_DEV * n), out_dtype), (T // tm, NB, N_DEV),
               pl.BlockSpec((tm, W), lambda i, b, d: (i, b)),
               pl.BlockSpec((None, None, W, n), lambda i, b, d: (d, b, 0, 0)),
               pl.BlockSpec((tm, n), lambda i, b, d: (i, b * N_DEV + d)), _NN)


def _branch_nt(name, dproj, wb, out_dtype):
    T = dproj.shape[0]
    _, NB, W, n = wb.shape
    tm = min(T, 1024)
    return _mm(name, dproj, wb, _sds((T, NB * W), out_dtype), (T // tm, NB, N_DEV),
               pl.BlockSpec((tm, n), lambda i, b, d: (i, b * N_DEV + d)),
               pl.BlockSpec((None, None, W, n), lambda i, b, d: (d, b, 0, 0)),
               pl.BlockSpec((tm, W), lambda i, b, d: (i, b)), _NT, red_axis=2)


def _branch_tn(name, br, dproj, n, out_dtype):
    T = br.shape[0]
    NB = 4
    W = br.shape[1] // NB
    tt = min(T, 1024)
    return _mm(name, br, dproj, _sds((N_DEV, NB, W, n), out_dtype), (NB, N_DEV, T // tt),
               pl.BlockSpec((tt, W), lambda b, d, t: (t, b)),
               pl.BlockSpec((tt, n), lambda b, d, t: (t, b * N_DEV + d)),
               pl.BlockSpec((None, None, W, n), lambda b, d, t: (d, b, 0, 0)), _TN, red_axis=2)


class _Att:
    def __init__(self, L, R, H, G, qcol, kcol, vcol, qstr, kstr, vstr, mode, reach=0, rows=0):
        self.L, self.R, self.H, self.G = L, R, H, G
        self.qcol, self.kcol, self.vcol = qcol, kcol, vcol
        self.qstr, self.kstr, self.vstr = qstr, kstr, vstr
        self.mode, self.reach, self.rows = mode, reach, rows
        self.t = min(ATT_TILE, L)
        self.nb = L // self.t
        if mode == "dense":
            self.off, self.nsteps = 0, self.nb
        else:
            self.off = min(-(-reach // self.t), self.nb - 1)
            self.nsteps = 2 * self.off + 1

    def other(self, i, s):
        return s if self.mode == "dense" else i + s - self.off

    def other_clamped(self, i, s):
        return s if self.mode == "dense" else jnp.clip(i + s - self.off, 0, self.nb - 1)

    def mask(self, qpos, kpos):
        if self.mode == "dense":
            return None
        if self.mode == "band":
            return jnp.abs(qpos - kpos) <= self.reach
        rq, cq = qpos >> 6, qpos & (GRID_W - 1)
        rk, ck = kpos >> 6, kpos & (GRID_W - 1)
        rs = jnp.clip(rq - NA_ROWS // 2, 0, self.rows - NA_ROWS)
        cs = jnp.clip(cq - NA_COLS // 2, 0, GRID_W - NA_COLS)
        return (rk >= rs) & (rk < rs + NA_ROWS) & (ck >= cs) & (ck < cs + NA_COLS)


def _scores(att, q, k, bias_ref, qi, ki):
    t = att.t
    sc = lax.dot_general(q, k, _NT, preferred_element_type=F32) * SCALE
    if bias_ref is not None:
        sc = sc + bias_ref[...]
    if att.mode != "dense":
        qpos = qi * t + lax.broadcasted_iota(jnp.int32, (t, 1), 0)
        kpos = ki * t + lax.broadcasted_iota(jnp.int32, (1, t), 1)
        sc = jnp.where(att.mask(qpos, kpos), sc, NEG_INF)
    return sc


def _attn_fwd(name, att, q, k, v, sink=None, bias=None):
    t, H, G = att.t, att.H, att.G

    def body(*refs):
        q_ref, k_ref, v_ref = refs[:3]
        pos = 3
        sink_ref = bias_ref = None
        if sink is not None:
            sink_ref = refs[pos]
            pos += 1
        if bias is not None:
            bias_ref = refs[pos]
            pos += 1
        o_ref, lse_ref, m_sc, l_sc, acc_sc = refs[pos:]
        i, s = pl.program_id(2), pl.program_id(3)
        j = att.other(i, s)

        @pl.when(s == 0)
        def _():
            if sink_ref is not None:
                m_sc[...] = jnp.broadcast_to(sink_ref[...], (t, HD))
                l_sc[...] = jnp.ones((t, HD), F32)
            else:
                m_sc[...] = jnp.full((t, HD), NEG_INF, F32)
                l_sc[...] = jnp.zeros((t, HD), F32)
            acc_sc[...] = jnp.zeros((t, HD), F32)

        @pl.when((j >= 0) & (j < att.nb))
        def _():
            sc = _scores(att, q_ref[...], k_ref[...], bias_ref, i, j)
            m_prev = m_sc[...]
            m_new = jnp.maximum(m_prev, jnp.max(sc, axis=1, keepdims=True))
            alpha = jnp.exp(m_prev - m_new)
            p = jnp.exp(sc - m_new[:, :1])
            l_sc[...] = alpha * l_sc[...] + jnp.sum(p, axis=1, keepdims=True)
            acc_sc[...] = alpha * acc_sc[...] + jnp.dot(p.astype(BF16), v_ref[...], preferred_element_type=F32)
            m_sc[...] = m_new

        @pl.when(s == att.nsteps - 1)
        def _():
            o_ref[...] = (acc_sc[...] / l_sc[...]).astype(o_ref.dtype)
            lse_ref[...] = m_sc[...] + jnp.log(l_sc[...])

    in_specs = [
        pl.BlockSpec((t, HD), lambda r, h, i, s: (i, r * att.qstr + att.qcol + h)),
        pl.BlockSpec((t, HD), lambda r, h, i, s: (att.other_clamped(i, s), r * att.kstr + att.kcol + h // G)),
        pl.BlockSpec((t, HD), lambda r, h, i, s: (att.other_clamped(i, s), r * att.vstr + att.vcol + h // G)),
    ]
    args = [q, k, v]
    if sink is not None:
        in_specs.append(pl.BlockSpec((None, 1, HD), lambda r, h, i, s: (h, 0, 0)))
        args.append(sink)
    if bias is not None:
        in_specs.append(pl.BlockSpec((None, None, t, t), lambda r, h, i, s: (h, s, 0, 0)))
        args.append(bias)
    o_spec = pl.BlockSpec((t, HD), lambda r, h, i, s: (i, r * H + h))
    cols = att.R * H * HD
    return _pc(body, name=name, grid=(att.R, H, att.nb, att.nsteps),
               out_shape=[_sds((att.L, cols), BF16), _sds((att.L, cols), F32)],
               in_specs=in_specs, out_specs=[o_spec, o_spec],
               scratch_shapes=[pltpu.VMEM((t, HD), F32)] * 3,
               compiler_params=_cparams(("parallel", "parallel", "parallel", "arbitrary")))(*args)


def _attn_dq(name, att, q, k, v, do, lse, delta, bias=None):
    t, H, G = att.t, att.H, att.G

    def body(*refs):
        q_ref, k_ref, v_ref, do_ref, lse_ref, dl_ref = refs[:6]
        pos = 6
        bias_ref = dtab_ref = None
        if bias is not None:
            bias_ref = refs[pos]
            pos += 1
        dq_ref = refs[pos]
        pos += 1
        if bias is not None:
            dtab_ref = refs[pos]
            pos += 1
        acc_sc = refs[pos]
        i, s = pl.program_id(2), pl.program_id(3)
        j = att.other(i, s)

        @pl.when(s == 0)
        def _():
            acc_sc[...] = jnp.zeros((t, HD), F32)

        if dtab_ref is not None:
            @pl.when((i == 0) & (s == 0))
            def _():
                dtab_ref[...] = jnp.zeros(dtab_ref.shape, F32)

        @pl.when((j >= 0) & (j < att.nb))
        def _():
            kk = k_ref[...]
            sc = _scores(att, q_ref[...], kk, bias_ref, i, j)
            p = jnp.exp(sc - lse_ref[...][:, :1])
            dp = lax.dot_general(do_ref[...], v_ref[...], _NT, preferred_element_type=F32)
            ds = p * (dp - dl_ref[...][:, :1])
            if dtab_ref is not None:
                dtab_ref[s] += ds
            acc_sc[...] += jnp.dot(ds.astype(BF16), kk, preferred_element_type=F32)

        @pl.when(s == att.nsteps - 1)
        def _():
            dq_ref[...] = acc_sc[...] * SCALE

    row = pl.BlockSpec((t, HD), lambda r, h, i, s: (i, r * H + h))
    in_specs = [
        pl.BlockSpec((t, HD), lambda r, h, i, s: (i, r * att.qstr + att.qcol + h)),
        pl.BlockSpec((t, HD), lambda r, h, i, s: (att.other_clamped(i, s), r * att.kstr + att.kcol + h // G)),
        pl.BlockSpec((t, HD), lambda r, h, i, s: (att.other_clamped(i, s), r * att.vstr + att.vcol + h // G)),
        row, row, row,
    ]
    args = [q, k, v, do, lse, delta]
    cols = att.R * H * HD
    out_shape = [_sds((att.L, cols), F32)]
    out_specs = [row]
    sem = ("parallel", "parallel", "parallel", "arbitrary")
    if bias is not None:
        in_specs.append(pl.BlockSpec((None, None, t, t), lambda r, h, i, s: (h, s, 0, 0)))
        args.append(bias)
        out_shape.append(_sds((H, att.nsteps, t, t), F32))
        out_specs.append(pl.BlockSpec((None, att.nsteps, t, t), lambda r, h, i, s: (h, 0, 0, 0)))
        sem = ("parallel", "parallel", "arbitrary", "arbitrary")
    return _pc(body, name=name, grid=(att.R, H, att.nb, att.nsteps), out_shape=out_shape,
               in_specs=in_specs, out_specs=out_specs, scratch_shapes=[pltpu.VMEM((t, HD), F32)],
               compiler_params=_cparams(sem))(*args)


def _attn_dkv(name, att, q, k, v, do, lse, delta, bias=None):
    t, H, G = att.t, att.H, att.G
    HK = H // G

    def body(*refs):
        q_ref, k_ref, v_ref, do_ref, lse_ref, dl_ref = refs[:6]
        pos = 6
        bias_ref = None
        if bias is not None:
            bias_ref = refs[pos]
            pos += 1
        dk_ref, dv_ref, dk_sc, dv_sc = refs[pos:]
        j, g, s = pl.program_id(2), pl.program_id(3), pl.program_id(4)
        i = att.other(j, s)

        @pl.when((g == 0) & (s == 0))
        def _():
            dk_sc[...] = jnp.zeros((t, HD), F32)
            dv_sc[...] = jnp.zeros((t, HD), F32)

        @pl.when((i >= 0) & (i < att.nb))
        def _():
            qq, dd = q_ref[...], do_ref[...]
            sc = _scores(att, qq, k_ref[...], bias_ref, i, j)
            p = jnp.exp(sc - lse_ref[...][:, :1])
            dp = lax.dot_general(dd, v_ref[...], _NT, preferred_element_type=F32)
            ds = p * (dp - dl_ref[...][:, :1])
            dv_sc[...] += jnp.dot(p.T.astype(BF16), dd, preferred_element_type=F32)
            dk_sc[...] += jnp.dot(ds.T.astype(BF16), qq, preferred_element_type=F32)

        @pl.when((g == G - 1) & (s == att.nsteps - 1))
        def _():
            dk_ref[...] = dk_sc[...] * SCALE
            dv_ref[...] = dv_sc[...]

    def qrow(col):
        return pl.BlockSpec((t, HD), lambda r, hk, j, g, s: (att.other_clamped(j, s), col(r, hk * G + g)))

    in_specs = [
        qrow(lambda r, h: r * att.qstr + att.qcol + h),
        pl.BlockSpec((t, HD), lambda r, hk, j, g, s: (j, r * att.kstr + att.kcol + hk)),
        pl.BlockSpec((t, HD), lambda r, hk, j, g, s: (j, r * att.vstr + att.vcol + hk)),
        qrow(lambda r, h: r * H + h), qrow(lambda r, h: r * H + h), qrow(lambda r, h: r * H + h),
    ]
    args = [q, k, v, do, lse, delta]
    if bias is not None:
        in_specs.append(pl.BlockSpec((None, None, t, t), lambda r, hk, j, g, s: (hk * G + g, 2 * att.off - s, 0, 0)))
        args.append(bias)
    o_spec = pl.BlockSpec((t, HD), lambda r, hk, j, g, s: (j, r * HK + hk))
    cols = att.R * HK * HD
    return _pc(body, name=name, grid=(att.R, HK, att.nb, G, att.nsteps),
               out_shape=[_sds((att.L, cols), F32)] * 2, in_specs=in_specs, out_specs=[o_spec, o_spec],
               scratch_shapes=[pltpu.VMEM((t, HD), F32)] * 2,
               compiler_params=_cparams(("parallel", "parallel", "parallel", "arbitrary", "arbitrary")))(*args)


def _row_call(name, body, T, tt, ins, outs, acc_outs=()):
    in_specs, args = [], []
    for item in ins:
        a = item[0]
        if item[1] is None:
            in_specs.append(pl.BlockSpec(a.shape, lambda i, nd=a.ndim: (0,) * nd))
        else:
            in_specs.append(pl.BlockSpec((tt, item[1]), lambda i, cb=item[2]: (i, cb)))
        args.append(a)
    out_shape = [_sds((T, c), d) for c, d in outs] + [_sds(s, F32) for s in acc_outs]
    out_specs = [pl.BlockSpec((tt, c), lambda i: (i, 0)) for c, _ in outs]
    out_specs += [pl.BlockSpec(s, lambda i, nd=len(s): (0,) * nd) for s in acc_outs]
    sem = ("arbitrary",) if acc_outs else ("parallel",)
    return _pc(body, name=name, grid=(T // tt,), out_shape=out_shape, in_specs=in_specs, out_specs=out_specs,
               compiler_params=_cparams(sem))(*args)


def _acc(ref, val):
    @pl.when(pl.program_id(0) == 0)
    def _():
        ref[...] = val

    @pl.when(pl.program_id(0) > 0)
    def _():
        ref[...] += val


def _prenorm_fwd(name, x, g, scale, shift):
    T, D = x.shape

    def body(x_ref, g_ref, sc_ref, sh_ref, h_ref):
        xf = x_ref[...]
        r = lax.rsqrt(jnp.mean(xf * xf, axis=1, keepdims=True) + EPS)
        h_ref[...] = ((xf * r) * g_ref[...] * (1.0 + sc_ref[...]) + sh_ref[...]).astype(BF16)

    return _row_call(name, body, T, min(T, 256), [(x, D, 0), (g, None), (scale, None), (shift, None)], [(D, BF16)])[0]


def _prenorm_bwd(name, x, dh, dxn, g, scale):
    T, D = x.shape

    def body(x_ref, dh_ref, dxn_ref, g_ref, sc_ref, dx_ref, dsh_ref, dsc_ref, dg_ref):
        xf, dh_, gg = x_ref[...], dh_ref[...], g_ref[...]
        r = lax.rsqrt(jnp.mean(xf * xf, axis=1, keepdims=True) + EPS)
        u = xf * r
        dn = dh_ * (1.0 + sc_ref[...])
        du = dn * gg
        dx_ref[...] = dxn_ref[...] + r * (du - u * jnp.mean(du * u, axis=1, keepdims=True))
        _acc(dsh_ref, jnp.sum(dh_, axis=0, keepdims=True))
        _acc(dsc_ref, jnp.sum(dh_ * (u * gg), axis=0, keepdims=True))
        _acc(dg_ref, jnp.sum(dn * u, axis=0, keepdims=True))

    return _row_call(name, body, T, min(T, 256), [(x, D, 0), (dh, D, 0), (dxn, D, 0), (g, None), (scale, None)],
                     [(D, F32)], [(1, D)] * 3)


def _resid_fwd(name, x, out, gate):
    T, D = x.shape

    def body(x_ref, o_ref, g_ref, y_ref):
        y_ref[...] = x_ref[...] + g_ref[...] * o_ref[...]

    return _row_call(name, body, T, min(T, 256), [(x, D, 0), (out, D, 0), (gate, None)], [(D, F32)])[0]


def _resid_bwd(name, dx, out, gate):
    T, D = dx.shape

    def body(dx_ref, o_ref, g_ref, do_ref, dg_ref):
        d = dx_ref[...]
        do_ref[...] = (d * g_ref[...]).astype(BF16)
        _acc(dg_ref, jnp.sum(d * o_ref[...], axis=0, keepdims=True))

    return _row_call(name, body, T, min(T, 256), [(dx, D, 0), (out, D, 0), (gate, None)], [(D, BF16)], [(1, D)])


def _final(name, x, tgt, g):
    T, D = x.shape

    def body(x_ref, t_ref, g_ref, dx_ref, loss_ref, dg_ref):
        xf, gg = x_ref[...], g_ref[...]
        r = lax.rsqrt(jnp.mean(xf * xf, axis=1, keepdims=True) + EPS)
        u = xf * r
        err = u * gg - t_ref[...]
        part = 0.5 * jnp.sum(jnp.mean(err * err, axis=1, keepdims=True), axis=0, keepdims=True)
        _acc(loss_ref, jnp.broadcast_to(part, (1, HD)))
        dy = err * (1.0 / D)
        _acc(dg_ref, jnp.sum(dy * u, axis=0, keepdims=True))
        du = dy * gg
        dx_ref[...] = r * (du - u * jnp.mean(du * u, axis=1, keepdims=True))

    return _row_call(name, body, T, min(T, 256), [(x, D, 0), (tgt, D, 0), (g, None)], [(D, F32)], [(1, HD), (1, D)])


def _roll_pair(x, shift):
    lanes = lax.broadcasted_iota(jnp.int32, x.shape, 1)
    r1 = pltpu.roll(x, shift, 1)
    r2 = pltpu.roll(x, HD - shift, 1)
    src = pltpu.roll(lanes, shift, 1)
    is_plus = src == ((lanes + shift) & (HD - 1))
    return jnp.where(is_plus, r1, r2), jnp.where(is_plus, r2, r1)


def _rope1(x, cos, ss):
    xp, _ = _roll_pair(x, 64)
    return x * cos + xp * ss


def _rope1_t(d, cos, ss):
    dp, _ = _roll_pair(d * ss, 64)
    return d * cos + dp


def _ropex(x, cos, sa, sb):
    xp, xm = _roll_pair(x, 32)
    return x * cos + xp * sa + xm * sb


def _ropex_t(d, cos, sa, sb):
    _, am = _roll_pair(d * sa, 32)
    bp, _ = _roll_pair(d * sb, 32)
    return d * cos + am + bp


def _qk_prep(name, qkvg, tabs, cqn, ckn):
    T = qkvg.shape[0]
    cos1, ss1, cosx, sax, sbx = tabs

    def body(aq, ak, bq, bk, cq, ck, c1, s1, cx, ax, bx, gq, gk, oaq, oak, obq, obk, ocq, ock):
        c1v, s1v = c1[...], s1[...]
        for src, dst, nh in ((aq, oaq, 4), (ak, oak, 2), (bq, obq, 4), (bk, obk, 4)):
            for h in range(nh):
                sl = slice(h * HD, (h + 1) * HD)
                dst[:, sl] = _rope1(src[:, sl].astype(F32), c1v, s1v).astype(BF16)
        cxv, axv, bxv = cx[...], ax[...], bx[...]
        for src, dst, nh, gref in ((cq, ocq, 4, gq), (ck, ock, 2, gk)):
            for h in range(nh):
                sl = slice(h * HD, (h + 1) * HD)
                xf = src[:, sl].astype(F32)
                r = lax.rsqrt(jnp.mean(xf * xf, axis=1, keepdims=True) + EPS)
                dst[:, sl] = _ropex(xf * r * gref[...], cxv, axv, bxv).astype(BF16)

    ins = [(qkvg, 512, 0), (qkvg, 256, 2), (qkvg, 512, 3), (qkvg, 512, 4), (qkvg, 512, 7), (qkvg, 256, 16),
           (cos1, HD, 0), (ss1, HD, 0), (cosx, HD, 0), (sax, HD, 0), (sbx, HD, 0), (cqn, None), (ckn, None)]
    outs = [(512, BF16), (256, BF16), (512, BF16), (512, BF16), (512, BF16), (256, BF16)]
    return _row_call(name, body, T, min(T, 512), ins, outs)


def _silu_parts(g):
    sig = 1.0 / (1.0 + jnp.exp(-g))
    return g * sig, sig * (1.0 + g * (1.0 - sig))


def _mix_weights(l0, l1, l2):
    mx = jnp.maximum(jnp.maximum(l0, l1), l2)
    e0, e1, e2 = jnp.exp(l0 - mx), jnp.exp(l1 - mx), jnp.exp(l2 - mx)
    inv = 1.0 / (e0 + e1 + e2)
    return e0 * inv, e1 * inv, e2 * inv


def _gate_fwd(name, qkvg, oa, ob, lb, oc, od):
    T = qkvg.shape[0]

    def body(ga, gb, gc, gd, oa_r, ob0, ob1, ob2, lb0, lb1, lb2, oc_r, od_r, br):
        w0, w1, w2 = _mix_weights(lb0[...], lb1[...], lb2[...])
        yb = w0 * ob0[...].astype(F32) + w1 * ob1[...].astype(F32) + w2 * ob2[...].astype(F32)
        ys = (oa_r[...].astype(F32), yb, oc_r[...].astype(F32), od_r[...].astype(F32))
        for n, (y, g) in enumerate(zip(ys, (ga, gb, gc, gd))):
            act, _ = _silu_parts(g[...].astype(F32))
            br[:, n * 512:(n + 1) * 512] = (y * act).astype(BF16)

    ins = [(qkvg, 512, 2), (qkvg, 512, 6), (qkvg, 512, 9), (qkvg, 512, 13), (oa, 512, 0)]
    ins += [(o, 512, 0) for o in ob] + [(l_, 512, 0) for l_ in lb] + [(oc, 512, 0), (od, 512, 0)]
    return _row_call(name, body, T, min(T, 256), ins, [(2048, BF16)])[0]


def _head_rowsum(x):
    parts = []
    for h in range(x.shape[1] // HD):
        s = jnp.sum(x[:, h * HD:(h + 1) * HD], axis=1, keepdims=True)
        parts.append(jnp.broadcast_to(s, (x.shape[0], HD)))
    return jnp.concatenate(parts, axis=1)


def _gate_bwd(name, dbr, qkvg, oa, la, sink_row, ob, lb, oc, od):
    T = qkvg.shape[0]

    def body(dbr_r, ga, gb, gc, gd, oa_r, la_r, sk, ob0, ob1, ob2, lb0, lb1, lb2, oc_r, od_r,
             doa, dla, dob0, dob1, dob2, dlb0, dlb1, dlb2, doc, dlc, dod, dld, dg, dsk):
        def one(n, g_ref, y):
            act, dact = _silu_parts(g_ref[...].astype(F32))
            d = dbr_r[:, n * 512:(n + 1) * 512]
            dg[:, n * 512:(n + 1) * 512] = (d * y * dact).astype(BF16)
            return d * act

        ya = oa_r[...].astype(F32)
        dya = one(0, ga, ya)
        doa[...] = dya.astype(BF16)
        dl_a = _head_rowsum(dya * ya)
        dla[...] = dl_a
        _acc(dsk, -jnp.sum(jnp.exp(sk[...] - la_r[...]) * dl_a, axis=0, keepdims=True))

        w = _mix_weights(lb0[...], lb1[...], lb2[...])
        obs = (ob0[...].astype(F32), ob1[...].astype(F32), ob2[...].astype(F32))
        yb = w[0] * obs[0] + w[1] * obs[1] + w[2] * obs[2]
        dyb = one(1, gb, yb)
        rs = _head_rowsum(dyb * yb)
        for wp, do_ref, dl_ref in zip(w, (dob0, dob1, dob2), (dlb0, dlb1, dlb2)):
            do_ref[...] = (wp * dyb).astype(BF16)
            dl_ref[...] = wp * rs

        for n, g_ref, o_r, do_ref, dl_ref in ((2, gc, oc_r, doc, dlc), (3, gd, od_r, dod, dld)):
            y = o_r[...].astype(F32)
            dy = one(n, g_ref, y)
            do_ref[...] = dy.astype(BF16)
            dl_ref[...] = _head_rowsum(dy * y)

    ins = [(dbr, 2048, 0), (qkvg, 512, 2), (qkvg, 512, 6), (qkvg, 512, 9), (qkvg, 512, 13),
           (oa, 512, 0), (la, 512, 0), (sink_row, None)]
    ins += [(o, 512, 0) for o in ob] + [(l_, 512, 0) for l_ in lb] + [(oc, 512, 0), (od, 512, 0)]
    outs = [(512, BF16), (512, F32)] + [(512, BF16)] * 3 + [(512, F32)] * 3 + [(512, BF16), (512, F32)] * 2
    outs += [(2048, BF16)]
    return _row_call(name, body, T, min(T, 256), ins, outs, [(1, 512)])


def _merge_fwd(name, mgl, proj, D):
    T = mgl.shape[0]

    def body(m_ref, p_ref, o_ref):
        acc = None
        for n in range(4):
            sl = slice(n * D, (n + 1) * D)
            sig = 1.0 / (1.0 + jnp.exp(-m_ref[:, sl].astype(F32)))
            term = sig * p_ref[:, sl].astype(F32)
            acc = term if acc is None else acc + term
        o_ref[...] = acc.astype(BF16)

    return _row_call(name, body, T, min(T, 256), [(mgl, 4 * D, 0), (proj, 4 * D, 0)], [(D, BF16)])[0]


def _merge_bwd(name, dm, mgl, proj, D):
    T = mgl.shape[0]

    def body(d_ref, m_ref, p_ref, dp_ref, dl_ref):
        d = d_ref[...]
        for n in range(4):
            sl = slice(n * D, (n + 1) * D)
            sig = 1.0 / (1.0 + jnp.exp(-m_ref[:, sl].astype(F32)))
            dp_ref[:, sl] = (d * sig).astype(BF16)
            dl_ref[:, sl] = (d * p_ref[:, sl].astype(F32) * sig * (1.0 - sig)).astype(BF16)

    return _row_call(name, body, T, min(T, 128), [(dm, D, 0), (mgl, 4 * D, 0), (proj, 4 * D, 0)],
                     [(4 * D, BF16), (4 * D, BF16)])


def _qkvg_bwd(name, qkvg, tabs, cqn, ckn, dA, dB, dC, dD, dg):
    T = qkvg.shape[0]
    cos1, ss1, cosx, sax, sbx = tabs

    def body(cq_r, ck_r, c1, s1, cx, ax, bx, gq, gk,
             dqa, dka, dva, dqb0, dqb1, dqb2, dkb0, dkb1, dkb2, dvb0, dvb1, dvb2,
             dqc, dkc, dvc, dqd, dkd, dvd, dg_r, out, dgq, dgk):
        c1v, s1v = c1[...], s1[...]

        def put(col, val):
            out[:, col * HD:(col + 1) * HD] = val.astype(BF16)

        for h in range(4):
            sl = slice(h * HD, (h + 1) * HD)
            put(A_Q + h, _rope1_t(dqa[:, sl], c1v, s1v))
            put(B_Q + h, _rope1_t(dqb0[:, sl] + dqb1[:, sl] + dqb2[:, sl], c1v, s1v))
            put(B_K + h, _rope1_t(dkb0[:, sl] + dkb1[:, sl] + dkb2[:, sl], c1v, s1v))
            put(B_V + h, dvb0[:, sl] + dvb1[:, sl] + dvb2[:, sl])
            put(D_Q + h, dqd[:, sl])
            put(D_K + h, dkd[:, sl])
            put(D_V + h, dvd[:, sl])
        for h in range(2):
            sl = slice(h * HD, (h + 1) * HD)
            put(A_K + h, _rope1_t(dka[:, sl], c1v, s1v))
            put(A_V + h, dva[:, sl])
            put(C_V + h, dvc[:, sl])
        cxv, axv, bxv = cx[...], ax[...], bx[...]
        for src, dref, col, nh, gref, dgref in ((cq_r, dqc, C_Q, 4, gq, dgq), (ck_r, dkc, C_K, 2, gk, dgk)):
            gsum = None
            for h in range(nh):
                sl = slice(h * HD, (h + 1) * HD)
                xf = src[:, sl].astype(F32)
                r = lax.rsqrt(jnp.mean(xf * xf, axis=1, keepdims=True) + EPS)
                u = xf * r
                dy = _ropex_t(dref[:, sl], cxv, axv, bxv)
                du = dy * gref[...]
                put(col + h, r * (du - u * jnp.mean(du * u, axis=1, keepdims=True)))
                part = jnp.sum(dy * u, axis=0, keepdims=True)
                gsum = part if gsum is None else gsum + part
            _acc(dgref, gsum)
        for n, col in enumerate((A_G, B_G, C_G, D_G)):
            out[:, col * HD:(col + 4) * HD] = dg_r[:, n * 512:(n + 1) * 512]

    ins = [(qkvg, 512, 7), (qkvg, 256, 16), (cos1, HD, 0), (ss1, HD, 0), (cosx, HD, 0), (sax, HD, 0), (sbx, HD, 0),
           (cqn, None), (ckn, None)]
    ins += [(dA[0], 512, 0), (dA[1], 256, 0), (dA[2], 256, 0)]
    ins += [(a, 512, 0) for a in dB[0]] + [(a, 512, 0) for a in dB[1]] + [(a, 512, 0) for a in dB[2]]
    ins += [(dC[0], 512, 0), (dC[1], 256, 0), (dC[2], 256, 0)]
    ins += [(dD[0], 512, 0), (dD[1], 512, 0), (dD[2], 512, 0), (dg, 2048, 0)]
    return _row_call(name, body, T, min(T, 256), ins, [(IN_HEADS * HD, BF16)], [(1, HD), (1, HD)])


def _ada_fwd(name, c_pad, w_ada):
    L, D, n = w_ada.shape
    P = c_pad.shape[0]

    def body(c_ref, w_ref, o_ref):
        cc = c_ref[...]
        cond = cc / (1.0 + jnp.exp(-cc))
        o_ref[...] = jnp.dot(cond.astype(BF16), w_ref[...].astype(BF16), preferred_element_type=F32)

    return _pc(body, name=name, grid=(L,), out_shape=_sds((L, P, n), F32),
               in_specs=[pl.BlockSpec((P, D), lambda l: (0, 0)), pl.BlockSpec((None, D, n), lambda l: (l, 0, 0))],
               out_specs=pl.BlockSpec((None, P, n), lambda l: (l, 0, 0)),
               compiler_params=_cparams(("parallel",)))(c_pad, w_ada)


def _ada_grad(name, c_pad, d_ada):
    L, P, n = d_ada.shape
    D = c_pad.shape[1]

    def body(c_ref, d_ref, o_ref):
        cc = c_ref[...]
        cond = cc / (1.0 + jnp.exp(-cc))
        o_ref[...] = lax.dot_general(cond.astype(BF16), d_ref[...].astype(BF16), _TN, preferred_element_type=F32)

    return _pc(body, name=name, grid=(L,), out_shape=_sds((L, D, n), F32),
               in_specs=[pl.BlockSpec((P, D), lambda l: (0, 0)), pl.BlockSpec((None, P, n), lambda l: (l, 0, 0))],
               out_specs=pl.BlockSpec((None, D, n), lambda l: (l, 0, 0)),
               compiler_params=_cparams(("parallel",)))(c_pad, d_ada)


def _adamw(name, w, m, v, g, slots):
    rows, cols = w.shape
    tr = rows
    cap = max(8, (2 ** 20) // (4 * cols))
    if rows > cap:
        tr = 8
        while tr * 2 <= cap and rows % (tr * 2) == 0:
            tr *= 2
    c1 = 1.0 / (1.0 - ADAM_B1 ** ADAM_STEP)
    c2 = 1.0 / (1.0 - ADAM_B2 ** ADAM_STEP)

    def body(w_ref, m_ref, v_ref, g_ref, go_ref, d_ref, mo_ref, vo_ref):
        if slots:
            gg = g_ref[0].astype(F32)
            for k in range(1, slots):
                gg = gg + g_ref[k].astype(F32)
        else:
            gg = g_ref[...]
        mn = ADAM_B1 * m_ref[...] + (1.0 - ADAM_B1) * gg
        vn = ADAM_B2 * v_ref[...] + (1.0 - ADAM_B2) * (gg * gg)
        go_ref[...] = gg
        mo_ref[...] = mn
        vo_ref[...] = vn
        d_ref[...] = -ADAM_LR * ((mn * c1) / (jnp.sqrt(vn * c2) + ADAM_EPS) + ADAM_WD * w_ref[...])

    blk = pl.BlockSpec((tr, cols), lambda i: (i, 0))
    gspec = pl.BlockSpec((slots, tr, cols), lambda i: (0, i, 0)) if slots else blk
    return _pc(body, name=name, grid=(rows // tr,), out_shape=[_sds((rows, cols), F32)] * 4,
               in_specs=[blk, blk, blk, gspec], out_specs=[blk] * 4,
               compiler_params=_cparams(("parallel",)))(w, m, v, g)


def _rope_tables(T):
    pos = np.arange(T)
    lane = np.arange(HD)
    inv = ROPE_THETA ** (-np.arange(0, HD, 2, dtype=np.float32) / HD)
    ang = pos.astype(np.float32)[:, None] * inv[None, :]
    ang = np.concatenate([ang, ang], axis=-1)
    cos1 = np.cos(ang)
    ss1 = np.sin(ang) * np.where(lane < HD // 2, -1.0, 1.0)[None, :]
    half = HD // 2
    invh = ROPE_THETA ** (-np.arange(0, half, 2, dtype=np.float32) / half)

    def tab(p):
        a = p.astype(np.float32)[:, None] * invh[None, :]
        return np.concatenate([a, a], axis=-1)

    angx = np.concatenate([tab(pos // GRID_W), tab(pos % GRID_W)], axis=-1)
    cosx, sinx = np.cos(angx), np.sin(angx)
    first = (lane % half) < half // 2
    sax = np.where(first[None, :], -sinx, 0.0)
    sbx = np.where(first[None, :], 0.0, sinx)
    return tuple(jnp.asarray(a, F32) for a in (cos1, ss1, cosx, sax, sbx))


def _na_selectors(t, off):
    rpt = t // GRID_W
    nst = 2 * off + 1
    a_r = np.arange(rpt)
    dr = (np.arange(nst)[:, None, None] - off) * rpt + a_r[None, None, :] - a_r[None, :, None]
    row_sel = (np.clip(dr, -(NA_ROWS - 1), NA_ROWS - 1)[..., None] + NA_ROWS - 1 == np.arange(2 * NA_ROWS - 1)).astype(np.float32)
    col = np.arange(GRID_W)
    dc = np.clip(col[None, :] - col[:, None], -(NA_COLS - 1), NA_COLS - 1)
    col_sel = (dc[..., None] + NA_COLS - 1 == np.arange(2 * NA_COLS - 1)).astype(np.float32)
    return jnp.asarray(row_sel), jnp.asarray(col_sel)


def _na_bias_table(rel_bias, t, off):
    row_sel, col_sel = _na_selectors(t, off)
    hp = lax.Precision.HIGHEST
    tab = jnp.einsum("sabi,hij,cdj->hsacbd", row_sel, rel_bias, col_sel, precision=hp)
    return tab.reshape(rel_bias.shape[0], 2 * off + 1, t, t)


def _na_bias_grad(dtab, t, off):
    row_sel, col_sel = _na_selectors(t, off)
    rpt = t // GRID_W
    hp = lax.Precision.HIGHEST
    d6 = dtab.reshape(dtab.shape[0], 2 * off + 1, rpt, GRID_W, rpt, GRID_W)
    return jnp.einsum("sabi,hsacbd,cdj->hij", row_sel, d6, col_sel, precision=hp)


def _att_specs(T):
    rows = T // GRID_W
    specs = {
        "A": _Att(T, 1, 4, 2, 0, 0, A_V, 0, 0, 0, "band", reach=A_REACH),
        "C": _Att(T, 1, 4, 2, 0, 0, C_V, 0, 0, 0, "dense"),
        "D": _Att(T, 1, 4, 1, D_Q, D_K, D_V, 0, 0, 0, "na", reach=ATT_TILE, rows=rows),
    }
    for p, (window, dil) in enumerate(B_PATTERNS):
        specs["B%d" % p] = _Att(T // dil, dil, 4, 1, 0, 0, B_V, 4, 4, IN_HEADS, "band", reach=(window // 2) // dil)
    return specs


def kernel(x, c, norm_g, w_ada, b_ada, w_in, a_sink, c_q_norm, c_k_norm, d_rel_bias, w_gate_merge, w_branch, w_out, final_g, loss_target, m_norm_g, m_w_ada, m_b_ada, m_w_in, m_a_sink, m_c_q_norm, m_c_k_norm, m_d_rel_bias, m_w_gate_merge, m_w_branch, m_w_out, m_final_g, v_norm_g, v_w_ada, v_b_ada, v_w_in, v_a_sink, v_c_q_norm, v_c_k_norm, v_d_rel_bias, v_w_gate_merge, v_w_branch, v_w_out, v_final_g):
    T, D = x.shape[1], x.shape[2]
    NL = norm_g.shape[0]
    x0 = x.reshape(T, D)
    tgt = loss_target.reshape(T, D)
    me = 4 * lax.axis_index("x") + 2 * lax.axis_index("y") + lax.axis_index("c")
    att = _att_specs(T)
    tabs = _rope_tables(T)
    n_ada = w_ada.shape[2]

    (c_all,) = _exchange("gather_c", [c], gather=True)
    c_pad = jnp.pad(c_all.reshape(N_DEV, D), ((0, HD - N_DEV), (0, 0)))
    ada_part = _ada_fwd("ada_fwd", c_pad, w_ada)[:, :N_DEV]
    (ada_all,) = _exchange("gather_ada", [ada_part], gather=True)
    ada_mine = lax.dynamic_index_in_dim(ada_all, me, axis=2, keepdims=False)
    ada = jnp.transpose(ada_mine, (1, 0, 2)).reshape(NL, N_DEV * n_ada) + b_ada
    shift, scale, gate = ada[:, :D], ada[:, D:2 * D], ada[:, 2 * D:]

    wi_all, wg_all, wb_all, wo_all = _exchange(
        "gather_weights", [w_in.astype(BF16), w_gate_merge.astype(BF16), w_branch.astype(BF16), w_out.astype(BF16)],
        gather=True)

    na_off = att["D"].off
    saved = []
    xl = x0
    for l in range(NL):
        Wi, Wg, Wb = wi_all[:, l], wg_all[:, l], wb_all[:, l]
        Wo = wo_all[:, l].reshape(1, D, D)
        sc_l, sh_l, gt_l = scale[l][None], shift[l][None], gate[l][None]
        ng_l = norm_g[l][None]
        cqn, ckn = c_q_norm[l][None], c_k_norm[l][None]
        sink3 = jnp.broadcast_to(a_sink[l][:, None, None], (4, 1, HD))
        btab = _na_bias_table(d_rel_bias[l], att["D"].t, na_off)

        h = _prenorm_fwd("prenorm_fwd", xl, ng_l, sc_l, sh_l)
        qkvg = _mm_nn("mm_in", h, Wi, BF16)
        mgl = _mm_nn("mm_gm", h, Wg, BF16)
        qa, ka, qb, kb, qc, kc = _qk_prep("qk_prep", qkvg, tabs, cqn, ckn)
        oa, la = _attn_fwd("attn_a_fwd", att["A"], qa, ka, qkvg, sink=sink3)
        ob, lb = [], []
        for p, (_, dil) in enumerate(B_PATTERNS):
            a_ = att["B%d" % p]
            o_, l_ = _attn_fwd("attn_b%d_fwd" % p, a_, qb.reshape(T // dil, -1), kb.reshape(T // dil, -1),
                               qkvg.reshape(T // dil, -1))
            ob.append(o_.reshape(T, 512))
            lb.append(l_.reshape(T, 512))
        oc, lc = _attn_fwd("attn_c_fwd", att["C"], qc, kc, qkvg)
        od, ld = _attn_fwd("attn_d_fwd", att["D"], qkvg, qkvg, qkvg, bias=btab)
        br = _gate_fwd("gate_fwd", qkvg, oa, ob, lb, oc, od)
        proj = _branch_nn("mm_branch", br, Wb, BF16)
        merged = _merge_fwd("merge_fwd", mgl, proj, D)
        out = _mm_nn("mm_out", merged, Wo, F32)
        xn = _resid_fwd("resid_fwd", xl, out, gt_l)
        saved.append(dict(x=xl, h=h, qkvg=qkvg, mgl=mgl, qa=qa, ka=ka, qb=qb, kb=kb, qc=qc, kc=kc, oa=oa, la=la, ob=ob,
                          lb=lb, oc=oc, lc=lc, od=od, ld=ld, br=br, proj=proj, merged=merged, out=out, btab=btab))
        xl = xn

    dx, loss_part, d_final_g = _final("final_loss", xl, tgt, final_g[None])

    g_in, g_gm, g_br, g_out = [None] * NL, [None] * NL, [None] * NL, [None] * NL
    d_norm_g, d_ada, d_sink, d_cq, d_ck, d_bias = [None] * NL, [None] * NL, [None] * NL, [None] * NL, [None] * NL, [None] * NL
    for l in reversed(range(NL)):
        S = saved[l]
        Wi, Wg, Wb = wi_all[:, l], wg_all[:, l], wb_all[:, l]
        Wo = wo_all[:, l].reshape(1, D, D)
        sc_l, gt_l, ng_l = scale[l][None], gate[l][None], norm_g[l][None]
        cqn, ckn = c_q_norm[l][None], c_k_norm[l][None]
        sink_row = jnp.repeat(a_sink[l], HD)[None]
        qkvg = S["qkvg"]

        dout, dgate = _resid_bwd("resid_bwd", dx, S["out"], gt_l)
        g_out[l] = _mm_tn("mm_out_dw", S["merged"], dout, 1, BF16).reshape(N_DEV, D // N_DEV, D)
        dm = _mm_nt("mm_out_dx", dout, Wo, F32)
        dproj, dmgl = _merge_bwd("merge_bwd", dm, S["mgl"], S["proj"], D)
        g_br[l] = _branch_tn("mm_branch_dw", S["br"], dproj, D // N_DEV, BF16)
        dbr = _branch_nt("mm_branch_dx", dproj, Wb, F32)
        (doa, dla, dob0, dob1, dob2, dlb0, dlb1, dlb2, doc, dlc, dod, dld, dg, dsk) = _gate_bwd(
            "gate_bwd", dbr, qkvg, S["oa"], S["la"], sink_row, S["ob"], S["lb"], S["oc"], S["od"])
        d_sink[l] = dsk.reshape(4, HD)[:, 0]

        (dqa,) = _attn_dq("attn_a_dq", att["A"], S["qa"], S["ka"], qkvg, doa, S["la"], dla)
        dka, dva = _attn_dkv("attn_a_dkv", att["A"], S["qa"], S["ka"], qkvg, doa, S["la"], dla)
        dqb, dkb, dvb = [], [], []
        for p, (dob, dlb) in enumerate(((dob0, dlb0), (dob1, dlb1), (dob2, dlb2))):
            dil = B_PATTERNS[p][1]
            a_ = att["B%d" % p]
            L = T // dil
            args = (S["qb"].reshape(L, -1), S["kb"].reshape(L, -1), qkvg.reshape(L, -1), dob.reshape(L, -1),
                    S["lb"][p].reshape(L, -1), dlb.reshape(L, -1))
            (dq_,) = _attn_dq("attn_b%d_dq" % p, a_, *args)
            dk_, dv_ = _attn_dkv("attn_b%d_dkv" % p, a_, *args)
            dqb.append(dq_.reshape(T, 512))
            dkb.append(dk_.reshape(T, 512))
            dvb.append(dv_.reshape(T, 512))
        (dqc,) = _attn_dq("attn_c_dq", att["C"], S["qc"], S["kc"], qkvg, doc, S["lc"], dlc)
        dkc, dvc = _attn_dkv("attn_c_dkv", att["C"], S["qc"], S["kc"], qkvg, doc, S["lc"], dlc)
        dqd, dtab = _attn_dq("attn_d_dq", att["D"], qkvg, qkvg, qkvg, dod, S["ld"], dld, bias=S["btab"])
        dkd, dvd = _attn_dkv("attn_d_dkv", att["D"], qkvg, qkvg, qkvg, dod, S["ld"], dld, bias=S["btab"])
        d_bias[l] = _na_bias_grad(dtab, att["D"].t, na_off)

        dqkvg, dcq, dck = _qkvg_bwd("qkvg_bwd", qkvg, tabs, cqn, ckn, (dqa, dka, dva), (dqb, dkb, dvb),
                                    (dqc, dkc, dvc), (dqd, dkd, dvd), dg)
        d_cq[l], d_ck[l] = dcq[0], dck[0]
        g_in[l] = _mm_tn("mm_in_dw", S["h"], dqkvg, N_DEV, BF16)
        g_gm[l] = _mm_tn("mm_gm_dw", S["h"], dmgl, N_DEV, BF16)
        dh = _mm_nt("mm_in_dx", dqkvg, Wi, F32)
        dh = _mm_nt("mm_gm_dx", dmgl, Wg, F32, add=dh)
        dx, dshift, dscale, dng = _prenorm_bwd("prenorm_bwd", S["x"], dh, dx, ng_l, sc_l)
        d_norm_g[l] = dng[0]
        d_ada[l] = jnp.concatenate([dshift[0], dscale[0], dgate[0]])

    grad_x = dx.reshape(1, T, D)

    s_in, s_gm, s_br, s_out = _exchange(
        "scatter_grads", [jnp.stack(g_in, 1), jnp.stack(g_gm, 1), jnp.stack(g_br, 1), jnp.stack(g_out, 1)], gather=False)

    def big(name, w, m, v, slots):
        cols = w.shape[-1]
        res = _adamw(name, w.reshape(-1, cols), m.reshape(-1, cols), v.reshape(-1, cols),
                     slots.reshape(N_DEV, -1, cols), N_DEV)
        return [r.reshape(w.shape) for r in res]

    r_in = big("adamw_in", w_in, m_w_in, v_w_in, s_in)
    r_gm = big("adamw_gm", w_gate_merge, m_w_gate_merge, v_w_gate_merge, s_gm)
    r_br = big("adamw_branch", w_branch, m_w_branch, v_w_branch, s_br)
    r_out = big("adamw_out", w_out, m_w_out, v_w_out, s_out)

    small_w = [norm_g, b_ada, a_sink, c_q_norm, c_k_norm, d_rel_bias, final_g]
    small_m = [m_norm_g, m_b_ada, m_a_sink, m_c_q_norm, m_c_k_norm, m_d_rel_bias, m_final_g]
    small_v = [v_norm_g, v_b_ada, v_a_sink, v_c_q_norm, v_c_k_norm, v_d_rel_bias, v_final_g]
    small_g = [jnp.stack(d_norm_g), jnp.stack(d_ada), jnp.stack(d_sink), jnp.stack(d_cq), jnp.stack(d_ck),
               jnp.stack(d_bias), d_final_g[0]]
    sizes = [int(np.prod(w.shape)) for w in small_w]
    total = sum(sizes) + HD
    rows = -(-total // (8 * HD)) * 8

    def pack(parts, extra):
        flat = jnp.concatenate([p.reshape(-1).astype(F32) for p in parts] + [extra])
        return jnp.pad(flat, (0, rows * HD - flat.shape[0])).reshape(rows, HD)

    zeros = jnp.zeros((HD,), F32)
    (g_slots,) = _exchange("gather_small", [pack(small_g, loss_part[0])], gather=True)
    gs, ds, ms, vs = _adamw("adamw_small", pack(small_w, zeros), pack(small_m, zeros), pack(small_v, zeros), g_slots, N_DEV)

    def unpack(flat2d):
        flat = flat2d.reshape(-1)
        res, o = [], 0
        for w, n in zip(small_w, sizes):
            res.append(flat[o:o + n].reshape(w.shape))
            o += n
        return res, flat[o]

    sg, loss = unpack(gs)
    sd, _ = unpack(ds)
    sm, _ = unpack(ms)
    sv, _ = unpack(vs)

    o_b = sizes[0]
    d_ada_all = g_slots.reshape(N_DEV, -1)[:, o_b:o_b + sizes[1]].reshape(N_DEV, NL, N_DEV, n_ada)
    d_ada_mine = jnp.transpose(lax.dynamic_index_in_dim(d_ada_all, me, axis=2, keepdims=False), (1, 0, 2))
    g_ada = _ada_grad("ada_grad", c_pad, jnp.pad(d_ada_mine, ((0, 0), (0, HD - N_DEV), (0, 0))))
    r_ada = _adamw("adamw_ada", w_ada.reshape(-1, n_ada), m_w_ada.reshape(-1, n_ada), v_w_ada.reshape(-1, n_ada),
                   g_ada.reshape(-1, n_ada), 0)
    r_ada = [r.reshape(w_ada.shape) for r in r_ada]

    def kind(k):
        sm_ = (sg, sd, sm, sv)[k]
        return [sm_[0], r_ada[k], sm_[1], r_in[k], sm_[2], sm_[3], sm_[4], sm_[5], r_gm[k], r_br[k], r_out[k], sm_[6]]

    return (loss, grad_x, *kind(0), *kind(1), *kind(2), *kind(3))
```

```python
import jax
import jax.numpy as jnp
import numpy as np
from jax import lax
from jax.experimental import pallas as pl
from jax.experimental.pallas import tpu as pltpu

F32 = jnp.float32
BF16 = jnp.bfloat16

N_DEV = 8
HD = 128
GRID_W = 64
EPS = 1e-6
NEG_INF = -1e30
ROPE_THETA = 10000.0
A_REACH = 128
B_PATTERNS = ((128, 1), (512, 4), (2048, 16))
NA_ROWS = 8
NA_COLS = 16
SCALE = HD ** -0.5
LOG2E = 1.4426950408889634
LN2 = 0.6931471805599453
SCALE2 = SCALE * LOG2E
A_Q, A_K, A_V, A_G = 0, 4, 6, 8
B_Q, B_K, B_V, B_G = 12, 16, 20, 24
C_Q, C_K, C_V, C_G = 28, 32, 34, 36
D_Q, D_K, D_V, D_G = 40, 44, 48, 52
IN_HEADS = 56

ADAM_LR, ADAM_B1, ADAM_B2, ADAM_EPS, ADAM_WD, ADAM_STEP = 0.001, 0.9, 0.999, 1e-08, 0.01, 10

V7X_VMEM_LIMIT = 56 * 2 ** 20
ATT_TILE = 512


def _pc(body, **kw):
    return pl.pallas_call(body, **kw)


def _cparams(sem=None):
    if sem is None:
        return pltpu.CompilerParams(vmem_limit_bytes=V7X_VMEM_LIMIT)
    return pltpu.CompilerParams(dimension_semantics=sem, vmem_limit_bytes=V7X_VMEM_LIMIT)


def _tile(n, cap):
    if n <= cap:
        return n
    t = (cap // 128) * 128
    while n % t:
        t -= 128
    return t


def _sds(shape, dtype):
    return jax.ShapeDtypeStruct(tuple(shape), dtype)


def _exchange(name, ins, gather):
    n = len(ins)
    outs = [_sds((N_DEV,) + (a.shape if gather else a.shape[1:]), a.dtype) for a in ins]

    def body(*refs):
        in_refs, out_refs = refs[:n], refs[n:2 * n]
        send_sems, recv_sems, loc_sems = refs[2 * n:]
        x, y, c = lax.axis_index("x"), lax.axis_index("y"), lax.axis_index("c")
        me = 4 * x + 2 * y + c
        local = []
        for t in range(n):
            src = in_refs[t] if gather else in_refs[t].at[me]
            cp = pltpu.make_async_copy(src, out_refs[t].at[me], loc_sems.at[t])
            cp.start()
            local.append(cp)
        remote = []
        for p in range(1, N_DEV):
            tx = 1 - x if p & 4 else x
            ty = 1 - y if p & 2 else y
            tc = 1 - c if p & 1 else c
            peer = 4 * tx + 2 * ty + tc
            for t in range(n):
                src = in_refs[t] if gather else in_refs[t].at[peer]
                cp = pltpu.make_async_remote_copy(
                    src_ref=src, dst_ref=out_refs[t].at[me],
                    send_sem=send_sems.at[t, p - 1], recv_sem=recv_sems.at[t, p - 1],
                    device_id=(tx, ty, tc), device_id_type=pl.DeviceIdType.MESH)
                cp.start()
                remote.append(cp)
        for cp in remote:
            cp.wait()
        for cp in local:
            cp.wait()

    res = _pc(
        body, name=name, out_shape=outs,
        in_specs=[pl.BlockSpec(memory_space=pl.ANY)] * n,
        out_specs=[pl.BlockSpec(memory_space=pl.ANY)] * n,
        scratch_shapes=[pltpu.SemaphoreType.DMA((n, N_DEV - 1)), pltpu.SemaphoreType.DMA((n, N_DEV - 1)),
                        pltpu.SemaphoreType.DMA((n,))],
    )(*ins)
    return list(res)


def _mm(name, a, w, out_sds, grid, a_spec, w_spec, o_spec, dims, red_axis=None, add=None, add_spec=None):
    nred = grid[red_axis] if red_axis is not None else 1
    acc_shape = tuple(d for d in o_spec.block_shape if d is not None)

    def body(*refs):
        if add is not None:
            a_ref, w_ref, add_ref, o_ref = refs[:4]
            scr = refs[4:]
        else:
            a_ref, w_ref, o_ref = refs[:3]
            scr = refs[3:]
        prod = lax.dot_general(a_ref[...], w_ref[...], dims, preferred_element_type=F32)

        def finish(acc):
            if add is not None:
                acc = acc + add_ref[...].astype(F32)
            o_ref[...] = acc.astype(o_ref.dtype)

        if nred == 1:
            finish(prod)
        else:
            acc_ref = scr[0]
            k = pl.program_id(red_axis)

            @pl.when(k == 0)
            def _():
                acc_ref[...] = prod

            @pl.when(k > 0)
            def _():
                acc_ref[...] += prod

            @pl.when(k == nred - 1)
            def _():
                finish(acc_ref[...])

    sem = tuple("arbitrary" if i == red_axis else "parallel" for i in range(len(grid)))
    in_specs = [a_spec, w_spec] + ([add_spec] if add is not None else [])
    args = (a, w) + ((add,) if add is not None else ())
    return _pc(body, name=name, out_shape=out_sds, grid=grid, in_specs=in_specs, out_specs=o_spec,
               scratch_shapes=[pltpu.VMEM(acc_shape, F32)] if nred > 1 else [],
               compiler_params=_cparams(sem))(*args)


_NN = (((1,), (0,)), ((), ()))
_NT = (((1,), (1,)), ((), ()))
_TN = (((0,), (0,)), ((), ()))


def _mm_nn(name, a, w, l, out_dtype):
    T, K = a.shape
    G, n = w.shape[0], w.shape[3]
    tm, tn = min(T, 1024), _tile(n, 1024)
    nj = n // tn
    return _mm(name, a, w, _sds((T, G * n), out_dtype), (T // tm, G, nj),
               pl.BlockSpec((tm, K), lambda i, g, j: (i, 0)),
               pl.BlockSpec((None, None, K, tn), lambda i, g, j: (g, l, 0, j)),
               pl.BlockSpec((tm, tn), lambda i, g, j: (i, g * nj + j)), _NN)


def _mm_nt(name, dy, w, l, out_dtype, add=None):
    T = dy.shape[0]
    G, K, n = w.shape[0], w.shape[2], w.shape[3]
    tm, tk = min(T, 1024), _tile(K, 1024)
    tn = _tile(n, 1024)
    nj = n // tn
    o_spec = pl.BlockSpec((tm, tk), lambda i, k, r: (i, k))
    return _mm(name, dy, w, _sds((T, K), out_dtype), (T // tm, K // tk, G * nj),
               pl.BlockSpec((tm, tn), lambda i, k, r: (i, r)),
               pl.BlockSpec((None, None, tk, tn), lambda i, k, r: (r // nj, l, k, r % nj)),
               o_spec, _NT, red_axis=2, add=add, add_spec=o_spec if add is not None else None)


def _mm_tn(name, a, dy, G, out_dtype):
    T, K = a.shape
    n = dy.shape[1] // G
    tt, tk, tn = min(T, 1024), _tile(K, 1024), _tile(n, 1024)
    nj = n // tn
    return _mm(name, a, dy, _sds((G, K, n), out_dtype), (K // tk, G, nj, T // tt),
               pl.BlockSpec((tt, tk), lambda k, g, j, t: (t, k)),
               pl.BlockSpec((tt, tn), lambda k, g, j, t: (t, g * nj + j)),
               pl.BlockSpec((None, tk, tn), lambda k, g, j, t: (g, k, j)), _TN, red_axis=3)


def _branch_nn(name, br, wb, l, out_dtype):
    T = br.shape[0]
    NB, W, n = wb.shape[2:]
    tm = min(T, 1024)
    return _mm(name, br, wb, _sds((T, NB * N_DEV * n), out_dtype), (T // tm, NB, N_DEV),
               pl.BlockSpec((tm, W), lambda i, b, d: (i, b)),
               pl.BlockSpec((None, None, None, W, n), lambda i, b, d: (d, l, b, 0, 0)),
               pl.BlockSpec((tm, n), lambda i, b, d: (i, b * N_DEV + d)), _NN)


def _branch_nt(name, dproj, wb, l, out_dtype):
    T = dproj.shape[0]
    NB, W, n = wb.shape[2:]
    tm = min(T, 1024)
    return _mm(name, dproj, wb, _sds((T, NB * W), out_dtype), (T // tm, NB, N_DEV),
               pl.BlockSpec((tm, n), lambda i, b, d: (i, b * N_DEV + d)),
               pl.BlockSpec((None, None, None, W, n), lambda i, b, d: (d, l, b, 0, 0)),
               pl.BlockSpec((tm, W), lambda i, b, d: (i, b)), _NT, red_axis=2)


def _branch_tn(name, br, dproj, n, out_dtype):
    T = br.shape[0]
    NB = 4
    W = br.shape[1] // NB
    tt = min(T, 1024)
    return _mm(name, br, dproj, _sds((N_DEV, NB, W, n), out_dtype), (NB, N_DEV, T // tt),
               pl.BlockSpec((tt, W), lambda b, d, t: (t, b)),
               pl.BlockSpec((tt, n), lambda b, d, t: (t, b * N_DEV + d)),
               pl.BlockSpec((None, None, W, n), lambda b, d, t: (d, b, 0, 0)), _TN, red_axis=2)


class _Att:
    def __init__(self, L, R, H, G, qcol, kcol, vcol, qstr, kstr, vstr, mode, reach=0, pad=0, rows=0):
        self.L, self.R, self.H, self.G = L, R, H, G
        self.qcol, self.kcol, self.vcol = qcol, kcol, vcol
        self.qstr, self.kstr, self.vstr = qstr, kstr, vstr
        self.mode, self.reach, self.rows = mode, reach, rows
        self.t = min(ATT_TILE, L)
        self.nb = L // self.t
        self.pieces = mode != "dense" and self.nb > 1
        self.pad = pad if self.pieces else 0
        self.span = self.t + 2 * self.pad
        self.ppb = self.t // pad if self.pieces else 0

    def qc(self, r, h):
        return r * self.qstr + self.qcol + h

    def kc(self, r, h):
        return r * self.kstr + self.kcol + h // self.G

    def vc(self, r, h):
        return r * self.vstr + self.vcol + h // self.G

    def oc(self, r, h):
        return r * self.H + h

    def mask(self, qpos, kpos, spos):
        if self.mode == "band":
            ok = jnp.abs(qpos - kpos) <= self.reach
        else:
            rq, cq = qpos >> 6, qpos & (GRID_W - 1)
            rk, ck = kpos >> 6, kpos & (GRID_W - 1)
            rs = jnp.clip(rq - NA_ROWS // 2, 0, self.rows - NA_ROWS)
            cs = jnp.clip(cq - NA_COLS // 2, 0, GRID_W - NA_COLS)
            ok = (rk >= rs) & (rk < rs + NA_ROWS) & (ck >= cs) & (ck < cs + NA_COLS)
        if self.pieces:
            ok = ok & (spos >= 0) & (spos < self.L)
        return ok

    def span_specs(self, col):
        t, pad, ppb = self.t, self.pad, self.ppb
        cur = pl.BlockSpec((t, HD), lambda r, h, i: (i, col(r, h)))
        if not self.pieces:
            return [cur]
        last = self.L // pad - 1
        prev = pl.BlockSpec((pad, HD), lambda r, h, i: (jnp.maximum(i * ppb - 1, 0), col(r, h)))
        nxt = pl.BlockSpec((pad, HD), lambda r, h, i: (jnp.minimum((i + 1) * ppb, last), col(r, h)))
        return [prev, cur, nxt]

    def positions(self, i):
        own = i * self.t + lax.broadcasted_iota(jnp.int32, (self.t, 1), 0)
        spn = i * self.t - self.pad + lax.broadcasted_iota(jnp.int32, (1, self.span), 1)
        return own, spn


def _cat(refs):
    return refs[0][...] if len(refs) == 1 else jnp.concatenate([r[...] for r in refs], axis=0)


def _band_fwd(name, att, q, k, v, sink=None, bias=None):
    t, ns = att.t, (3 if att.pieces else 1)

    def body(*refs):
        q_ref, k_refs, v_refs = refs[0], refs[1:1 + ns], refs[1 + ns:1 + 2 * ns]
        pos = 1 + 2 * ns
        sink_ref = bias_ref = None
        if sink is not None:
            sink_ref = refs[pos]
            pos += 1
        if bias is not None:
            bias_ref = refs[pos]
            pos += 1
        o_ref, lse_ref = refs[pos:]
        ks, vs = _cat(k_refs), _cat(v_refs)
        s = lax.dot_general(q_ref[...], ks, _NT, preferred_element_type=F32) * SCALE2
        if bias_ref is not None:
            s = s + bias_ref[...]
        qpos, kpos = att.positions(pl.program_id(2))
        s = jnp.where(att.mask(qpos, kpos, kpos), s, NEG_INF)
        m = jnp.max(s, axis=1, keepdims=True)
        if sink_ref is not None:
            sk = sink_ref[...][:, :1] * LOG2E
            m = jnp.maximum(m, sk)
        p = jnp.exp2(s - m)
        den = jnp.sum(p, axis=1, keepdims=True)
        if sink_ref is not None:
            den = den + jnp.exp2(sk - m)
        o = jnp.dot(p.astype(BF16), vs, preferred_element_type=F32)
        o_ref[...] = (o / den).astype(o_ref.dtype)
        lse_ref[...] = jnp.broadcast_to((m + jnp.log2(den)) * LN2, (t, HD))

    in_specs = [pl.BlockSpec((t, HD), lambda r, h, i: (i, att.qc(r, h)))] + att.span_specs(att.kc) + att.span_specs(att.vc)
    args = [q] + [k] * ns + [v] * ns
    if sink is not None:
        in_specs.append(pl.BlockSpec((None, 1, HD), lambda r, h, i: (h, 0, 0)))
        args.append(sink)
    if bias is not None:
        in_specs.append(pl.BlockSpec((None, t, att.span), lambda r, h, i: (h, 0, 0)))
        args.append(bias)
    o_spec = pl.BlockSpec((t, HD), lambda r, h, i: (i, att.oc(r, h)))
    cols = att.R * att.H * HD
    return _pc(body, name=name, grid=(att.R, att.H, att.nb),
               out_shape=[_sds((att.L, cols), BF16), _sds((att.L, cols), F32)],
               in_specs=in_specs, out_specs=[o_spec, o_spec],
               compiler_params=_cparams(("parallel", "parallel", "parallel")))(*args)


def _band_dq(name, att, q, k, v, do, lse, delta, bias=None):
    t, ns = att.t, (3 if att.pieces else 1)

    def body(*refs):
        q_ref, k_refs, v_refs = refs[0], refs[1:1 + ns], refs[1 + ns:1 + 2 * ns]
        do_ref, lse_ref, dl_ref = refs[1 + 2 * ns:4 + 2 * ns]
        pos = 4 + 2 * ns
        bias_ref = dtab_ref = None
        if bias is not None:
            bias_ref = refs[pos]
            pos += 1
        dq_ref = refs[pos]
        if bias is not None:
            dtab_ref = refs[pos + 1]
        i = pl.program_id(2)
        ks, vs = _cat(k_refs), _cat(v_refs)
        s = lax.dot_general(q_ref[...], ks, _NT, preferred_element_type=F32) * SCALE2
        if bias_ref is not None:
            s = s + bias_ref[...]
        qpos, kpos = att.positions(i)
        s = jnp.where(att.mask(qpos, kpos, kpos), s, NEG_INF)
        p = jnp.exp2(s - lse_ref[...][:, :1] * LOG2E)
        dp = lax.dot_general(do_ref[...], vs, _NT, preferred_element_type=F32)
        ds = p * (dp - dl_ref[...][:, :1])
        if dtab_ref is not None:
            @pl.when(i == 0)
            def _():
                dtab_ref[...] = ds

            @pl.when(i > 0)
            def _():
                dtab_ref[...] += ds
        dq_ref[...] = jnp.dot(ds.astype(BF16), ks, preferred_element_type=F32) * SCALE

    row = pl.BlockSpec((t, HD), lambda r, h, i: (i, att.oc(r, h)))
    in_specs = [pl.BlockSpec((t, HD), lambda r, h, i: (i, att.qc(r, h)))] + att.span_specs(att.kc) + att.span_specs(att.vc)
    in_specs += [row, row, row]
    args = [q] + [k] * ns + [v] * ns + [do, lse, delta]
    out_shape = [_sds((att.L, att.R * att.H * HD), F32)]
    out_specs = [row]
    sem = ("parallel", "parallel", "parallel")
    if bias is not None:
        tab = pl.BlockSpec((None, t, att.span), lambda r, h, i: (h, 0, 0))
        in_specs.append(tab)
        args.append(bias)
        out_shape.append(_sds((att.H, t, att.span), F32))
        out_specs.append(tab)
        sem = ("parallel", "parallel", "arbitrary")
    return _pc(body, name=name, grid=(att.R, att.H, att.nb), out_shape=out_shape, in_specs=in_specs,
               out_specs=out_specs, compiler_params=_cparams(sem))(*args)


def _band_dkv(name, att, q, k, v, do, lse_sp, dl_sp, bias_t=None):
    t, ns = att.t, (3 if att.pieces else 1)

    def body(*refs):
        k_ref, v_ref = refs[:2]
        q_refs, do_refs = refs[2:2 + ns], refs[2 + ns:2 + 2 * ns]
        lse_ref, dl_ref = refs[2 + 2 * ns:4 + 2 * ns]
        pos = 4 + 2 * ns
        bias_ref = None
        if bias_t is not None:
            bias_ref = refs[pos]
            pos += 1
        dk_ref, dv_ref = refs[pos:]
        qs, dos = _cat(q_refs), _cat(do_refs)
        st = lax.dot_general(k_ref[...], qs, _NT, preferred_element_type=F32) * SCALE2
        if bias_ref is not None:
            st = st + bias_ref[...]
        kpos, qpos = att.positions(pl.program_id(2))
        st = jnp.where(att.mask(qpos, kpos, qpos), st, NEG_INF)
        pt = jnp.exp2(st - lse_ref[...] * LOG2E)
        dv_ref[...] = jnp.dot(pt.astype(BF16), dos, preferred_element_type=F32)
        dpt = lax.dot_general(v_ref[...], dos, _NT, preferred_element_type=F32)
        dst = pt * (dpt - dl_ref[...])
        dk_ref[...] = jnp.dot(dst.astype(BF16), qs, preferred_element_type=F32) * SCALE

    stat = pl.BlockSpec((None, None, 1, att.span), lambda r, h, i: (r * att.H + h, i, 0, 0))
    in_specs = [pl.BlockSpec((t, HD), lambda r, h, i: (i, att.kc(r, h))),
                pl.BlockSpec((t, HD), lambda r, h, i: (i, att.vc(r, h)))]
    in_specs += att.span_specs(att.qc) + att.span_specs(att.oc) + [stat, stat]
    args = [k, v] + [q] * ns + [do] * ns + [lse_sp, dl_sp]
    if bias_t is not None:
        in_specs.append(pl.BlockSpec((None, t, att.span), lambda r, h, i: (h, 0, 0)))
        args.append(bias_t)
    o_spec = pl.BlockSpec((t, HD), lambda r, h, i: (i, att.oc(r, h)))
    cols = att.R * att.H * HD
    return _pc(body, name=name, grid=(att.R, att.H, att.nb), out_shape=[_sds((att.L, cols), F32)] * 2,
               in_specs=in_specs, out_specs=[o_spec, o_spec],
               compiler_params=_cparams(("parallel", "parallel", "parallel")))(*args)


def _dense_fwd(name, att, q, k, v):
    t, L, nb = att.t, att.L, att.nb

    def body(q_ref, k_ref, v_ref, o_ref, lse_ref):
        qq = q_ref[...]

        def step(j, carry):
            m, den, acc = carry
            rows = pl.ds(pl.multiple_of(j * t, t), t)
            s = lax.dot_general(qq, k_ref[rows, :], _NT, preferred_element_type=F32) * SCALE2
            m_new = jnp.maximum(m, jnp.max(s, axis=1, keepdims=True))
            alpha = jnp.exp2(m - m_new)
            p = jnp.exp2(s - m_new)
            den = alpha * den + jnp.sum(p, axis=1, keepdims=True)
            acc = alpha * acc + jnp.dot(p.astype(BF16), v_ref[rows, :], preferred_element_type=F32)
            return m_new, den, acc

        init = (jnp.full((t, 1), NEG_INF, F32), jnp.zeros((t, 1), F32), jnp.zeros((t, HD), F32))
        m, den, acc = lax.fori_loop(0, nb, step, init, unroll=2 if nb % 2 == 0 else 1)
        o_ref[...] = (acc / den).astype(o_ref.dtype)
        lse_ref[...] = jnp.broadcast_to((m + jnp.log2(den)) * LN2, (t, HD))

    o_spec = pl.BlockSpec((t, HD), lambda h, i: (i, att.oc(0, h)))
    return _pc(body, name=name, grid=(att.H, nb),
               out_shape=[_sds((L, att.H * HD), BF16), _sds((L, att.H * HD), F32)],
               in_specs=[pl.BlockSpec((t, HD), lambda h, i: (i, att.qc(0, h))),
                         pl.BlockSpec((L, HD), lambda h, i: (0, att.kc(0, h))),
                         pl.BlockSpec((L, HD), lambda h, i: (0, att.vc(0, h)))],
               out_specs=[o_spec, o_spec], compiler_params=_cparams(("parallel", "parallel")))(q, k, v)


def _dense_dq(name, att, q, k, v, do, lse, delta):
    t, L, nb = att.t, att.L, att.nb

    def body(q_ref, k_ref, v_ref, do_ref, lse_ref, dl_ref, dq_ref):
        qq, dd = q_ref[...], do_ref[...]
        lse2 = lse_ref[...][:, :1] * LOG2E
        dl = dl_ref[...][:, :1]

        def step(j, acc):
            rows = pl.ds(pl.multiple_of(j * t, t), t)
            kk = k_ref[rows, :]
            s = lax.dot_general(qq, kk, _NT, preferred_element_type=F32) * SCALE2
            p = jnp.exp2(s - lse2)
            dp = lax.dot_general(dd, v_ref[rows, :], _NT, preferred_element_type=F32)
            ds = p * (dp - dl)
            return acc + jnp.dot(ds.astype(BF16), kk, preferred_element_type=F32)

        acc = lax.fori_loop(0, nb, step, jnp.zeros((t, HD), F32), unroll=2 if nb % 2 == 0 else 1)
        dq_ref[...] = acc * SCALE

    row = pl.BlockSpec((t, HD), lambda h, i: (i, att.oc(0, h)))
    return _pc(body, name=name, grid=(att.H, nb), out_shape=_sds((L, att.H * HD), F32),
               in_specs=[pl.BlockSpec((t, HD), lambda h, i: (i, att.qc(0, h))),
                         pl.BlockSpec((L, HD), lambda h, i: (0, att.kc(0, h))),
                         pl.BlockSpec((L, HD), lambda h, i: (0, att.vc(0, h))), row, row, row],
               out_specs=row, compiler_params=_cparams(("parallel", "parallel")))(q, k, v, do, lse, delta)


def _dense_dkv(name, att, q, k, v, do, lse_rows, dl_rows):
    t, L, nb = att.t, att.L, att.nb

    def body(k_ref, v_ref, q_ref, do_ref, lse_ref, dl_ref, dk_ref, dv_ref):
        kk, vv = k_ref[...], v_ref[...]

        def step(i, carry):
            dk, dv = carry
            rows = pl.ds(pl.multiple_of(i * t, t), t)
            qq, dd = q_ref[rows, :], do_ref[rows, :]
            st = lax.dot_general(kk, qq, _NT, preferred_element_type=F32) * SCALE2
            pt = jnp.exp2(st - lse_ref[i] * LOG2E)
            dv = dv + jnp.dot(pt.astype(BF16), dd, preferred_element_type=F32)
            dpt = lax.dot_general(vv, dd, _NT, preferred_element_type=F32)
            dst = pt * (dpt - dl_ref[i])
            dk = dk + jnp.dot(dst.astype(BF16), qq, preferred_element_type=F32)
            return dk, dv

        zero = jnp.zeros((t, HD), F32)
        dk, dv = lax.fori_loop(0, nb, step, (zero, zero), unroll=2 if nb % 2 == 0 else 1)
        dk_ref[...] = dk * SCALE
        dv_ref[...] = dv

    stat = pl.BlockSpec((None, nb, 1, t), lambda h, j: (h, 0, 0, 0))
    o_spec = pl.BlockSpec((t, HD), lambda h, j: (j, att.oc(0, h)))
    return _pc(body, name=name, grid=(att.H, nb), out_shape=[_sds((L, att.H * HD), F32)] * 2,
               in_specs=[pl.BlockSpec((t, HD), lambda h, j: (j, att.kc(0, h))),
                         pl.BlockSpec((t, HD), lambda h, j: (j, att.vc(0, h))),
                         pl.BlockSpec((L, HD), lambda h, j: (0, att.qc(0, h))),
                         pl.BlockSpec((L, HD), lambda h, j: (0, att.oc(0, h))), stat, stat],
               out_specs=[o_spec, o_spec],
               compiler_params=_cparams(("parallel", "parallel")))(k, v, q, do, lse_rows, dl_rows)


def _stat_rows(att, col):
    RH = att.R * att.H
    rows = col.reshape(att.L, RH, HD)[:, :, 0].T
    if not att.pieces:
        return rows.reshape(RH, att.nb, 1, att.t)
    padded = jnp.pad(rows, ((0, 0), (att.pad, att.pad)))
    return jnp.stack([padded[:, j * att.t:j * att.t + att.span] for j in range(att.nb)], axis=1)[:, :, None, :]


def _attn_fwd(name, att, q, k, v, sink=None, bias=None):
    if att.mode == "dense":
        return _dense_fwd(name, att, q, k, v)
    return _band_fwd(name, att, q, k, v, sink=sink, bias=bias)


def _attn_bwd(name, att, q, k, v, do, lse, delta, bias=None, bias_t=None):
    lse_r, dl_r = _stat_rows(att, lse), _stat_rows(att, delta)
    if att.mode == "dense":
        dq = _dense_dq(name + "_dq", att, q, k, v, do, lse, delta)
        dk, dv = _dense_dkv(name + "_dkv", att, q, k, v, do, lse_r, dl_r)
        return dq, dk, dv, None
    res = _band_dq(name + "_dq", att, q, k, v, do, lse, delta, bias=bias)
    dk, dv = _band_dkv(name + "_dkv", att, q, k, v, do, lse_r, dl_r, bias_t=bias_t)
    return res[0], dk, dv, (res[1] if bias is not None else None)


def _row_call(name, body, T, tt, ins, outs, acc_outs=()):
    in_specs, args = [], []
    for item in ins:
        a = item[0]
        if item[1] is None:
            in_specs.append(pl.BlockSpec(a.shape, lambda i, nd=a.ndim: (0,) * nd))
        else:
            in_specs.append(pl.BlockSpec((tt, item[1]), lambda i, cb=item[2]: (i, cb)))
        args.append(a)
    out_shape = [_sds((T, c), d) for c, d in outs] + [_sds(s, F32) for s in acc_outs]
    out_specs = [pl.BlockSpec((tt, c), lambda i: (i, 0)) for c, _ in outs]
    out_specs += [pl.BlockSpec(s, lambda i, nd=len(s): (0,) * nd) for s in acc_outs]
    sem = ("arbitrary",) if acc_outs else ("parallel",)
    return _pc(body, name=name, grid=(T // tt,), out_shape=out_shape, in_specs=in_specs, out_specs=out_specs,
               compiler_params=_cparams(sem))(*args)


def _acc(ref, val):
    @pl.when(pl.program_id(0) == 0)
    def _():
        ref[...] = val

    @pl.when(pl.program_id(0) > 0)
    def _():
        ref[...] += val


def _prenorm_fwd(name, x, g, scale, shift):
    T, D = x.shape

    def body(x_ref, g_ref, sc_ref, sh_ref, h_ref):
        xf = x_ref[...]
        r = lax.rsqrt(jnp.mean(xf * xf, axis=1, keepdims=True) + EPS)
        h_ref[...] = ((xf * r) * g_ref[...] * (1.0 + sc_ref[...]) + sh_ref[...]).astype(BF16)

    return _row_call(name, body, T, min(T, 256), [(x, D, 0), (g, None), (scale, None), (shift, None)], [(D, BF16)])[0]


def _prenorm_bwd(name, x, dh, dxn, g, scale):
    T, D = x.shape

    def body(x_ref, dh_ref, dxn_ref, g_ref, sc_ref, dx_ref, dsh_ref, dsc_ref, dg_ref):
        xf, dh_, gg = x_ref[...], dh_ref[...], g_ref[...]
        r = lax.rsqrt(jnp.mean(xf * xf, axis=1, keepdims=True) + EPS)
        u = xf * r
        dn = dh_ * (1.0 + sc_ref[...])
        du = dn * gg
        dx_ref[...] = dxn_ref[...] + r * (du - u * jnp.mean(du * u, axis=1, keepdims=True))
        _acc(dsh_ref, jnp.sum(dh_, axis=0, keepdims=True))
        _acc(dsc_ref, jnp.sum(dh_ * (u * gg), axis=0, keepdims=True))
        _acc(dg_ref, jnp.sum(dn * u, axis=0, keepdims=True))

    return _row_call(name, body, T, min(T, 256), [(x, D, 0), (dh, D, 0), (dxn, D, 0), (g, None), (scale, None)],
                     [(D, F32)], [(1, D)] * 3)


def _resid_fwd(name, x, out, gate):
    T, D = x.shape

    def body(x_ref, o_ref, g_ref, y_ref):
        y_ref[...] = x_ref[...] + g_ref[...] * o_ref[...]

    return _row_call(name, body, T, min(T, 256), [(x, D, 0), (out, D, 0), (gate, None)], [(D, F32)])[0]


def _resid_bwd(name, dx, out, gate):
    T, D = dx.shape

    def body(dx_ref, o_ref, g_ref, do_ref, dg_ref):
        d = dx_ref[...]
        do_ref[...] = (d * g_ref[...]).astype(BF16)
        _acc(dg_ref, jnp.sum(d * o_ref[...], axis=0, keepdims=True))

    return _row_call(name, body, T, min(T, 256), [(dx, D, 0), (out, D, 0), (gate, None)], [(D, BF16)], [(1, D)])


def _final(name, x, tgt, g):
    T, D = x.shape

    def body(x_ref, t_ref, g_ref, dx_ref, loss_ref, dg_ref):
        xf, gg = x_ref[...], g_ref[...]
        r = lax.rsqrt(jnp.mean(xf * xf, axis=1, keepdims=True) + EPS)
        u = xf * r
        err = u * gg - t_ref[...]
        part = 0.5 * jnp.sum(jnp.mean(err * err, axis=1, keepdims=True), axis=0, keepdims=True)
        _acc(loss_ref, jnp.broadcast_to(part, (1, HD)))
        dy = err * (1.0 / D)
        _acc(dg_ref, jnp.sum(dy * u, axis=0, keepdims=True))
        du = dy * gg
        dx_ref[...] = r * (du - u * jnp.mean(du * u, axis=1, keepdims=True))

    return _row_call(name, body, T, min(T, 256), [(x, D, 0), (tgt, D, 0), (g, None)], [(D, F32)], [(1, HD), (1, D)])


def _roll_pair(x, shift):
    lanes = lax.broadcasted_iota(jnp.int32, x.shape, 1)
    r1 = pltpu.roll(x, shift, 1)
    r2 = pltpu.roll(x, HD - shift, 1)
    src = pltpu.roll(lanes, shift, 1)
    is_plus = src == ((lanes + shift) & (HD - 1))
    return jnp.where(is_plus, r1, r2), jnp.where(is_plus, r2, r1)


def _rope1(x, cos, ss):
    xp, _ = _roll_pair(x, 64)
    return x * cos + xp * ss


def _rope1_t(d, cos, ss):
    dp, _ = _roll_pair(d * ss, 64)
    return d * cos + dp


def _ropex(x, cos, sa, sb):
    xp, xm = _roll_pair(x, 32)
    return x * cos + xp * sa + xm * sb


def _ropex_t(d, cos, sa, sb):
    _, am = _roll_pair(d * sa, 32)
    bp, _ = _roll_pair(d * sb, 32)
    return d * cos + am + bp


def _qk_prep(name, qkvg, tabs, cqn, ckn):
    T = qkvg.shape[0]
    cos1, ss1, cosx, sax, sbx = tabs

    def body(aq, ak, bq, bk, bv, cq, ck, c1, s1, cx, ax, bx, gq, gk, oaq, oak, obq, obk, obv, ocq, ock):
        c1v, s1v = c1[...], s1[...]
        for src, dst, nh in ((aq, oaq, 4), (ak, oak, 2), (bq, obq, 4), (bk, obk, 4)):
            for h in range(nh):
                sl = slice(h * HD, (h + 1) * HD)
                dst[:, sl] = _rope1(src[:, sl].astype(F32), c1v, s1v).astype(BF16)
        obv[...] = bv[...]
        cxv, axv, bxv = cx[...], ax[...], bx[...]
        for src, dst, nh, gref in ((cq, ocq, 4, gq), (ck, ock, 2, gk)):
            for h in range(nh):
                sl = slice(h * HD, (h + 1) * HD)
                xf = src[:, sl].astype(F32)
                r = lax.rsqrt(jnp.mean(xf * xf, axis=1, keepdims=True) + EPS)
                dst[:, sl] = _ropex(xf * r * gref[...], cxv, axv, bxv).astype(BF16)

    ins = [(qkvg, 512, 0), (qkvg, 256, 2), (qkvg, 512, 3), (qkvg, 512, 4), (qkvg, 512, 5), (qkvg, 512, 7),
           (qkvg, 256, 16), (cos1, HD, 0), (ss1, HD, 0), (cosx, HD, 0), (sax, HD, 0), (sbx, HD, 0), (cqn, None), (ckn, None)]
    outs = [(512, BF16), (256, BF16), (512, BF16), (512, BF16), (512, BF16), (512, BF16), (256, BF16)]
    return _row_call(name, body, T, min(T, 512), ins, outs)


def _silu_parts(g):
    sig = 1.0 / (1.0 + jnp.exp(-g))
    return g * sig, sig * (1.0 + g * (1.0 - sig))


def _mix_weights(l0, l1, l2):
    mx = jnp.maximum(jnp.maximum(l0, l1), l2)
    e0, e1, e2 = jnp.exp(l0 - mx), jnp.exp(l1 - mx), jnp.exp(l2 - mx)
    inv = 1.0 / (e0 + e1 + e2)
    return e0 * inv, e1 * inv, e2 * inv


def _gate_fwd(name, qkvg, oa, ob, lb, oc, od):
    T = qkvg.shape[0]

    def body(ga, gb, gc, gd, oa_r, ob0, ob1, ob2, lb0, lb1, lb2, oc_r, od_r, br):
        w0, w1, w2 = _mix_weights(lb0[...], lb1[...], lb2[...])
        yb = w0 * ob0[...].astype(F32) + w1 * ob1[...].astype(F32) + w2 * ob2[...].astype(F32)
        ys = (oa_r[...].astype(F32), yb, oc_r[...].astype(F32), od_r[...].astype(F32))
        for n, (y, g) in enumerate(zip(ys, (ga, gb, gc, gd))):
            act, _ = _silu_parts(g[...].astype(F32))
            br[:, n * 512:(n + 1) * 512] = (y * act).astype(BF16)

    ins = [(qkvg, 512, 2), (qkvg, 512, 6), (qkvg, 512, 9), (qkvg, 512, 13), (oa, 512, 0)]
    ins += [(o, 512, 0) for o in ob] + [(l_, 512, 0) for l_ in lb] + [(oc, 512, 0), (od, 512, 0)]
    return _row_call(name, body, T, min(T, 256), ins, [(2048, BF16)])[0]


def _head_rowsum(x):
    parts = []
    for h in range(x.shape[1] // HD):
        s = jnp.sum(x[:, h * HD:(h + 1) * HD], axis=1, keepdims=True)
        parts.append(jnp.broadcast_to(s, (x.shape[0], HD)))
    return jnp.concatenate(parts, axis=1)


def _gate_bwd(name, dbr, qkvg, oa, la, sink_row, ob, lb, oc, od):
    T = qkvg.shape[0]

    def body(dbr_r, ga, gb, gc, gd, oa_r, la_r, sk, ob0, ob1, ob2, lb0, lb1, lb2, oc_r, od_r,
             doa, dla, dob0, dob1, dob2, dlb0, dlb1, dlb2, doc, dlc, dod, dld, dg, dsk):
        def one(n, g_ref, y):
            act, dact = _silu_parts(g_ref[...].astype(F32))
            d = dbr_r[:, n * 512:(n + 1) * 512]
            dg[:, n * 512:(n + 1) * 512] = (d * y * dact).astype(BF16)
            return d * act

        ya = oa_r[...].astype(F32)
        dya = one(0, ga, ya)
        doa[...] = dya.astype(BF16)
        dl_a = _head_rowsum(dya * ya)
        dla[...] = dl_a
        _acc(dsk, -jnp.sum(jnp.exp(sk[...] - la_r[...]) * dl_a, axis=0, keepdims=True))

        w = _mix_weights(lb0[...], lb1[...], lb2[...])
        obs = (ob0[...].astype(F32), ob1[...].astype(F32), ob2[...].astype(F32))
        yb = w[0] * obs[0] + w[1] * obs[1] + w[2] * obs[2]
        dyb = one(1, gb, yb)
        rs = _head_rowsum(dyb * yb)
        for wp, do_ref, dl_ref in zip(w, (dob0, dob1, dob2), (dlb0, dlb1, dlb2)):
            do_ref[...] = (wp * dyb).astype(BF16)
            dl_ref[...] = wp * rs

        for n, g_ref, o_r, do_ref, dl_ref in ((2, gc, oc_r, doc, dlc), (3, gd, od_r, dod, dld)):
            y = o_r[...].astype(F32)
            dy = one(n, g_ref, y)
            do_ref[...] = dy.astype(BF16)
            dl_ref[...] = _head_rowsum(dy * y)

    ins = [(dbr, 2048, 0), (qkvg, 512, 2), (qkvg, 512, 6), (qkvg, 512, 9), (qkvg, 512, 13),
           (oa, 512, 0), (la, 512, 0), (sink_row, None)]
    ins += [(o, 512, 0) for o in ob] + [(l_, 512, 0) for l_ in lb] + [(oc, 512, 0), (od, 512, 0)]
    outs = [(512, BF16), (512, F32)] + [(512, BF16)] * 3 + [(512, F32)] * 3 + [(512, BF16), (512, F32)] * 2
    outs += [(2048, BF16)]
    return _row_call(name, body, T, min(T, 256), ins, outs, [(1, 512)])


def _merge_fwd(name, mgl, proj, D):
    T = mgl.shape[0]

    def body(m_ref, p_ref, o_ref):
        acc = None
        for n in range(4):
            sl = slice(n * D, (n + 1) * D)
            sig = 1.0 / (1.0 + jnp.exp(-m_ref[:, sl].astype(F32)))
            term = sig * p_ref[:, sl].astype(F32)
            acc = term if acc is None else acc + term
        o_ref[...] = acc.astype(BF16)

    return _row_call(name, body, T, min(T, 256), [(mgl, 4 * D, 0), (proj, 4 * D, 0)], [(D, BF16)])[0]


def _merge_bwd(name, dm, mgl, proj, D):
    T = mgl.shape[0]

    def body(d_ref, m_ref, p_ref, dp_ref, dl_ref):
        d = d_ref[...]
        for n in range(4):
            sl = slice(n * D, (n + 1) * D)
            sig = 1.0 / (1.0 + jnp.exp(-m_ref[:, sl].astype(F32)))
            dp_ref[:, sl] = (d * sig).astype(BF16)
            dl_ref[:, sl] = (d * p_ref[:, sl].astype(F32) * sig * (1.0 - sig)).astype(BF16)

    return _row_call(name, body, T, min(T, 128), [(dm, D, 0), (mgl, 4 * D, 0), (proj, 4 * D, 0)],
                     [(4 * D, BF16), (4 * D, BF16)])


def _qkvg_bwd(name, qkvg, tabs, cqn, ckn, dA, dB, dC, dD, dg):
    T = qkvg.shape[0]
    cos1, ss1, cosx, sax, sbx = tabs

    def body(cq_r, ck_r, c1, s1, cx, ax, bx, gq, gk,
             dqa, dka, dva, dqb0, dqb1, dqb2, dkb0, dkb1, dkb2, dvb0, dvb1, dvb2,
             dqc, dkc, dvc, dqd, dkd, dvd, dg_r, out, dgq, dgk):
        c1v, s1v = c1[...], s1[...]

        def put(col, val):
            out[:, col * HD:(col + 1) * HD] = val.astype(BF16)

        def pair(ref, kv):
            return ref[:, 2 * kv * HD:(2 * kv + 1) * HD] + ref[:, (2 * kv + 1) * HD:(2 * kv + 2) * HD]

        for h in range(4):
            sl = slice(h * HD, (h + 1) * HD)
            put(A_Q + h, _rope1_t(dqa[:, sl], c1v, s1v))
            put(B_Q + h, _rope1_t(dqb0[:, sl] + dqb1[:, sl] + dqb2[:, sl], c1v, s1v))
            put(B_K + h, _rope1_t(dkb0[:, sl] + dkb1[:, sl] + dkb2[:, sl], c1v, s1v))
            put(B_V + h, dvb0[:, sl] + dvb1[:, sl] + dvb2[:, sl])
            put(D_Q + h, dqd[:, sl])
            put(D_K + h, dkd[:, sl])
            put(D_V + h, dvd[:, sl])
        for kv in range(2):
            put(A_K + kv, _rope1_t(pair(dka, kv), c1v, s1v))
            put(A_V + kv, pair(dva, kv))
            put(C_V + kv, pair(dvc, kv))
        cxv, axv, bxv = cx[...], ax[...], bx[...]
        for src, dref, col, nh, gref, dgref in ((cq_r, dqc, C_Q, 4, gq, dgq), (ck_r, dkc, C_K, 2, gk, dgk)):
            gsum = None
            for h in range(nh):
                sl = slice(h * HD, (h + 1) * HD)
                xf = src[:, sl].astype(F32)
                r = lax.rsqrt(jnp.mean(xf * xf, axis=1, keepdims=True) + EPS)
                u = xf * r
                dy = _ropex_t(dref[:, sl] if nh == 4 else pair(dref, h), cxv, axv, bxv)
                du = dy * gref[...]
                put(col + h, r * (du - u * jnp.mean(du * u, axis=1, keepdims=True)))
                part = jnp.sum(dy * u, axis=0, keepdims=True)
                gsum = part if gsum is None else gsum + part
            _acc(dgref, gsum)
        for n, col in enumerate((A_G, B_G, C_G, D_G)):
            out[:, col * HD:(col + 4) * HD] = dg_r[:, n * 512:(n + 1) * 512]

    ins = [(qkvg, 512, 7), (qkvg, 256, 16), (cos1, HD, 0), (ss1, HD, 0), (cosx, HD, 0), (sax, HD, 0), (sbx, HD, 0),
           (cqn, None), (ckn, None)]
    ins += [(a, 512, 0) for a in dA]
    ins += [(a, 512, 0) for a in dB[0]] + [(a, 512, 0) for a in dB[1]] + [(a, 512, 0) for a in dB[2]]
    ins += [(a, 512, 0) for a in dC] + [(a, 512, 0) for a in dD] + [(dg, 2048, 0)]
    return _row_call(name, body, T, min(T, 256), ins, [(IN_HEADS * HD, BF16)], [(1, HD), (1, HD)])


def _ada_fwd(name, c_pad, w_ada):
    L, D, n = w_ada.shape
    P = c_pad.shape[0]

    def body(c_ref, w_ref, o_ref):
        cc = c_ref[...]
        cond = cc / (1.0 + jnp.exp(-cc))
        o_ref[...] = jnp.dot(cond.astype(BF16), w_ref[...].astype(BF16), preferred_element_type=F32)

    return _pc(body, name=name, grid=(L,), out_shape=_sds((L, P, n), F32),
               in_specs=[pl.BlockSpec((P, D), lambda l: (0, 0)), pl.BlockSpec((None, D, n), lambda l: (l, 0, 0))],
               out_specs=pl.BlockSpec((None, P, n), lambda l: (l, 0, 0)),
               compiler_params=_cparams(("parallel",)))(c_pad, w_ada)


def _ada_grad(name, c_pad, d_ada):
    L, P, n = d_ada.shape
    D = c_pad.shape[1]

    def body(c_ref, d_ref, o_ref):
        cc = c_ref[...]
        cond = cc / (1.0 + jnp.exp(-cc))
        o_ref[...] = lax.dot_general(cond.astype(BF16), d_ref[...].astype(BF16), _TN, preferred_element_type=F32)

    return _pc(body, name=name, grid=(L,), out_shape=_sds((L, D, n), F32),
               in_specs=[pl.BlockSpec((P, D), lambda l: (0, 0)), pl.BlockSpec((None, P, n), lambda l: (l, 0, 0))],
               out_specs=pl.BlockSpec((None, D, n), lambda l: (l, 0, 0)),
               compiler_params=_cparams(("parallel",)))(c_pad, d_ada)


def _adamw(name, w, m, v, g, slots):
    rows, cols = w.shape
    tr = rows
    cap = max(8, (2 ** 20) // (4 * cols))
    if rows > cap:
        tr = 8
        while tr * 2 <= cap and rows % (tr * 2) == 0:
            tr *= 2
    c1 = 1.0 / (1.0 - ADAM_B1 ** ADAM_STEP)
    c2 = 1.0 / (1.0 - ADAM_B2 ** ADAM_STEP)

    def body(w_ref, m_ref, v_ref, g_ref, go_ref, d_ref, mo_ref, vo_ref):
        if slots:
            gg = g_ref[0].astype(F32)
            for k in range(1, slots):
                gg = gg + g_ref[k].astype(F32)
        else:
            gg = g_ref[...]
        mn = ADAM_B1 * m_ref[...] + (1.0 - ADAM_B1) * gg
        vn = ADAM_B2 * v_ref[...] + (1.0 - ADAM_B2) * (gg * gg)
        go_ref[...] = gg
        mo_ref[...] = mn
        vo_ref[...] = vn
        d_ref[...] = -ADAM_LR * ((mn * c1) / (jnp.sqrt(vn * c2) + ADAM_EPS) + ADAM_WD * w_ref[...])

    blk = pl.BlockSpec((tr, cols), lambda i: (i, 0))
    gspec = pl.BlockSpec((slots, tr, cols), lambda i: (0, i, 0)) if slots else blk
    return _pc(body, name=name, grid=(rows // tr,), out_shape=[_sds((rows, cols), F32)] * 4,
               in_specs=[blk, blk, blk, gspec], out_specs=[blk] * 4,
               compiler_params=_cparams(("parallel",)))(w, m, v, g)


def _rope_tables(T):
    pos = np.arange(T)
    lane = np.arange(HD)
    inv = ROPE_THETA ** (-np.arange(0, HD, 2, dtype=np.float32) / HD)
    ang = pos.astype(np.float32)[:, None] * inv[None, :]
    ang = np.concatenate([ang, ang], axis=-1)
    cos1 = np.cos(ang)
    ss1 = np.sin(ang) * np.where(lane < HD // 2, -1.0, 1.0)[None, :]
    half = HD // 2
    invh = ROPE_THETA ** (-np.arange(0, half, 2, dtype=np.float32) / half)

    def tab(p):
        a = p.astype(np.float32)[:, None] * invh[None, :]
        return np.concatenate([a, a], axis=-1)

    angx = np.concatenate([tab(pos // GRID_W), tab(pos % GRID_W)], axis=-1)
    cosx, sinx = np.cos(angx), np.sin(angx)
    first = (lane % half) < half // 2
    sax = np.where(first[None, :], -sinx, 0.0)
    sbx = np.where(first[None, :], 0.0, sinx)
    return tuple(jnp.asarray(a, F32) for a in (cos1, ss1, cosx, sax, sbx))


def _na_selectors(att, transposed):
    own_r = np.arange(att.t // GRID_W)[:, None]
    span_r = np.arange(att.span // GRID_W)[None, :] - att.pad // GRID_W
    dr = own_r - span_r if transposed else span_r - own_r
    row_sel = (np.clip(dr, -(NA_ROWS - 1), NA_ROWS - 1)[..., None] + NA_ROWS - 1 == np.arange(2 * NA_ROWS - 1))
    col = np.arange(GRID_W)
    dc = col[:, None] - col[None, :] if transposed else col[None, :] - col[:, None]
    col_sel = (np.clip(dc, -(NA_COLS - 1), NA_COLS - 1)[..., None] + NA_COLS - 1 == np.arange(2 * NA_COLS - 1))
    return jnp.asarray(row_sel.astype(np.float32)), jnp.asarray(col_sel.astype(np.float32))


def _na_bias_table(att, rel_bias, transposed):
    row_sel, col_sel = _na_selectors(att, transposed)
    tab = jnp.einsum("xyi,hij,cdj->hxcyd", row_sel, rel_bias * LOG2E, col_sel, precision=lax.Precision.HIGHEST)
    return tab.reshape(rel_bias.shape[0], att.t, att.span)


def _na_bias_grad(att, dtab):
    row_sel, col_sel = _na_selectors(att, False)
    d5 = dtab.reshape(dtab.shape[0], att.t // GRID_W, GRID_W, att.span // GRID_W, GRID_W)
    return jnp.einsum("xyi,hxcyd,cdj->hij", row_sel, d5, col_sel, precision=lax.Precision.HIGHEST)


def _att_specs(T):
    specs = {
        "A": _Att(T, 1, 4, 2, 0, 0, A_V, 0, 0, 0, "band", reach=A_REACH, pad=128),
        "C": _Att(T, 1, 4, 2, 0, 0, C_V, 0, 0, 0, "dense"),
        "D": _Att(T, 1, 4, 1, D_Q, D_K, D_V, 0, 0, 0, "na", pad=256, rows=T // GRID_W),
    }
    for p, (window, dil) in enumerate(B_PATTERNS):
        specs["B%d" % p] = _Att(T // dil, dil, 4, 1, 0, 0, 0, 4, 4, 4, "band", reach=(window // 2) // dil, pad=64)
    return specs


def kernel(x, c, norm_g, w_ada, b_ada, w_in, a_sink, c_q_norm, c_k_norm, d_rel_bias, w_gate_merge, w_branch, w_out, final_g, loss_target, m_norm_g, m_w_ada, m_b_ada, m_w_in, m_a_sink, m_c_q_norm, m_c_k_norm, m_d_rel_bias, m_w_gate_merge, m_w_branch, m_w_out, m_final_g, v_norm_g, v_w_ada, v_b_ada, v_w_in, v_a_sink, v_c_q_norm, v_c_k_norm, v_d_rel_bias, v_w_gate_merge, v_w_branch, v_w_out, v_final_g):
    T, D = x.shape[1], x.shape[2]
    NL = norm_g.shape[0]
    x0 = x.reshape(T, D)
    tgt = loss_target.reshape(T, D)
    me = 4 * lax.axis_index("x") + 2 * lax.axis_index("y") + lax.axis_index("c")
    att = _att_specs(T)
    tabs = _rope_tables(T)
    n_ada = w_ada.shape[2]

    (c_all,) = _exchange("gather_c", [c], gather=True)
    c_pad = jnp.pad(c_all.reshape(N_DEV, D), ((0, HD - N_DEV), (0, 0)))
    ada_part = _ada_fwd("ada_fwd", c_pad, w_ada)[:, :N_DEV]
    (ada_all,) = _exchange("gather_ada", [ada_part], gather=True)
    ada_mine = lax.dynamic_index_in_dim(ada_all, me, axis=2, keepdims=False)
    ada = jnp.transpose(ada_mine, (1, 0, 2)).reshape(NL, N_DEV * n_ada) + b_ada
    shift, scale, gate = ada[:, :D], ada[:, D:2 * D], ada[:, 2 * D:]

    wi_all, wg_all, wb_all, wo_all = _exchange(
        "gather_weights", [w_in.astype(BF16), w_gate_merge.astype(BF16), w_branch.astype(BF16), w_out.astype(BF16)],
        gather=True)

    def views(a_, arrs):
        return [a.reshape(a_.L, -1) for a in arrs]

    saved = []
    xl = x0
    for l in range(NL):
        Wo = wo_all[:, l].reshape(1, 1, D, D)
        sc_l, sh_l, gt_l = scale[l][None], shift[l][None], gate[l][None]
        ng_l = norm_g[l][None]
        cqn, ckn = c_q_norm[l][None], c_k_norm[l][None]
        sink3 = jnp.broadcast_to(a_sink[l][:, None, None], (4, 1, HD))
        btab = _na_bias_table(att["D"], d_rel_bias[l], False)

        h = _prenorm_fwd("prenorm_fwd", xl, ng_l, sc_l, sh_l)
        qkvg = _mm_nn("mm_in", h, wi_all, l, BF16)
        mgl = _mm_nn("mm_gm", h, wg_all, l, BF16)
        qa, ka, qb, kb, vb, qc, kc = _qk_prep("qk_prep", qkvg, tabs, cqn, ckn)
        oa, la = _attn_fwd("attn_a_fwd", att["A"], qa, ka, qkvg, sink=sink3)
        ob, lb = [], []
        for p in range(len(B_PATTERNS)):
            a_ = att["B%d" % p]
            o_, l_ = _attn_fwd("attn_b%d_fwd" % p, a_, *views(a_, (qb, kb, vb)))
            ob.append(o_.reshape(T, 512))
            lb.append(l_.reshape(T, 512))
        oc, lc = _attn_fwd("attn_c_fwd", att["C"], qc, kc, qkvg)
        od, ld = _attn_fwd("attn_d_fwd", att["D"], qkvg, qkvg, qkvg, bias=btab)
        br = _gate_fwd("gate_fwd", qkvg, oa, ob, lb, oc, od)
        proj = _branch_nn("mm_branch", br, wb_all, l, BF16)
        merged = _merge_fwd("merge_fwd", mgl, proj, D)
        out = _mm_nn("mm_out", merged, Wo, 0, F32)
        xn = _resid_fwd("resid_fwd", xl, out, gt_l)
        saved.append(dict(x=xl, h=h, qkvg=qkvg, mgl=mgl, qa=qa, ka=ka, qb=qb, kb=kb, vb=vb, qc=qc, kc=kc, oa=oa, la=la,
                          ob=ob, lb=lb, oc=oc, lc=lc, od=od, ld=ld, br=br, proj=proj, merged=merged, out=out, btab=btab))
        xl = xn

    dx, loss_part, d_final_g = _final("final_loss", xl, tgt, final_g[None])

    g_in, g_gm, g_br, g_out = [None] * NL, [None] * NL, [None] * NL, [None] * NL
    d_norm_g, d_ada, d_sink, d_cq, d_ck, d_bias = [None] * NL, [None] * NL, [None] * NL, [None] * NL, [None] * NL, [None] * NL
    for l in reversed(range(NL)):
        S = saved[l]
        Wo = wo_all[:, l].reshape(1, 1, D, D)
        sc_l, gt_l, ng_l = scale[l][None], gate[l][None], norm_g[l][None]
        cqn, ckn = c_q_norm[l][None], c_k_norm[l][None]
        sink_row = jnp.repeat(a_sink[l], HD)[None]
        qkvg = S["qkvg"]

        dout, dgate = _resid_bwd("resid_bwd", dx, S["out"], gt_l)
        g_out[l] = _mm_tn("mm_out_dw", S["merged"], dout, 1, BF16).reshape(N_DEV, D // N_DEV, D)
        dm = _mm_nt("mm_out_dx", dout, Wo, 0, F32)
        dproj, dmgl = _merge_bwd("merge_bwd", dm, S["mgl"], S["proj"], D)
        g_br[l] = _branch_tn("mm_branch_dw", S["br"], dproj, D // N_DEV, BF16)
        dbr = _branch_nt("mm_branch_dx", dproj, wb_all, l, F32)
        (doa, dla, dob0, dob1, dob2, dlb0, dlb1, dlb2, doc, dlc, dod, dld, dg, dsk) = _gate_bwd(
            "gate_bwd", dbr, qkvg, S["oa"], S["la"], sink_row, S["ob"], S["lb"], S["oc"], S["od"])
        d_sink[l] = dsk.reshape(4, HD)[:, 0]

        dqa, dka, dva, _ = _attn_bwd("attn_a", att["A"], S["qa"], S["ka"], qkvg, doa, S["la"], dla)
        dqb, dkb, dvb = [], [], []
        for p, (dob, dlb) in enumerate(((dob0, dlb0), (dob1, dlb1), (dob2, dlb2))):
            a_ = att["B%d" % p]
            dq_, dk_, dv_, _ = _attn_bwd("attn_b%d" % p, a_, *views(a_, (S["qb"], S["kb"], S["vb"], dob, S["lb"][p], dlb)))
            dqb.append(dq_.reshape(T, 512))
            dkb.append(dk_.reshape(T, 512))
            dvb.append(dv_.reshape(T, 512))
        dqc, dkc, dvc, _ = _attn_bwd("attn_c", att["C"], S["qc"], S["kc"], qkvg, doc, S["lc"], dlc)
        dqd, dkd, dvd, dtab = _attn_bwd("attn_d", att["D"], qkvg, qkvg, qkvg, dod, S["ld"], dld, bias=S["btab"],
                                        bias_t=_na_bias_table(att["D"], d_rel_bias[l], True))
        d_bias[l] = _na_bias_grad(att["D"], dtab)

        dqkvg, dcq, dck = _qkvg_bwd("qkvg_bwd", qkvg, tabs, cqn, ckn, (dqa, dka, dva), (dqb, dkb, dvb),
                                    (dqc, dkc, dvc), (dqd, dkd, dvd), dg)
        d_cq[l], d_ck[l] = dcq[0], dck[0]
        g_in[l] = _mm_tn("mm_in_dw", S["h"], dqkvg, N_DEV, BF16)
        g_gm[l] = _mm_tn("mm_gm_dw", S["h"], dmgl, N_DEV, BF16)
        dh = _mm_nt("mm_in_dx", dqkvg, wi_all, l, F32)
        dh = _mm_nt("mm_gm_dx", dmgl, wg_all, l, F32, add=dh)
        dx, dshift, dscale, dng = _prenorm_bwd("prenorm_bwd", S["x"], dh, dx, ng_l, sc_l)
        d_norm_g[l] = dng[0]
        d_ada[l] = jnp.concatenate([dshift[0], dscale[0], dgate[0]])

    grad_x = dx.reshape(1, T, D)

    s_in, s_gm, s_br, s_out = _exchange(
        "scatter_grads", [jnp.stack(g_in, 1), jnp.stack(g_gm, 1), jnp.stack(g_br, 1), jnp.stack(g_out, 1)], gather=False)

    def big(name, w, m, v, slots):
        cols = w.shape[-1]
        res = _adamw(name, w.reshape(-1, cols), m.reshape(-1, cols), v.reshape(-1, cols),
                     slots.reshape(N_DEV, -1, cols), N_DEV)
        return [r.reshape(w.shape) for r in res]

    r_in = big("adamw_in", w_in, m_w_in, v_w_in, s_in)
    r_gm = big("adamw_gm", w_gate_merge, m_w_gate_merge, v_w_gate_merge, s_gm)
    r_br = big("adamw_branch", w_branch, m_w_branch, v_w_branch, s_br)
    r_out = big("adamw_out", w_out, m_w_out, v_w_out, s_out)

    small_w = [norm_g, b_ada, a_sink, c_q_norm, c_k_norm, d_rel_bias, final_g]
    small_m = [m_norm_g, m_b_ada, m_a_sink, m_c_q_norm, m_c_k_norm, m_d_rel_bias, m_final_g]
    small_v = [v_norm_g, v_b_ada, v_a_sink, v_c_q_norm, v_c_k_norm, v_d_rel_bias, v_final_g]
    small_g = [jnp.stack(d_norm_g), jnp.stack(d_ada), jnp.stack(d_sink), jnp.stack(d_cq), jnp.stack(d_ck),
               jnp.stack(d_bias), d_final_g[0]]
    sizes = [int(np.prod(w.shape)) for w in small_w]
    total = sum(sizes) + HD
    rows = -(-total // (8 * HD)) * 8

    def pack(parts, extra):
        flat = jnp.concatenate([p.reshape(-1).astype(F32) for p in parts] + [extra])
        return jnp.pad(flat, (0, rows * HD - flat.shape[0])).reshape(rows, HD)

    zeros = jnp.zeros((HD,), F32)
    (g_slots,) = _exchange("gather_small", [pack(small_g, loss_part[0])], gather=True)
    gs, ds, ms, vs = _adamw("adamw_small", pack(small_w, zeros), pack(small_m, zeros), pack(small_v, zeros), g_slots, N_DEV)

    def unpack(flat2d):
        flat = flat2d.reshape(-1)
        res, o = [], 0
        for w, n in zip(small_w, sizes):
            res.append(flat[o:o + n].reshape(w.shape))
            o += n
        return res, flat[o]

    sg, loss = unpack(gs)
    sd, _ = unpack(ds)
    sm, _ = unpack(ms)
    sv, _ = unpack(vs)

    o_b = sizes[0]
    d_ada_all = g_slots.reshape(N_DEV, -1)[:, o_b:o_b + sizes[1]].reshape(N_DEV, NL, N_DEV, n_ada)
    d_ada_mine = jnp.transpose(lax.dynamic_index_in_dim(d_ada_all, me, axis=2, keepdims=False), (1, 0, 2))
    g_ada = _ada_grad("ada_grad", c_pad, jnp.pad(d_ada_mine, ((0, 0), (0, HD - N_DEV), (0, 0))))
    r_ada = _adamw("adamw_ada", w_ada.reshape(-1, n_ada), m_w_ada.reshape(-1, n_ada), v_w_ada.reshape(-1, n_ada),
                   g_ada.reshape(-1, n_ada), 0)
    r_ada = [r.reshape(w_ada.shape) for r in r_ada]

    def kind(k):
        sm_ = (sg, sd, sm, sv)[k]
        return [sm_[0], r_ada[k], sm_[1], r_in[k], sm_[2], sm_[3], sm_[4], sm_[5], r_gm[k], r_br[k], r_out[k], sm_[6]]

    return (loss, grad_x, *kind(0), *kind(1), *kind(2), *kind(3))
```

```python
import jax
import jax.numpy as jnp
import numpy as np
from jax import lax
from jax.experimental import pallas as pl
from jax.experimental.pallas import tpu as pltpu

F32 = jnp.float32
BF16 = jnp.bfloat16

N_DEV = 8
HD = 128
GRID_W = 64
EPS = 1e-6
NEG_INF = -1e30
ROPE_THETA = 10000.0
A_REACH = 128
B_PATTERNS = ((128, 1), (512, 4), (2048, 16))
NA_ROWS = 8
NA_COLS = 16
SCALE = HD ** -0.5
LOG2E = 1.4426950408889634
LN2 = 0.6931471805599453
SCALE2 = SCALE * LOG2E
A_Q, A_K, A_V, A_G = 0, 4, 6, 8
B_Q, B_K, B_V, B_G = 12, 16, 20, 24
C_Q, C_K, C_V, C_G = 28, 32, 34, 36
D_Q, D_K, D_V, D_G = 40, 44, 48, 52
IN_HEADS = 56

ADAM_LR, ADAM_B1, ADAM_B2, ADAM_EPS, ADAM_WD, ADAM_STEP = 0.001, 0.9, 0.999, 1e-08, 0.01, 10

V7X_VMEM_LIMIT = 56 * 2 ** 20
ATT_TILE = 512


def _pc(body, **kw):
    return pl.pallas_call(body, **kw)


def _cparams(sem=None):
    if sem is None:
        return pltpu.CompilerParams(vmem_limit_bytes=V7X_VMEM_LIMIT)
    return pltpu.CompilerParams(dimension_semantics=sem, vmem_limit_bytes=V7X_VMEM_LIMIT)


def _tile(n, cap):
    if n <= cap:
        return n
    t = (cap // 128) * 128
    while n % t:
        t -= 128
    return t


def _sds(shape, dtype):
    return jax.ShapeDtypeStruct(tuple(shape), dtype)


def _exchange_out(ins, gather):
    return [_sds((N_DEV,) + (a.shape if gather else a.shape[1:]), a.dtype) for a in ins]


def _exchange_sems(n):
    return [pltpu.SemaphoreType.DMA((n, N_DEV - 1)), pltpu.SemaphoreType.DMA((n, N_DEV - 1)), pltpu.SemaphoreType.DMA((n,))]


def _exchange_copies(in_refs, out_refs, send_sems, recv_sems, loc_sems, gather):
    x, y, c = lax.axis_index("x"), lax.axis_index("y"), lax.axis_index("c")
    me = 4 * x + 2 * y + c
    copies = []
    for t, (src, dst) in enumerate(zip(in_refs, out_refs)):
        copies.append(pltpu.make_async_copy(src if gather else src.at[me], dst.at[me], loc_sems.at[t]))
    for p in range(1, N_DEV):
        tx = 1 - x if p & 4 else x
        ty = 1 - y if p & 2 else y
        tc = 1 - c if p & 1 else c
        peer = 4 * tx + 2 * ty + tc
        for t, (src, dst) in enumerate(zip(in_refs, out_refs)):
            copies.append(pltpu.make_async_remote_copy(
                src_ref=src if gather else src.at[peer], dst_ref=dst.at[me],
                send_sem=send_sems.at[t, p - 1], recv_sem=recv_sems.at[t, p - 1],
                device_id=(tx, ty, tc), device_id_type=pl.DeviceIdType.MESH))
    return copies


def _exchange(name, ins, gather):
    n = len(ins)

    def body(*refs):
        copies = _exchange_copies(refs[:n], refs[n:2 * n], *refs[2 * n:], gather)
        for cp in copies:
            cp.start()
        for cp in copies:
            cp.wait()

    res = _pc(body, name=name, out_shape=_exchange_out(ins, gather),
              in_specs=[pl.BlockSpec(memory_space=pl.ANY)] * n, out_specs=[pl.BlockSpec(memory_space=pl.ANY)] * n,
              scratch_shapes=_exchange_sems(n))(*ins)
    return list(res)


class _Rider:
    def __init__(self, ins, gather):
        self.ins, self.gather, self.n = list(ins), gather, len(ins)

    def specs(self):
        any_ = [pl.BlockSpec(memory_space=pl.ANY)] * self.n
        return any_, _exchange_out(self.ins, self.gather), any_, _exchange_sems(self.n)

    def start(self, in_refs, out_refs, sems):
        @pl.when((pl.program_id(0) == 0) & (pl.program_id(1) == 0))
        def _():
            for cp in _exchange_copies(in_refs, out_refs, *sems, self.gather):
                cp.start()

    def finish(self, in_refs, out_refs, sems, grid):
        @pl.when((pl.program_id(0) == grid[0] - 1) & (pl.program_id(1) == grid[1] - 1))
        def _():
            for cp in _exchange_copies(in_refs, out_refs, *sems, self.gather):
                cp.wait()


def _mm(name, a, w, out_sds, grid, a_spec, w_spec, o_spec, dims, red_axis=None, add=None, add_spec=None):
    nred = grid[red_axis] if red_axis is not None else 1
    acc_shape = tuple(d for d in o_spec.block_shape if d is not None)

    def body(*refs):
        if add is not None:
            a_ref, w_ref, add_ref, o_ref = refs[:4]
            scr = refs[4:]
        else:
            a_ref, w_ref, o_ref = refs[:3]
            scr = refs[3:]
        prod = lax.dot_general(a_ref[...], w_ref[...], dims, preferred_element_type=F32)

        def finish(acc):
            if add is not None:
                acc = acc + add_ref[...].astype(F32)
            o_ref[...] = acc.astype(o_ref.dtype)

        if nred == 1:
            finish(prod)
        else:
            acc_ref = scr[0]
            k = pl.program_id(red_axis)

            @pl.when(k == 0)
            def _():
                acc_ref[...] = prod

            @pl.when(k > 0)
            def _():
                acc_ref[...] += prod

            @pl.when(k == nred - 1)
            def _():
                finish(acc_ref[...])

    sem = tuple("arbitrary" if i == red_axis else "parallel" for i in range(len(grid)))
    in_specs = [a_spec, w_spec] + ([add_spec] if add is not None else [])
    args = (a, w) + ((add,) if add is not None else ())
    return _pc(body, name=name, out_shape=out_sds, grid=grid, in_specs=in_specs, out_specs=o_spec,
               scratch_shapes=[pltpu.VMEM(acc_shape, F32)] if nred > 1 else [],
               compiler_params=_cparams(sem))(*args)


_NN = (((1,), (0,)), ((), ()))
_NT = (((1,), (1,)), ((), ()))
_TN = (((0,), (0,)), ((), ()))


def _mm_nn(name, a, w, out_dtype):
    T, K = a.shape
    G, _, n = w.shape
    tm, tn = min(T, 1024), _tile(n, 1024)
    nj = n // tn
    return _mm(name, a, w, _sds((T, G * n), out_dtype), (T // tm, G, nj),
               pl.BlockSpec((tm, K), lambda i, g, j: (i, 0)),
               pl.BlockSpec((None, K, tn), lambda i, g, j: (g, 0, j)),
               pl.BlockSpec((tm, tn), lambda i, g, j: (i, g * nj + j)), _NN)


def _mm_nt(name, dy, w, out_dtype, add=None):
    T = dy.shape[0]
    G, K, n = w.shape
    tm, tk = min(T, 1024), _tile(K, 1024)
    tn = _tile(n, 1024)
    nj = n // tn
    o_spec = pl.BlockSpec((tm, tk), lambda i, k, r: (i, k))
    return _mm(name, dy, w, _sds((T, K), out_dtype), (T // tm, K // tk, G * nj),
               pl.BlockSpec((tm, tn), lambda i, k, r: (i, r)),
               pl.BlockSpec((None, tk, tn), lambda i, k, r: (r // nj, k, r % nj)),
               o_spec, _NT, red_axis=2, add=add, add_spec=o_spec if add is not None else None)


def _mm_tn(name, a, dy, G, out_dtype):
    T, K = a.shape
    n = dy.shape[1] // G
    tt, tk, tn = min(T, 1024), _tile(K, 1024), _tile(n, 1024)
    nj = n // tn
    return _mm(name, a, dy, _sds((G, K, n), out_dtype), (K // tk, G, nj, T // tt),
               pl.BlockSpec((tt, tk), lambda k, g, j, t: (t, k)),
               pl.BlockSpec((tt, tn), lambda k, g, j, t: (t, g * nj + j)),
               pl.BlockSpec((None, tk, tn), lambda k, g, j, t: (g, k, j)), _TN, red_axis=3)


def _branch_nn(name, br, wb, out_dtype):
    T = br.shape[0]
    _, NB, W, n = wb.shape
    tm = min(T, 1024)
    return _mm(name, br, wb, _sds((T, NB * N_DEV * n), out_dtype), (T // tm, NB, N_DEV),
               pl.BlockSpec((tm, W), lambda i, b, d: (i, b)),
               pl.BlockSpec((None, None, W, n), lambda i, b, d: (d, b, 0, 0)),
               pl.BlockSpec((tm, n), lambda i, b, d: (i, b * N_DEV + d)), _NN)


def _branch_nt(name, dproj, wb, out_dtype):
    T = dproj.shape[0]
    _, NB, W, n = wb.shape
    tm = min(T, 1024)
    return _mm(name, dproj, wb, _sds((T, NB * W), out_dtype), (T // tm, NB, N_DEV),
               pl.BlockSpec((tm, n), lambda i, b, d: (i, b * N_DEV + d)),
               pl.BlockSpec((None, None, W, n), lambda i, b, d: (d, b, 0, 0)),
               pl.BlockSpec((tm, W), lambda i, b, d: (i, b)), _NT, red_axis=2)


def _branch_tn(name, br, dproj, n, out_dtype):
    T = br.shape[0]
    NB = 4
    W = br.shape[1] // NB
    tt = min(T, 1024)
    return _mm(name, br, dproj, _sds((N_DEV, NB, W, n), out_dtype), (NB, N_DEV, T // tt),
               pl.BlockSpec((tt, W), lambda b, d, t: (t, b)),
               pl.BlockSpec((tt, n), lambda b, d, t: (t, b * N_DEV + d)),
               pl.BlockSpec((None, None, W, n), lambda b, d, t: (d, b, 0, 0)), _TN, red_axis=2)


class _Att:
    def __init__(self, L, R, H, G, qcol, kcol, vcol, qstr, kstr, vstr, mode, reach=0, pad=0, rows=0):
        self.L, self.R, self.H, self.G = L, R, H, G
        self.qcol, self.kcol, self.vcol = qcol, kcol, vcol
        self.qstr, self.kstr, self.vstr = qstr, kstr, vstr
        self.mode, self.reach, self.rows = mode, reach, rows
        self.t = min(ATT_TILE, L)
        self.nb = L // self.t
        self.pieces = mode != "dense" and self.nb > 1
        self.pad = pad if self.pieces else 0
        self.span = self.t + 2 * self.pad
        self.ppb = self.t // pad if self.pieces else 0

    def qc(self, r, h):
        return r * self.qstr + self.qcol + h

    def kc(self, r, h):
        return r * self.kstr + self.kcol + h // self.G

    def vc(self, r, h):
        return r * self.vstr + self.vcol + h // self.G

    def oc(self, r, h):
        return r * self.H + h

    def mask(self, qpos, kpos, spos):
        if self.mode == "band":
            ok = jnp.abs(qpos - kpos) <= self.reach
        else:
            rq, cq = qpos >> 6, qpos & (GRID_W - 1)
            rk, ck = kpos >> 6, kpos & (GRID_W - 1)
            rs = jnp.clip(rq - NA_ROWS // 2, 0, self.rows - NA_ROWS)
            cs = jnp.clip(cq - NA_COLS // 2, 0, GRID_W - NA_COLS)
            ok = (rk >= rs) & (rk < rs + NA_ROWS) & (ck >= cs) & (ck < cs + NA_COLS)
        if self.pieces:
            ok = ok & (spos >= 0) & (spos < self.L)
        return ok

    def span_specs(self, col):
        t, pad, ppb = self.t, self.pad, self.ppb
        cur = pl.BlockSpec((t, HD), lambda r, h, i: (i, col(r, h)))
        if not self.pieces:
            return [cur]
        last = self.L // pad - 1
        prev = pl.BlockSpec((pad, HD), lambda r, h, i: (jnp.maximum(i * ppb - 1, 0), col(r, h)))
        nxt = pl.BlockSpec((pad, HD), lambda r, h, i: (jnp.minimum((i + 1) * ppb, last), col(r, h)))
        return [prev, cur, nxt]

    def positions(self, i):
        own = i * self.t + lax.broadcasted_iota(jnp.int32, (self.t, 1), 0)
        spn = i * self.t - self.pad + lax.broadcasted_iota(jnp.int32, (1, self.span), 1)
        return own, spn


def _cat(refs):
    return refs[0][...] if len(refs) == 1 else jnp.concatenate([r[...] for r in refs], axis=0)


def _band_fwd(name, att, q, k, v, sink=None, bias=None):
    t, ns = att.t, (3 if att.pieces else 1)

    def body(*refs):
        q_ref, k_refs, v_refs = refs[0], refs[1:1 + ns], refs[1 + ns:1 + 2 * ns]
        pos = 1 + 2 * ns
        sink_ref = bias_ref = None
        if sink is not None:
            sink_ref = refs[pos]
            pos += 1
        if bias is not None:
            bias_ref = refs[pos]
            pos += 1
        o_ref, lse_ref = refs[pos:]
        ks, vs = _cat(k_refs), _cat(v_refs)
        s = lax.dot_general(q_ref[...], ks, _NT, preferred_element_type=F32)
        if bias_ref is not None:
            s = s + bias_ref[...]
        qpos, kpos = att.positions(pl.program_id(2))
        s = jnp.where(att.mask(qpos, kpos, kpos), s, NEG_INF)
        m = jnp.max(s, axis=1, keepdims=True)
        if sink_ref is not None:
            sk = sink_ref[...][:, :1] * LOG2E
            m = jnp.maximum(m, sk)
        p = jnp.exp2(s - m)
        den = jnp.sum(p, axis=1, keepdims=True)
        if sink_ref is not None:
            den = den + jnp.exp2(sk - m)
        o = jnp.dot(p.astype(BF16), vs, preferred_element_type=F32)
        o_ref[...] = (o / den).astype(o_ref.dtype)
        lse_ref[...] = jnp.broadcast_to((m + jnp.log2(den)) * LN2, (t, HD))

    in_specs = [pl.BlockSpec((t, HD), lambda r, h, i: (i, att.qc(r, h)))] + att.span_specs(att.kc) + att.span_specs(att.vc)
    args = [q] + [k] * ns + [v] * ns
    if sink is not None:
        in_specs.append(pl.BlockSpec((None, 1, HD), lambda r, h, i: (h, 0, 0)))
        args.append(sink)
    if bias is not None:
        in_specs.append(pl.BlockSpec((None, t, att.span), lambda r, h, i: (h, 0, 0)))
        args.append(bias)
    o_spec = pl.BlockSpec((t, HD), lambda r, h, i: (i, att.oc(r, h)))
    cols = att.R * att.H * HD
    return _pc(body, name=name, grid=(att.R, att.H, att.nb),
               out_shape=[_sds((att.L, cols), BF16), _sds((att.L, cols), F32)],
               in_specs=in_specs, out_specs=[o_spec, o_spec],
               compiler_params=_cparams(("parallel", "parallel", "parallel")))(*args)


def _band_dq(name, att, q, k, v, do, lse, delta, bias=None):
    t, ns = att.t, (3 if att.pieces else 1)

    def body(*refs):
        q_ref, k_refs, v_refs = refs[0], refs[1:1 + ns], refs[1 + ns:1 + 2 * ns]
        do_ref, lse_ref, dl_ref = refs[1 + 2 * ns:4 + 2 * ns]
        pos = 4 + 2 * ns
        bias_ref = dtab_ref = None
        if bias is not None:
            bias_ref = refs[pos]
            pos += 1
        dq_ref = refs[pos]
        if bias is not None:
            dtab_ref = refs[pos + 1]
        i = pl.program_id(2)
        ks, vs = _cat(k_refs), _cat(v_refs)
        s = lax.dot_general(q_ref[...], ks, _NT, preferred_element_type=F32)
        if bias_ref is not None:
            s = s + bias_ref[...]
        qpos, kpos = att.positions(i)
        s = jnp.where(att.mask(qpos, kpos, kpos), s, NEG_INF)
        p = jnp.exp2(s - lse_ref[...][:, :1] * LOG2E)
        dp = lax.dot_general(do_ref[...], vs, _NT, preferred_element_type=F32)
        ds = p * (dp - dl_ref[...][:, :1])
        if dtab_ref is not None:
            @pl.when(i == 0)
            def _():
                dtab_ref[...] = ds

            @pl.when(i > 0)
            def _():
                dtab_ref[...] += ds
        dq_ref[...] = jnp.dot(ds.astype(BF16), ks, preferred_element_type=F32) * SCALE

    row = pl.BlockSpec((t, HD), lambda r, h, i: (i, att.oc(r, h)))
    in_specs = [pl.BlockSpec((t, HD), lambda r, h, i: (i, att.qc(r, h)))] + att.span_specs(att.kc) + att.span_specs(att.vc)
    in_specs += [row, row, row]
    args = [q] + [k] * ns + [v] * ns + [do, lse, delta]
    out_shape = [_sds((att.L, att.R * att.H * HD), F32)]
    out_specs = [row]
    sem = ("parallel", "parallel", "parallel")
    if bias is not None:
        tab = pl.BlockSpec((None, t, att.span), lambda r, h, i: (h, 0, 0))
        in_specs.append(tab)
        args.append(bias)
        out_shape.append(_sds((att.H, t, att.span), F32))
        out_specs.append(tab)
        sem = ("parallel", "parallel", "arbitrary")
    return _pc(body, name=name, grid=(att.R, att.H, att.nb), out_shape=out_shape, in_specs=in_specs,
               out_specs=out_specs, compiler_params=_cparams(sem))(*args)


def _band_dkv(name, att, q, k, v, do, lse_sp, dl_sp, bias_t=None):
    t, ns = att.t, (3 if att.pieces else 1)

    def body(*refs):
        k_ref, v_ref = refs[:2]
        q_refs, do_refs = refs[2:2 + ns], refs[2 + ns:2 + 2 * ns]
        lse_ref, dl_ref = refs[2 + 2 * ns:4 + 2 * ns]
        pos = 4 + 2 * ns
        bias_ref = None
        if bias_t is not None:
            bias_ref = refs[pos]
            pos += 1
        dk_ref, dv_ref = refs[pos:]
        qs, dos = _cat(q_refs), _cat(do_refs)
        st = lax.dot_general(k_ref[...], qs, _NT, preferred_element_type=F32)
        if bias_ref is not None:
            st = st + bias_ref[...]
        kpos, qpos = att.positions(pl.program_id(2))
        st = jnp.where(att.mask(qpos, kpos, qpos), st, NEG_INF)
        pt = jnp.exp2(st - lse_ref[...] * LOG2E)
        dv_ref[...] = jnp.dot(pt.astype(BF16), dos, preferred_element_type=F32)
        dpt = lax.dot_general(v_ref[...], dos, _NT, preferred_element_type=F32)
        dst = pt * (dpt - dl_ref[...])
        dk_ref[...] = jnp.dot(dst.astype(BF16), qs, preferred_element_type=F32) * LN2

    stat = pl.BlockSpec((None, None, 1, att.span), lambda r, h, i: (r * att.H + h, i, 0, 0))
    in_specs = [pl.BlockSpec((t, HD), lambda r, h, i: (i, att.kc(r, h))),
                pl.BlockSpec((t, HD), lambda r, h, i: (i, att.vc(r, h)))]
    in_specs += att.span_specs(att.qc) + att.span_specs(att.oc) + [stat, stat]
    args = [k, v] + [q] * ns + [do] * ns + [lse_sp, dl_sp]
    if bias_t is not None:
        in_specs.append(pl.BlockSpec((None, t, att.span), lambda r, h, i: (h, 0, 0)))
        args.append(bias_t)
    o_spec = pl.BlockSpec((t, HD), lambda r, h, i: (i, att.oc(r, h)))
    cols = att.R * att.H * HD
    return _pc(body, name=name, grid=(att.R, att.H, att.nb), out_shape=[_sds((att.L, cols), F32)] * 2,
               in_specs=in_specs, out_specs=[o_spec, o_spec],
               compiler_params=_cparams(("parallel", "parallel", "parallel")))(*args)


def _dense_fwd(name, att, q, k, v, rider=None):
    t, L, nb = att.t, att.L, att.nb
    grid = (att.H, nb)
    nr = rider.n if rider is not None else 0

    def body(*refs):
        q_ref, k_ref, v_ref = refs[:3]
        o_ref, lse_ref = refs[3 + nr:5 + nr]
        comm = (refs[3:3 + nr], refs[5 + nr:5 + 2 * nr], refs[5 + 2 * nr:])
        if rider is not None:
            rider.start(*comm)
        qq = q_ref[...]

        def step(j, carry):
            m, den, acc = carry
            rows = pl.ds(pl.multiple_of(j * t, t), t)
            s = lax.dot_general(qq, k_ref[rows, :], _NT, preferred_element_type=F32)
            m_new = jnp.maximum(m, jnp.max(s, axis=1, keepdims=True))
            alpha = jnp.exp2(m - m_new)
            p = jnp.exp2(s - m_new)
            den = alpha * den + jnp.sum(p, axis=1, keepdims=True)
            acc = alpha * acc + jnp.dot(p.astype(BF16), v_ref[rows, :], preferred_element_type=F32)
            return m_new, den, acc

        init = (jnp.full((t, 1), NEG_INF, F32), jnp.zeros((t, 1), F32), jnp.zeros((t, HD), F32))
        m, den, acc = lax.fori_loop(0, nb, step, init, unroll=2 if nb % 2 == 0 else 1)
        o_ref[...] = (acc / den).astype(o_ref.dtype)
        lse_ref[...] = jnp.broadcast_to((m + jnp.log2(den)) * LN2, (t, HD))
        if rider is not None:
            rider.finish(*comm, grid)

    o_spec = pl.BlockSpec((t, HD), lambda h, i: (i, att.oc(0, h)))
    r_in, r_out, r_outspec, r_sems = rider.specs() if rider is not None else ([], [], [], [])
    sem = ("arbitrary", "arbitrary") if rider is not None else ("parallel", "parallel")
    return _pc(body, name=name, grid=grid,
               out_shape=[_sds((L, att.H * HD), BF16), _sds((L, att.H * HD), F32)] + r_out,
               in_specs=[pl.BlockSpec((t, HD), lambda h, i: (i, att.qc(0, h))),
                         pl.BlockSpec((L, HD), lambda h, i: (0, att.kc(0, h))),
                         pl.BlockSpec((L, HD), lambda h, i: (0, att.vc(0, h)))] + r_in,
               out_specs=[o_spec, o_spec] + r_outspec, scratch_shapes=r_sems,
               compiler_params=_cparams(sem))(q, k, v, *(rider.ins if rider is not None else []))


def _dense_dq(name, att, q, k, v, do, lse, delta):
    t, L, nb = att.t, att.L, att.nb

    def body(q_ref, k_ref, v_ref, do_ref, lse_ref, dl_ref, dq_ref):
        qq, dd = q_ref[...], do_ref[...]
        lse2 = lse_ref[...][:, :1] * LOG2E
        dl = dl_ref[...][:, :1]

        def step(j, acc):
            rows = pl.ds(pl.multiple_of(j * t, t), t)
            kk = k_ref[rows, :]
            s = lax.dot_general(qq, kk, _NT, preferred_element_type=F32)
            p = jnp.exp2(s - lse2)
            dp = lax.dot_general(dd, v_ref[rows, :], _NT, preferred_element_type=F32)
            ds = p * (dp - dl)
            return acc + jnp.dot(ds.astype(BF16), kk, preferred_element_type=F32)

        acc = lax.fori_loop(0, nb, step, jnp.zeros((t, HD), F32), unroll=2 if nb % 2 == 0 else 1)
        dq_ref[...] = acc * SCALE

    row = pl.BlockSpec((t, HD), lambda h, i: (i, att.oc(0, h)))
    return _pc(body, name=name, grid=(att.H, nb), out_shape=_sds((L, att.H * HD), F32),
               in_specs=[pl.BlockSpec((t, HD), lambda h, i: (i, att.qc(0, h))),
                         pl.BlockSpec((L, HD), lambda h, i: (0, att.kc(0, h))),
                         pl.BlockSpec((L, HD), lambda h, i: (0, att.vc(0, h))), row, row, row],
               out_specs=row, compiler_params=_cparams(("parallel", "parallel")))(q, k, v, do, lse, delta)


def _dense_dkv(name, att, q, k, v, do, lse_rows, dl_rows, rider=None):
    t, L, nb = att.t, att.L, att.nb
    grid = (att.H, nb)
    nr = rider.n if rider is not None else 0

    def body(*refs):
        k_ref, v_ref, q_ref, do_ref, lse_ref, dl_ref = refs[:6]
        dk_ref, dv_ref = refs[6 + nr:8 + nr]
        comm = (refs[6:6 + nr], refs[8 + nr:8 + 2 * nr], refs[8 + 2 * nr:])
        if rider is not None:
            rider.start(*comm)
        kk, vv = k_ref[...], v_ref[...]

        def step(i, carry):
            dk, dv = carry
            rows = pl.ds(pl.multiple_of(i * t, t), t)
            qq, dd = q_ref[rows, :], do_ref[rows, :]
            st = lax.dot_general(kk, qq, _NT, preferred_element_type=F32)
            pt = jnp.exp2(st - lse_ref[i] * LOG2E)
            dv = dv + jnp.dot(pt.astype(BF16), dd, preferred_element_type=F32)
            dpt = lax.dot_general(vv, dd, _NT, preferred_element_type=F32)
            dst = pt * (dpt - dl_ref[i])
            dk = dk + jnp.dot(dst.astype(BF16), qq, preferred_element_type=F32)
            return dk, dv

        zero = jnp.zeros((t, HD), F32)
        dk, dv = lax.fori_loop(0, nb, step, (zero, zero), unroll=2 if nb % 2 == 0 else 1)
        dk_ref[...] = dk * LN2
        dv_ref[...] = dv
        if rider is not None:
            rider.finish(*comm, grid)

    stat = pl.BlockSpec((None, nb, 1, t), lambda h, j: (h, 0, 0, 0))
    o_spec = pl.BlockSpec((t, HD), lambda h, j: (j, att.oc(0, h)))
    r_in, r_out, r_outspec, r_sems = rider.specs() if rider is not None else ([], [], [], [])
    sem = ("arbitrary", "arbitrary") if rider is not None else ("parallel", "parallel")
    return _pc(body, name=name, grid=grid, out_shape=[_sds((L, att.H * HD), F32)] * 2 + r_out,
               in_specs=[pl.BlockSpec((t, HD), lambda h, j: (j, att.kc(0, h))),
                         pl.BlockSpec((t, HD), lambda h, j: (j, att.vc(0, h))),
                         pl.BlockSpec((L, HD), lambda h, j: (0, att.qc(0, h))),
                         pl.BlockSpec((L, HD), lambda h, j: (0, att.oc(0, h))), stat, stat] + r_in,
               out_specs=[o_spec, o_spec] + r_outspec, scratch_shapes=r_sems,
               compiler_params=_cparams(sem))(k, v, q, do, lse_rows, dl_rows, *(rider.ins if rider is not None else []))


def _stat_rows(att, col):
    RH = att.R * att.H
    rows = col.reshape(att.L, RH, HD)[:, :, 0].T
    if not att.pieces:
        return rows.reshape(RH, att.nb, 1, att.t)
    padded = jnp.pad(rows, ((0, 0), (att.pad, att.pad)))
    return jnp.stack([padded[:, j * att.t:j * att.t + att.span] for j in range(att.nb)], axis=1)[:, :, None, :]


def _attn_fwd(name, att, q, k, v, sink=None, bias=None, rider=None):
    if att.mode == "dense":
        return _dense_fwd(name, att, q, k, v, rider=rider)
    return _band_fwd(name, att, q, k, v, sink=sink, bias=bias)


def _attn_bwd(name, att, q, k, v, do, lse, delta, bias=None, bias_t=None, rider=None):
    lse_r, dl_r = _stat_rows(att, lse), _stat_rows(att, delta)
    if att.mode == "dense":
        dq = _dense_dq(name + "_dq", att, q, k, v, do, lse, delta)
        res = _dense_dkv(name + "_dkv", att, q, k, v, do, lse_r, dl_r, rider=rider)
        return dq, res[0], res[1], list(res[2:])
    res = _band_dq(name + "_dq", att, q, k, v, do, lse, delta, bias=bias)
    dk, dv = _band_dkv(name + "_dkv", att, q, k, v, do, lse_r, dl_r, bias_t=bias_t)
    return res[0], dk, dv, (res[1] if bias is not None else None)


def _row_call(name, body, T, tt, ins, outs, acc_outs=()):
    in_specs, args = [], []
    for item in ins:
        a = item[0]
        if item[1] is None:
            in_specs.append(pl.BlockSpec(a.shape, lambda i, nd=a.ndim: (0,) * nd))
        else:
            in_specs.append(pl.BlockSpec((tt, item[1]), lambda i, cb=item[2]: (i, cb)))
        args.append(a)
    out_shape = [_sds((T, c), d) for c, d in outs] + [_sds(s, F32) for s in acc_outs]
    out_specs = [pl.BlockSpec((tt, c), lambda i: (i, 0)) for c, _ in outs]
    out_specs += [pl.BlockSpec(s, lambda i, nd=len(s): (0,) * nd) for s in acc_outs]
    sem = ("arbitrary",) if acc_outs else ("parallel",)
    return _pc(body, name=name, grid=(T // tt,), out_shape=out_shape, in_specs=in_specs, out_specs=out_specs,
               compiler_params=_cparams(sem))(*args)


def _acc(ref, val):
    @pl.when(pl.program_id(0) == 0)
    def _():
        ref[...] = val

    @pl.when(pl.program_id(0) > 0)
    def _():
        ref[...] += val


def _prenorm_fwd(name, x, g, scale, shift):
    T, D = x.shape

    def body(x_ref, g_ref, sc_ref, sh_ref, h_ref):
        xf = x_ref[...]
        r = lax.rsqrt(jnp.mean(xf * xf, axis=1, keepdims=True) + EPS)
        h_ref[...] = ((xf * r) * g_ref[...] * (1.0 + sc_ref[...]) + sh_ref[...]).astype(BF16)

    return _row_call(name, body, T, min(T, 256), [(x, D, 0), (g, None), (scale, None), (shift, None)], [(D, BF16)])[0]


def _prenorm_bwd(name, x, dh, dxn, g, scale):
    T, D = x.shape

    def body(x_ref, dh_ref, dxn_ref, g_ref, sc_ref, dx_ref, dsh_ref, dsc_ref, dg_ref):
        xf, dh_, gg = x_ref[...], dh_ref[...], g_ref[...]
        r = lax.rsqrt(jnp.mean(xf * xf, axis=1, keepdims=True) + EPS)
        u = xf * r
        dn = dh_ * (1.0 + sc_ref[...])
        du = dn * gg
        dx_ref[...] = dxn_ref[...] + r * (du - u * jnp.mean(du * u, axis=1, keepdims=True))
        _acc(dsh_ref, jnp.sum(dh_, axis=0, keepdims=True))
        _acc(dsc_ref, jnp.sum(dh_ * (u * gg), axis=0, keepdims=True))
        _acc(dg_ref, jnp.sum(dn * u, axis=0, keepdims=True))

    return _row_call(name, body, T, min(T, 256), [(x, D, 0), (dh, D, 0), (dxn, D, 0), (g, None), (scale, None)],
                     [(D, F32)], [(1, D)] * 3)


def _resid_fwd(name, x, out, gate):
    T, D = x.shape

    def body(x_ref, o_ref, g_ref, y_ref):
        y_ref[...] = x_ref[...] + g_ref[...] * o_ref[...]

    return _row_call(name, body, T, min(T, 256), [(x, D, 0), (out, D, 0), (gate, None)], [(D, F32)])[0]


def _resid_bwd(name, dx, out, gate):
    T, D = dx.shape

    def body(dx_ref, o_ref, g_ref, do_ref, dg_ref):
        d = dx_ref[...]
        do_ref[...] = (d * g_ref[...]).astype(BF16)
        _acc(dg_ref, jnp.sum(d * o_ref[...], axis=0, keepdims=True))

    return _row_call(name, body, T, min(T, 256), [(dx, D, 0), (out, D, 0), (gate, None)], [(D, BF16)], [(1, D)])


def _final(name, x, tgt, g):
    T, D = x.shape

    def body(x_ref, t_ref, g_ref, dx_ref, loss_ref, dg_ref):
        xf, gg = x_ref[...], g_ref[...]
        r = lax.rsqrt(jnp.mean(xf * xf, axis=1, keepdims=True) + EPS)
        u = xf * r
        err = u * gg - t_ref[...]
        part = 0.5 * jnp.sum(jnp.mean(err * err, axis=1, keepdims=True), axis=0, keepdims=True)
        _acc(loss_ref, jnp.broadcast_to(part, (1, HD)))
        dy = err * (1.0 / D)
        _acc(dg_ref, jnp.sum(dy * u, axis=0, keepdims=True))
        du = dy * gg
        dx_ref[...] = r * (du - u * jnp.mean(du * u, axis=1, keepdims=True))

    return _row_call(name, body, T, min(T, 256), [(x, D, 0), (tgt, D, 0), (g, None)], [(D, F32)], [(1, HD), (1, D)])


def _roll_pair(x, shift):
    lanes = lax.broadcasted_iota(jnp.int32, x.shape, 1)
    r1 = pltpu.roll(x, shift, 1)
    r2 = pltpu.roll(x, HD - shift, 1)
    src = pltpu.roll(lanes, shift, 1)
    is_plus = src == ((lanes + shift) & (HD - 1))
    return jnp.where(is_plus, r1, r2), jnp.where(is_plus, r2, r1)


def _rope1(x, cos, ss):
    xp, _ = _roll_pair(x, 64)
    return x * cos + xp * ss


def _rope1_t(d, cos, ss):
    dp, _ = _roll_pair(d * ss, 64)
    return d * cos + dp


def _ropex(x, cos, sa, sb):
    xp, xm = _roll_pair(x, 32)
    return x * cos + xp * sa + xm * sb


def _ropex_t(d, cos, sa, sb):
    _, am = _roll_pair(d * sa, 32)
    bp, _ = _roll_pair(d * sb, 32)
    return d * cos + am + bp


def _qk_prep(name, qkvg, tabs, cqn, ckn):
    T = qkvg.shape[0]
    cos1, ss1, cosx, sax, sbx = tabs

    def body(aq, ak, bq, bk, bv, cq, ck, dq, c1, s1, cx, ax, bx, gq, gk, oaq, oak, obq, obk, obv, ocq, ock, odq):
        c1v, s1v = c1[...], s1[...]
        for src, dst, nh, mul in ((aq, oaq, 4, SCALE2), (ak, oak, 2, None), (bq, obq, 4, SCALE2), (bk, obk, 4, None)):
            for h in range(nh):
                sl = slice(h * HD, (h + 1) * HD)
                y = _rope1(src[:, sl].astype(F32), c1v, s1v)
                dst[:, sl] = (y if mul is None else y * mul).astype(BF16)
        obv[...] = bv[...]
        odq[...] = (dq[...].astype(F32) * SCALE2).astype(BF16)
        cxv, axv, bxv = cx[...], ax[...], bx[...]
        for src, dst, nh, gref, mul in ((cq, ocq, 4, gq, SCALE2), (ck, ock, 2, gk, None)):
            for h in range(nh):
                sl = slice(h * HD, (h + 1) * HD)
                xf = src[:, sl].astype(F32)
                r = lax.rsqrt(jnp.mean(xf * xf, axis=1, keepdims=True) + EPS)
                y = _ropex(xf * r * gref[...], cxv, axv, bxv)
                dst[:, sl] = (y if mul is None else y * mul).astype(BF16)

    ins = [(qkvg, 512, 0), (qkvg, 256, 2), (qkvg, 512, 3), (qkvg, 512, 4), (qkvg, 512, 5), (qkvg, 512, 7),
           (qkvg, 256, 16), (qkvg, 512, 10),
           (cos1, HD, 0), (ss1, HD, 0), (cosx, HD, 0), (sax, HD, 0), (sbx, HD, 0), (cqn, None), (ckn, None)]
    outs = [(512, BF16), (256, BF16), (512, BF16), (512, BF16), (512, BF16), (512, BF16), (256, BF16), (512, BF16)]
    return _row_call(name, body, T, min(T, 512), ins, outs)


def _silu_parts(g):
    sig = 1.0 / (1.0 + jnp.exp(-g))
    return g * sig, sig * (1.0 + g * (1.0 - sig))


def _mix_weights(l0, l1, l2):
    mx = jnp.maximum(jnp.maximum(l0, l1), l2)
    e0, e1, e2 = jnp.exp(l0 - mx), jnp.exp(l1 - mx), jnp.exp(l2 - mx)
    inv = 1.0 / (e0 + e1 + e2)
    return e0 * inv, e1 * inv, e2 * inv


def _gate_fwd(name, qkvg, oa, ob, lb, oc, od):
    T = qkvg.shape[0]

    def body(ga, gb, gc, gd, oa_r, ob0, ob1, ob2, lb0, lb1, lb2, oc_r, od_r, br):
        w0, w1, w2 = _mix_weights(lb0[...], lb1[...], lb2[...])
        yb = w0 * ob0[...].astype(F32) + w1 * ob1[...].astype(F32) + w2 * ob2[...].astype(F32)
        ys = (oa_r[...].astype(F32), yb, oc_r[...].astype(F32), od_r[...].astype(F32))
        for n, (y, g) in enumerate(zip(ys, (ga, gb, gc, gd))):
            act, _ = _silu_parts(g[...].astype(F32))
            br[:, n * 512:(n + 1) * 512] = (y * act).astype(BF16)

    ins = [(qkvg, 512, 2), (qkvg, 512, 6), (qkvg, 512, 9), (qkvg, 512, 13), (oa, 512, 0)]
    ins += [(o, 512, 0) for o in ob] + [(l_, 512, 0) for l_ in lb] + [(oc, 512, 0), (od, 512, 0)]
    return _row_call(name, body, T, min(T, 256), ins, [(2048, BF16)])[0]


def _head_rowsum(x):
    parts = []
    for h in range(x.shape[1] // HD):
        s = jnp.sum(x[:, h * HD:(h + 1) * HD], axis=1, keepdims=True)
        parts.append(jnp.broadcast_to(s, (x.shape[0], HD)))
    return jnp.concatenate(parts, axis=1)


def _gate_bwd(name, dbr, qkvg, oa, la, sink_row, ob, lb, oc, od):
    T = qkvg.shape[0]

    def body(dbr_r, ga, gb, gc, gd, oa_r, la_r, sk, ob0, ob1, ob2, lb0, lb1, lb2, oc_r, od_r,
             doa, dla, dob0, dob1, dob2, dlb0, dlb1, dlb2, doc, dlc, dod, dld, dg, dsk):
        def one(n, g_ref, y):
            act, dact = _silu_parts(g_ref[...].astype(F32))
            d = dbr_r[:, n * 512:(n + 1) * 512]
            dg[:, n * 512:(n + 1) * 512] = (d * y * dact).astype(BF16)
            return d * act

        ya = oa_r[...].astype(F32)
        dya = one(0, ga, ya)
        doa[...] = dya.astype(BF16)
        dl_a = _head_rowsum(dya * ya)
        dla[...] = dl_a
        _acc(dsk, -jnp.sum(jnp.exp(sk[...] - la_r[...]) * dl_a, axis=0, keepdims=True))

        w = _mix_weights(lb0[...], lb1[...], lb2[...])
        obs = (ob0[...].astype(F32), ob1[...].astype(F32), ob2[...].astype(F32))
        yb = w[0] * obs[0] + w[1] * obs[1] + w[2] * obs[2]
        dyb = one(1, gb, yb)
        rs = _head_rowsum(dyb * yb)
        for wp, do_ref, dl_ref in zip(w, (dob0, dob1, dob2), (dlb0, dlb1, dlb2)):
            do_ref[...] = (wp * dyb).astype(BF16)
            dl_ref[...] = wp * rs

        for n, g_ref, o_r, do_ref, dl_ref in ((2, gc, oc_r, doc, dlc), (3, gd, od_r, dod, dld)):
            y = o_r[...].astype(F32)
            dy = one(n, g_ref, y)
            do_ref[...] = dy.astype(BF16)
            dl_ref[...] = _head_rowsum(dy * y)

    ins = [(dbr, 2048, 0), (qkvg, 512, 2), (qkvg, 512, 6), (qkvg, 512, 9), (qkvg, 512, 13),
           (oa, 512, 0), (la, 512, 0), (sink_row, None)]
    ins += [(o, 512, 0) for o in ob] + [(l_, 512, 0) for l_ in lb] + [(oc, 512, 0), (od, 512, 0)]
    outs = [(512, BF16), (512, F32)] + [(512, BF16)] * 3 + [(512, F32)] * 3 + [(512, BF16), (512, F32)] * 2
    outs += [(2048, BF16)]
    return _row_call(name, body, T, min(T, 256), ins, outs, [(1, 512)])


def _merge_fwd(name, mgl, proj, D):
    T = mgl.shape[0]

    def body(m_ref, p_ref, o_ref):
        acc = None
        for n in range(4):
            sl = slice(n * D, (n + 1) * D)
            sig = 1.0 / (1.0 + jnp.exp(-m_ref[:, sl].astype(F32)))
            term = sig * p_ref[:, sl].astype(F32)
            acc = term if acc is None else acc + term
        o_ref[...] = acc.astype(BF16)

    return _row_call(name, body, T, min(T, 256), [(mgl, 4 * D, 0), (proj, 4 * D, 0)], [(D, BF16)])[0]


def _merge_bwd(name, dm, mgl, proj, D):
    T = mgl.shape[0]

    def body(d_ref, m_ref, p_ref, dp_ref, dl_ref):
        d = d_ref[...]
        for n in range(4):
            sl = slice(n * D, (n + 1) * D)
            sig = 1.0 / (1.0 + jnp.exp(-m_ref[:, sl].astype(F32)))
            dp_ref[:, sl] = (d * sig).astype(BF16)
            dl_ref[:, sl] = (d * p_ref[:, sl].astype(F32) * sig * (1.0 - sig)).astype(BF16)

    return _row_call(name, body, T, min(T, 128), [(dm, D, 0), (mgl, 4 * D, 0), (proj, 4 * D, 0)],
                     [(4 * D, BF16), (4 * D, BF16)])


def _qkvg_bwd(name, qkvg, tabs, cqn, ckn, dA, dB, dC, dD, dg):
    T = qkvg.shape[0]
    cos1, ss1, cosx, sax, sbx = tabs

    def body(cq_r, ck_r, c1, s1, cx, ax, bx, gq, gk,
             dqa, dka, dva, dqb0, dqb1, dqb2, dkb0, dkb1, dkb2, dvb0, dvb1, dvb2,
             dqc, dkc, dvc, dqd, dkd, dvd, dg_r, out, dgq, dgk):
        c1v, s1v = c1[...], s1[...]

        def put(col, val):
            out[:, col * HD:(col + 1) * HD] = val.astype(BF16)

        def pair(ref, kv):
            return ref[:, 2 * kv * HD:(2 * kv + 1) * HD] + ref[:, (2 * kv + 1) * HD:(2 * kv + 2) * HD]

        for h in range(4):
            sl = slice(h * HD, (h + 1) * HD)
            put(A_Q + h, _rope1_t(dqa[:, sl], c1v, s1v))
            put(B_Q + h, _rope1_t(dqb0[:, sl] + dqb1[:, sl] + dqb2[:, sl], c1v, s1v))
            put(B_K + h, _rope1_t(dkb0[:, sl] + dkb1[:, sl] + dkb2[:, sl], c1v, s1v))
            put(B_V + h, dvb0[:, sl] + dvb1[:, sl] + dvb2[:, sl])
            put(D_Q + h, dqd[:, sl])
            put(D_K + h, dkd[:, sl])
            put(D_V + h, dvd[:, sl])
        for kv in range(2):
            put(A_K + kv, _rope1_t(pair(dka, kv), c1v, s1v))
            put(A_V + kv, pair(dva, kv))
            put(C_V + kv, pair(dvc, kv))
        cxv, axv, bxv = cx[...], ax[...], bx[...]
        for src, dref, col, nh, gref, dgref in ((cq_r, dqc, C_Q, 4, gq, dgq), (ck_r, dkc, C_K, 2, gk, dgk)):
            gsum = None
            for h in range(nh):
                sl = slice(h * HD, (h + 1) * HD)
                xf = src[:, sl].astype(F32)
                r = lax.rsqrt(jnp.mean(xf * xf, axis=1, keepdims=True) + EPS)
                u = xf * r
                dy = _ropex_t(dref[:, sl] if nh == 4 else pair(dref, h), cxv, axv, bxv)
                du = dy * gref[...]
                put(col + h, r * (du - u * jnp.mean(du * u, axis=1, keepdims=True)))
                part = jnp.sum(dy * u, axis=0, keepdims=True)
                gsum = part if gsum is None else gsum + part
            _acc(dgref, gsum)
        for n, col in enumerate((A_G, B_G, C_G, D_G)):
            out[:, col * HD:(col + 4) * HD] = dg_r[:, n * 512:(n + 1) * 512]

    ins = [(qkvg, 512, 7), (qkvg, 256, 16), (cos1, HD, 0), (ss1, HD, 0), (cosx, HD, 0), (sax, HD, 0), (sbx, HD, 0),
           (cqn, None), (ckn, None)]
    ins += [(a, 512, 0) for a in dA]
    ins += [(a, 512, 0) for a in dB[0]] + [(a, 512, 0) for a in dB[1]] + [(a, 512, 0) for a in dB[2]]
    ins += [(a, 512, 0) for a in dC] + [(a, 512, 0) for a in dD] + [(dg, 2048, 0)]
    return _row_call(name, body, T, min(T, 256), ins, [(IN_HEADS * HD, BF16)], [(1, HD), (1, HD)])


def _ada_fwd(name, c_pad, w_ada):
    L, D, n = w_ada.shape
    P = c_pad.shape[0]

    def body(c_ref, w_ref, o_ref):
        cc = c_ref[...]
        cond = cc / (1.0 + jnp.exp(-cc))
        o_ref[...] = jnp.dot(cond.astype(BF16), w_ref[...].astype(BF16), preferred_element_type=F32)

    return _pc(body, name=name, grid=(L,), out_shape=_sds((L, P, n), F32),
               in_specs=[pl.BlockSpec((P, D), lambda l: (0, 0)), pl.BlockSpec((None, D, n), lambda l: (l, 0, 0))],
               out_specs=pl.BlockSpec((None, P, n), lambda l: (l, 0, 0)),
               compiler_params=_cparams(("parallel",)))(c_pad, w_ada)


def _ada_grad(name, c_pad, d_ada):
    L, P, n = d_ada.shape
    D = c_pad.shape[1]

    def body(c_ref, d_ref, o_ref):
        cc = c_ref[...]
        cond = cc / (1.0 + jnp.exp(-cc))
        o_ref[...] = lax.dot_general(cond.astype(BF16), d_ref[...].astype(BF16), _TN, preferred_element_type=F32)

    return _pc(body, name=name, grid=(L,), out_shape=_sds((L, D, n), F32),
               in_specs=[pl.BlockSpec((P, D), lambda l: (0, 0)), pl.BlockSpec((None, P, n), lambda l: (l, 0, 0))],
               out_specs=pl.BlockSpec((None, D, n), lambda l: (l, 0, 0)),
               compiler_params=_cparams(("parallel",)))(c_pad, d_ada)


def _adam_rows(rows, cols):
    cap = max(8, (2 ** 20) // (4 * cols))
    if rows <= cap:
        return rows
    tr = 8
    while tr * 2 <= cap and rows % (tr * 2) == 0:
        tr *= 2
    return tr


def _adam_update(g_ref, slots, w_ref, m_ref, v_ref, go_ref, d_ref, mo_ref, vo_ref):
    if slots:
        gg = g_ref[0].astype(F32)
        for k in range(1, slots):
            gg = gg + g_ref[k].astype(F32)
    else:
        gg = g_ref[...]
    c1 = 1.0 / (1.0 - ADAM_B1 ** ADAM_STEP)
    c2 = 1.0 / (1.0 - ADAM_B2 ** ADAM_STEP)
    mn = ADAM_B1 * m_ref[...] + (1.0 - ADAM_B1) * gg
    vn = ADAM_B2 * v_ref[...] + (1.0 - ADAM_B2) * (gg * gg)
    go_ref[...] = gg
    mo_ref[...] = mn
    vo_ref[...] = vn
    d_ref[...] = -ADAM_LR * ((mn * c1) / (jnp.sqrt(vn * c2) + ADAM_EPS) + ADAM_WD * w_ref[...])


def _adamw(name, w, m, v, g, slots):
    rows, cols = w.shape
    tr = _adam_rows(rows, cols)

    def body(w_ref, m_ref, v_ref, g_ref, *outs):
        _adam_update(g_ref, slots, w_ref, m_ref, v_ref, *outs)

    blk = pl.BlockSpec((tr, cols), lambda i: (i, 0))
    gspec = pl.BlockSpec((slots, tr, cols), lambda i: (0, i, 0)) if slots else blk
    return _pc(body, name=name, grid=(rows // tr,), out_shape=[_sds((rows, cols), F32)] * 4,
               in_specs=[blk, blk, blk, gspec], out_specs=[blk] * 4,
               compiler_params=_cparams(("parallel",)))(w, m, v, g)


def _adamw_layers(name, w, m, v, slot_list):
    nl = len(slot_list)
    slots, rows, cols = slot_list[0].shape
    tr = _adam_rows(rows, cols)
    nblk = rows // tr

    def body(w_ref, m_ref, v_ref, *rest):
        layer = pl.program_id(0)
        for ll in range(nl):
            @pl.when(layer == ll)
            def _(g_ref=rest[ll]):
                _adam_update(g_ref, slots, w_ref, m_ref, v_ref, *rest[nl:])

    def gspec(ll):
        return pl.BlockSpec((slots, tr, cols),
                            lambda l, i: (0, jnp.where(l == ll, i, jnp.where(l > ll, nblk - 1, 0)), 0))

    blk = pl.BlockSpec((tr, cols), lambda l, i: (l * nblk + i, 0))
    return _pc(body, name=name, grid=(nl, nblk), out_shape=[_sds((nl * rows, cols), F32)] * 4,
               in_specs=[blk, blk, blk] + [gspec(ll) for ll in range(nl)], out_specs=[blk] * 4,
               compiler_params=_cparams(("parallel", "parallel")))(w, m, v, *slot_list)


def _rope_tables(T):
    pos = np.arange(T)
    lane = np.arange(HD)
    inv = ROPE_THETA ** (-np.arange(0, HD, 2, dtype=np.float32) / HD)
    ang = pos.astype(np.float32)[:, None] * inv[None, :]
    ang = np.concatenate([ang, ang], axis=-1)
    cos1 = np.cos(ang)
    ss1 = np.sin(ang) * np.where(lane < HD // 2, -1.0, 1.0)[None, :]
    half = HD // 2
    invh = ROPE_THETA ** (-np.arange(0, half, 2, dtype=np.float32) / half)

    def tab(p):
        a = p.astype(np.float32)[:, None] * invh[None, :]
        return np.concatenate([a, a], axis=-1)

    angx = np.concatenate([tab(pos // GRID_W), tab(pos % GRID_W)], axis=-1)
    cosx, sinx = np.cos(angx), np.sin(angx)
    first = (lane % half) < half // 2
    sax = np.where(first[None, :], -sinx, 0.0)
    sbx = np.where(first[None, :], 0.0, sinx)
    return tuple(jnp.asarray(a, F32) for a in (cos1, ss1, cosx, sax, sbx))


def _na_selectors(att, transposed):
    own_r = np.arange(att.t // GRID_W)[:, None]
    span_r = np.arange(att.span // GRID_W)[None, :] - att.pad // GRID_W
    dr = own_r - span_r if transposed else span_r - own_r
    row_sel = (np.clip(dr, -(NA_ROWS - 1), NA_ROWS - 1)[..., None] + NA_ROWS - 1 == np.arange(2 * NA_ROWS - 1))
    col = np.arange(GRID_W)
    dc = col[:, None] - col[None, :] if transposed else col[None, :] - col[:, None]
    col_sel = (np.clip(dc, -(NA_COLS - 1), NA_COLS - 1)[..., None] + NA_COLS - 1 == np.arange(2 * NA_COLS - 1))
    return jnp.asarray(row_sel.astype(np.float32)), jnp.asarray(col_sel.astype(np.float32))


def _na_bias_table(att, rel_bias, transposed):
    row_sel, col_sel = _na_selectors(att, transposed)
    tab = jnp.einsum("xyi,hij,cdj->hxcyd", row_sel, rel_bias * LOG2E, col_sel, precision=lax.Precision.HIGHEST)
    return tab.reshape(rel_bias.shape[0], att.t, att.span)


def _na_bias_grad(att, dtab):
    row_sel, col_sel = _na_selectors(att, False)
    d5 = dtab.reshape(dtab.shape[0], att.t // GRID_W, GRID_W, att.span // GRID_W, GRID_W)
    return jnp.einsum("xyi,hxcyd,cdj->hij", row_sel, d5, col_sel, precision=lax.Precision.HIGHEST)


def _att_specs(T):
    specs = {
        "A": _Att(T, 1, 4, 2, 0, 0, A_V, 0, 0, 0, "band", reach=A_REACH, pad=128),
        "C": _Att(T, 1, 4, 2, 0, 0, C_V, 0, 0, 0, "dense"),
        "D": _Att(T, 1, 4, 1, 0, D_K, D_V, 0, 0, 0, "na", pad=256, rows=T // GRID_W),
    }
    for p, (window, dil) in enumerate(B_PATTERNS):
        specs["B%d" % p] = _Att(T // dil, dil, 4, 1, 0, 0, 0, 4, 4, 4, "band", reach=(window // 2) // dil, pad=64)
    return specs


def kernel(x, c, norm_g, w_ada, b_ada, w_in, a_sink, c_q_norm, c_k_norm, d_rel_bias, w_gate_merge, w_branch, w_out, final_g, loss_target, m_norm_g, m_w_ada, m_b_ada, m_w_in, m_a_sink, m_c_q_norm, m_c_k_norm, m_d_rel_bias, m_w_gate_merge, m_w_branch, m_w_out, m_final_g, v_norm_g, v_w_ada, v_b_ada, v_w_in, v_a_sink, v_c_q_norm, v_c_k_norm, v_d_rel_bias, v_w_gate_merge, v_w_branch, v_w_out, v_final_g):
    T, D = x.shape[1], x.shape[2]
    NL = norm_g.shape[0]
    x0 = x.reshape(T, D)
    tgt = loss_target.reshape(T, D)
    me = 4 * lax.axis_index("x") + 2 * lax.axis_index("y") + lax.axis_index("c")
    att = _att_specs(T)
    tabs = _rope_tables(T)
    n_ada = w_ada.shape[2]

    (c_all,) = _exchange("gather_c", [c], gather=True)
    c_pad = jnp.pad(c_all.reshape(N_DEV, D), ((0, HD - N_DEV), (0, 0)))
    ada_part = _ada_fwd("ada_fwd", c_pad, w_ada)[:, :N_DEV]
    (ada_all,) = _exchange("gather_ada", [ada_part], gather=True)
    ada_mine = lax.dynamic_index_in_dim(ada_all, me, axis=2, keepdims=False)
    ada = jnp.transpose(ada_mine, (1, 0, 2)).reshape(NL, N_DEV * n_ada) + b_ada
    shift, scale, gate = ada[:, :D], ada[:, D:2 * D], ada[:, 2 * D:]

    shards = [[w[l].astype(BF16) for w in (w_in, w_gate_merge, w_branch, w_out)] for l in range(NL)]
    weights = [_exchange("gather_w0", shards[0], gather=True)] + [None] * (NL - 1)

    def views(a_, arrs):
        return [a.reshape(a_.L, -1) for a in arrs]

    saved = []
    xl = x0
    for l in range(NL):
        Wi, Wg, Wb, Wo = weights[l]
        Wo = Wo.reshape(1, D, D)
        sc_l, sh_l, gt_l = scale[l][None], shift[l][None], gate[l][None]
        ng_l = norm_g[l][None]
        cqn, ckn = c_q_norm[l][None], c_k_norm[l][None]
        sink3 = jnp.broadcast_to(a_sink[l][:, None, None], (4, 1, HD))
        btab = _na_bias_table(att["D"], d_rel_bias[l], False)

        h = _prenorm_fwd("prenorm_fwd", xl, ng_l, sc_l, sh_l)
        qkvg = _mm_nn("mm_in", h, Wi, BF16)
        mgl = _mm_nn("mm_gm", h, Wg, BF16)
        qa, ka, qb, kb, vb, qc, kc, qd = _qk_prep("qk_prep", qkvg, tabs, cqn, ckn)
        oa, la = _attn_fwd("attn_a_fwd", att["A"], qa, ka, qkvg, sink=sink3)
        ob, lb = [], []
        for p in range(len(B_PATTERNS)):
            a_ = att["B%d" % p]
            o_, l_ = _attn_fwd("attn_b%d_fwd" % p, a_, *views(a_, (qb, kb, vb)))
            ob.append(o_.reshape(T, 512))
            lb.append(l_.reshape(T, 512))
        if l + 1 < NL:
            oc, lc, *nxt = _attn_fwd("attn_c_fwd_gather", att["C"], qc, kc, qkvg, rider=_Rider(shards[l + 1], True))
            weights[l + 1] = nxt
        else:
            oc, lc = _attn_fwd("attn_c_fwd", att["C"], qc, kc, qkvg)
        od, ld = _attn_fwd("attn_d_fwd", att["D"], qd, qkvg, qkvg, bias=btab)
        br = _gate_fwd("gate_fwd", qkvg, oa, ob, lb, oc, od)
        proj = _branch_nn("mm_branch", br, Wb, BF16)
        merged = _merge_fwd("merge_fwd", mgl, proj, D)
        out = _mm_nn("mm_out", merged, Wo, F32)
        xn = _resid_fwd("resid_fwd", xl, out, gt_l)
        saved.append(dict(x=xl, h=h, qkvg=qkvg, mgl=mgl, qa=qa, ka=ka, qb=qb, kb=kb, vb=vb, qc=qc, kc=kc, qd=qd, oa=oa,
                          la=la, ob=ob, lb=lb, oc=oc, lc=lc, od=od, ld=ld, br=br, proj=proj, merged=merged, out=out,
                          btab=btab))
        xl = xn

    dx, loss_part, d_final_g = _final("final_loss", xl, tgt, final_g[None])

    grads, slots = [None] * NL, [None] * NL
    d_norm_g, d_ada, d_sink, d_cq, d_ck, d_bias = [None] * NL, [None] * NL, [None] * NL, [None] * NL, [None] * NL, [None] * NL
    for l in reversed(range(NL)):
        S = saved[l]
        Wi, Wg, Wb, Wo = weights[l]
        Wo = Wo.reshape(1, D, D)
        sc_l, gt_l, ng_l = scale[l][None], gate[l][None], norm_g[l][None]
        cqn, ckn = c_q_norm[l][None], c_k_norm[l][None]
        sink_row = jnp.repeat(a_sink[l], HD)[None]
        qkvg = S["qkvg"]

        dout, dgate = _resid_bwd("resid_bwd", dx, S["out"], gt_l)
        g_out = _mm_tn("mm_out_dw", S["merged"], dout, 1, BF16).reshape(N_DEV, D // N_DEV, D)
        dm = _mm_nt("mm_out_dx", dout, Wo, F32)
        dproj, dmgl = _merge_bwd("merge_bwd", dm, S["mgl"], S["proj"], D)
        g_br = _branch_tn("mm_branch_dw", S["br"], dproj, D // N_DEV, BF16)
        dbr = _branch_nt("mm_branch_dx", dproj, Wb, F32)
        (doa, dla, dob0, dob1, dob2, dlb0, dlb1, dlb2, doc, dlc, dod, dld, dg, dsk) = _gate_bwd(
            "gate_bwd", dbr, qkvg, S["oa"], S["la"], sink_row, S["ob"], S["lb"], S["oc"], S["od"])
        d_sink[l] = dsk.reshape(4, HD)[:, 0]

        dqa, dka, dva, _ = _attn_bwd("attn_a", att["A"], S["qa"], S["ka"], qkvg, doa, S["la"], dla)
        dqb, dkb, dvb = [], [], []
        for p, (dob, dlb) in enumerate(((dob0, dlb0), (dob1, dlb1), (dob2, dlb2))):
            a_ = att["B%d" % p]
            dq_, dk_, dv_, _ = _attn_bwd("attn_b%d" % p, a_, *views(a_, (S["qb"], S["kb"], S["vb"], dob, S["lb"][p], dlb)))
            dqb.append(dq_.reshape(T, 512))
            dkb.append(dk_.reshape(T, 512))
            dvb.append(dv_.reshape(T, 512))
        if l + 1 < NL:
            dqc, dkc, dvc, slots[l + 1] = _attn_bwd("attn_c_scatter", att["C"], S["qc"], S["kc"], qkvg, doc, S["lc"], dlc,
                                                    rider=_Rider(grads[l + 1], False))
        else:
            dqc, dkc, dvc, _ = _attn_bwd("attn_c", att["C"], S["qc"], S["kc"], qkvg, doc, S["lc"], dlc)
        dqd, dkd, dvd, dtab = _attn_bwd("attn_d", att["D"], S["qd"], qkvg, qkvg, dod, S["ld"], dld, bias=S["btab"],
                                        bias_t=_na_bias_table(att["D"], d_rel_bias[l], True))
        d_bias[l] = _na_bias_grad(att["D"], dtab)

        dqkvg, dcq, dck = _qkvg_bwd("qkvg_bwd", qkvg, tabs, cqn, ckn, (dqa, dka, dva), (dqb, dkb, dvb),
                                    (dqc, dkc, dvc), (dqd, dkd, dvd), dg)
        d_cq[l], d_ck[l] = dcq[0], dck[0]
        g_in = _mm_tn("mm_in_dw", S["h"], dqkvg, N_DEV, BF16)
        g_gm = _mm_tn("mm_gm_dw", S["h"], dmgl, N_DEV, BF16)
        grads[l] = [g_in, g_gm, g_br, g_out]
        dh = _mm_nt("mm_in_dx", dqkvg, Wi, F32)
        dh = _mm_nt("mm_gm_dx", dmgl, Wg, F32, add=dh)
        dx, dshift, dscale, dng = _prenorm_bwd("prenorm_bwd", S["x"], dh, dx, ng_l, sc_l)
        d_norm_g[l] = dng[0]
        d_ada[l] = jnp.concatenate([dshift[0], dscale[0], dgate[0]])

    grad_x = dx.reshape(1, T, D)

    slots[0] = _exchange("scatter_g0", grads[0], gather=False)

    def big(name, w, m, v, k):
        cols = w.shape[-1]
        res = _adamw_layers(name, w.reshape(-1, cols), m.reshape(-1, cols), v.reshape(-1, cols),
                            [slots[l][k].reshape(N_DEV, -1, cols) for l in range(NL)])
        return [r.reshape(w.shape) for r in res]

    r_in = big("adamw_in", w_in, m_w_in, v_w_in, 0)
    r_gm = big("adamw_gm", w_gate_merge, m_w_gate_merge, v_w_gate_merge, 1)
    r_br = big("adamw_branch", w_branch, m_w_branch, v_w_branch, 2)
    r_out = big("adamw_out", w_out, m_w_out, v_w_out, 3)

    small_w = [norm_g, b_ada, a_sink, c_q_norm, c_k_norm, d_rel_bias, final_g]
    small_m = [m_norm_g, m_b_ada, m_a_sink, m_c_q_norm, m_c_k_norm, m_d_rel_bias, m_final_g]
    small_v = [v_norm_g, v_b_ada, v_a_sink, v_c_q_norm, v_c_k_norm, v_d_rel_bias, v_final_g]
    small_g = [jnp.stack(d_norm_g), jnp.stack(d_ada), jnp.stack(d_sink), jnp.stack(d_cq), jnp.stack(d_ck),
               jnp.stack(d_bias), d_final_g[0]]
    sizes = [int(np.prod(w.shape)) for w in small_w]
    total = sum(sizes) + HD
    rows = -(-total // (8 * HD)) * 8

    def pack(parts, extra):
        flat = jnp.concatenate([p.reshape(-1).astype(F32) for p in parts] + [extra])
        return jnp.pad(flat, (0, rows * HD - flat.shape[0])).reshape(rows, HD)

    zeros = jnp.zeros((HD,), F32)
    (g_slots,) = _exchange("gather_small", [pack(small_g, loss_part[0])], gather=True)
    gs, ds, ms, vs = _adamw("adamw_small", pack(small_w, zeros), pack(small_m, zeros), pack(small_v, zeros), g_slots, N_DEV)

    def unpack(flat2d):
        flat = flat2d.reshape(-1)
        res, o = [], 0
        for w, n in zip(small_w, sizes):
            res.append(flat[o:o + n].reshape(w.shape))
            o += n
        return res, flat[o]

    sg, loss = unpack(gs)
    sd, _ = unpack(ds)
    sm, _ = unpack(ms)
    sv, _ = unpack(vs)

    o_b = sizes[0]
    d_ada_all = g_slots.reshape(N_DEV, -1)[:, o_b:o_b + sizes[1]].reshape(N_DEV, NL, N_DEV, n_ada)
    d_ada_mine = jnp.transpose(lax.dynamic_index_in_dim(d_ada_all, me, axis=2, keepdims=False), (1, 0, 2))
    g_ada = _ada_grad("ada_grad", c_pad, jnp.pad(d_ada_mine, ((0, 0), (0, HD - N_DEV), (0, 0))))
    r_ada = _adamw("adamw_ada", w_ada.reshape(-1, n_ada), m_w_ada.reshape(-1, n_ada), v_w_ada.reshape(-1, n_ada),
                   g_ada.reshape(-1, n_ada), 0)
    r_ada = [r.reshape(w_ada.shape) for r in r_ada]

    def kind(k):
        sm_ = (sg, sd, sm, sv)[k]
        return [sm_[0], r_ada[k], sm_[1], r_in[k], sm_[2], sm_[3], sm_[4], sm_[5], r_gm[k], r_br[k], r_out[k], sm_[6]]

    return (loss, grad_x, *kind(0), *kind(1), *kind(2), *kind(3))
```

```python
import jax
import jax.numpy as jnp
import numpy as np
from jax import lax
from jax.experimental import pallas as pl
from jax.experimental.pallas import tpu as pltpu

F32 = jnp.float32
BF16 = jnp.bfloat16

N_DEV = 8
HD = 128
GRID_W = 64
EPS = 1e-6
NEG_INF = -1e30
ROPE_THETA = 10000.0
A_REACH = 128
B_PATTERNS = ((128, 1), (512, 4), (2048, 16))
NA_ROWS = 8
NA_COLS = 16
SCALE = HD ** -0.5
LOG2E = 1.4426950408889634
LN2 = 0.6931471805599453
SCALE2 = SCALE * LOG2E
A_Q, A_K, A_V, A_G = 0, 4, 6, 8
B_Q, B_K, B_V, B_G = 12, 16, 20, 24
C_Q, C_K, C_V, C_G = 28, 32, 34, 36
D_Q, D_K, D_V, D_G = 40, 44, 48, 52
IN_HEADS = 56

ADAM_LR, ADAM_B1, ADAM_B2, ADAM_EPS, ADAM_WD, ADAM_STEP = 0.001, 0.9, 0.999, 1e-08, 0.01, 10

V7X_VMEM_LIMIT = 56 * 2 ** 20
ATT_TILE = 512


def _pc(body, **kw):
    return pl.pallas_call(body, **kw)


def _cparams(sem=None):
    if sem is None:
        return pltpu.CompilerParams(vmem_limit_bytes=V7X_VMEM_LIMIT)
    return pltpu.CompilerParams(dimension_semantics=sem, vmem_limit_bytes=V7X_VMEM_LIMIT)


def _tile(n, cap):
    if n <= cap:
        return n
    t = (cap // 128) * 128
    while n % t:
        t -= 128
    return t


def _sds(shape, dtype):
    return jax.ShapeDtypeStruct(tuple(shape), dtype)


def _exchange_out(ins, gather):
    return [_sds((N_DEV,) + (a.shape if gather else a.shape[1:]), a.dtype) for a in ins]


def _exchange_sems(n):
    return [pltpu.SemaphoreType.DMA((n, N_DEV - 1)), pltpu.SemaphoreType.DMA((n, N_DEV - 1)), pltpu.SemaphoreType.DMA((n,))]


def _exchange_copies(in_refs, out_refs, send_sems, recv_sems, loc_sems, gather):
    x, y, c = lax.axis_index("x"), lax.axis_index("y"), lax.axis_index("c")
    me = 4 * x + 2 * y + c
    copies = []
    for t, (src, dst) in enumerate(zip(in_refs, out_refs)):
        copies.append(pltpu.make_async_copy(src if gather else src.at[me], dst.at[me], loc_sems.at[t]))
    for p in range(1, N_DEV):
        tx = 1 - x if p & 4 else x
        ty = 1 - y if p & 2 else y
        tc = 1 - c if p & 1 else c
        peer = 4 * tx + 2 * ty + tc
        for t, (src, dst) in enumerate(zip(in_refs, out_refs)):
            copies.append(pltpu.make_async_remote_copy(
                src_ref=src if gather else src.at[peer], dst_ref=dst.at[me],
                send_sem=send_sems.at[t, p - 1], recv_sem=recv_sems.at[t, p - 1],
                device_id=(tx, ty, tc), device_id_type=pl.DeviceIdType.MESH))
    return copies


def _exchange(name, ins, gather):
    n = len(ins)

    def body(*refs):
        copies = _exchange_copies(refs[:n], refs[n:2 * n], *refs[2 * n:], gather)
        for cp in copies:
            cp.start()
        for cp in copies:
            cp.wait()

    res = _pc(body, name=name, out_shape=_exchange_out(ins, gather),
              in_specs=[pl.BlockSpec(memory_space=pl.ANY)] * n, out_specs=[pl.BlockSpec(memory_space=pl.ANY)] * n,
              scratch_shapes=_exchange_sems(n))(*ins)
    return list(res)


class _Rider:
    def __init__(self, ins, gather):
        self.ins, self.gather, self.n = list(ins), gather, len(ins)

    def specs(self):
        any_ = [pl.BlockSpec(memory_space=pl.ANY)] * self.n
        return any_, _exchange_out(self.ins, self.gather), any_, _exchange_sems(self.n)

    def start(self, in_refs, out_refs, sems, grid):
        first = pl.program_id(0) == 0
        for ax in range(1, len(grid)):
            first = first & (pl.program_id(ax) == 0)

        @pl.when(first)
        def _():
            for cp in _exchange_copies(in_refs, out_refs, *sems, self.gather):
                cp.start()

    def finish(self, in_refs, out_refs, sems, grid):
        last = pl.program_id(0) == grid[0] - 1
        for ax in range(1, len(grid)):
            last = last & (pl.program_id(ax) == grid[ax] - 1)

        @pl.when(last)
        def _():
            for cp in _exchange_copies(in_refs, out_refs, *sems, self.gather):
                cp.wait()


_NN = (((1,), (0,)), ((), ()))
_NT = (((1,), (1,)), ((), ()))
_TN = (((0,), (0,)), ((), ()))


def _dot(a, b, dims):
    return lax.dot_general(a, b, dims, preferred_element_type=F32)


def _mm(name, a, w, out_sds, grid, a_spec, w_spec, o_spec, prod, red_axis=None, add=None, acc_shape=None,
        store=None, rider=None):
    nred = grid[red_axis] if red_axis is not None else 1
    if acc_shape is None:
        acc_shape = tuple(d for d in o_spec.block_shape if d is not None)
    nin = 3 if add is not None else 2
    nr = rider.n if rider is not None else 0

    def body(*refs):
        a_ref, w_ref = refs[:2]
        add_ref = refs[2] if add is not None else None
        o_ref = refs[nin + nr]
        scr = refs[nin + nr + 1 + nr:]
        comm = (refs[nin:nin + nr], refs[nin + nr + 1:nin + nr + 1 + nr], scr[-3:])
        if rider is not None:
            rider.start(*comm, grid)
        part = prod(a_ref, w_ref)

        def finish(acc):
            if add_ref is not None:
                acc = acc + add_ref[...].astype(F32)
            if store is not None:
                store(o_ref, acc)
            else:
                o_ref[...] = acc.astype(o_ref.dtype)

        if nred == 1:
            finish(part)
        else:
            acc_ref = scr[0]
            k = pl.program_id(red_axis)

            @pl.when(k == 0)
            def _():
                acc_ref[...] = part

            @pl.when(k > 0)
            def _():
                acc_ref[...] += part

            @pl.when(k == nred - 1)
            def _():
                finish(acc_ref[...])
        if rider is not None:
            rider.finish(*comm, grid)

    r_in, r_out, r_outspec, r_sems = rider.specs() if rider is not None else ([], [], [], [])
    if rider is not None:
        sem = ("arbitrary",) * len(grid)
    else:
        sem = tuple("arbitrary" if i == red_axis else "parallel" for i in range(len(grid)))
    in_specs = [a_spec, w_spec] + ([o_spec] if add is not None else []) + r_in
    args = (a, w) + ((add,) if add is not None else ()) + tuple(rider.ins if rider is not None else ())
    res = _pc(body, name=name, out_shape=[out_sds] + r_out, grid=grid, in_specs=in_specs,
              out_specs=[o_spec] + r_outspec,
              scratch_shapes=([pltpu.VMEM(acc_shape, F32)] if nred > 1 else []) + r_sems,
              compiler_params=_cparams(sem))(*args)
    return res[0] if rider is None else res


def _mm_nn(name, a, w, out_dtype):
    T, K = a.shape
    G, _, n = w.shape
    tm, tn = min(T, 1024), _tile(n, 1024)
    nj = n // tn
    return _mm(name, a, w, _sds((T, G * n), out_dtype), (T // tm, G, nj),
               pl.BlockSpec((tm, K), lambda i, g, j: (i, 0)),
               pl.BlockSpec((None, K, tn), lambda i, g, j: (g, 0, j)),
               pl.BlockSpec((tm, tn), lambda i, g, j: (i, g * nj + j)),
               lambda a_ref, w_ref: _dot(a_ref[...], w_ref[...], _NN))


def _mm_nt(name, dy, w, out_dtype, add=None, rider=None):
    T = dy.shape[0]
    G, K, n = w.shape
    tm, tk = min(T, 1024), _tile(K, 1024)
    cps = 2 if G % 2 == 0 and n <= 1024 else 1

    def prod(a_ref, w_ref):
        acc = _dot(a_ref[:, 0:n], w_ref[0], _NT)
        for c in range(1, cps):
            acc = acc + _dot(a_ref[:, c * n:(c + 1) * n], w_ref[c], _NT)
        return acc

    return _mm(name, dy, w, _sds((T, K), out_dtype), (T // tm, K // tk, G // cps),
               pl.BlockSpec((tm, cps * n), lambda i, k, r: (i, r)),
               pl.BlockSpec((cps, tk, n), lambda i, k, r: (r, k, 0)),
               pl.BlockSpec((tm, tk), lambda i, k, r: (i, k)), prod, red_axis=2, add=add, rider=rider)


def _mm_tn(name, a, dy, G, out_dtype):
    T, K = a.shape
    n = dy.shape[1] // G
    tt, tk, tn = min(T, 2048), _tile(K, 1024), _tile(n, 1024)
    nj = n // tn
    return _mm(name, a, dy, _sds((G, K, n), out_dtype), (K // tk, G, nj, T // tt),
               pl.BlockSpec((tt, tk), lambda k, g, j, t: (t, k)),
               pl.BlockSpec((tt, tn), lambda k, g, j, t: (t, g * nj + j)),
               pl.BlockSpec((None, tk, tn), lambda k, g, j, t: (g, k, j)),
               lambda a_ref, w_ref: _dot(a_ref[...], w_ref[...], _TN), red_axis=3)


def _branch_weight(w_ref):
    return jnp.concatenate([w_ref[d] for d in range(N_DEV)], axis=1)


def _branch_nn(name, br, wb, out_dtype):
    T = br.shape[0]
    _, NB, W, n = wb.shape
    tm = min(T, 1024)
    return _mm(name, br, wb, _sds((T, NB * N_DEV * n), out_dtype), (T // tm, NB),
               pl.BlockSpec((tm, W), lambda i, b: (i, b)),
               pl.BlockSpec((N_DEV, None, W, n), lambda i, b: (0, b, 0, 0)),
               pl.BlockSpec((tm, N_DEV * n), lambda i, b: (i, b)),
               lambda a_ref, w_ref: _dot(a_ref[...], _branch_weight(w_ref), _NN))


def _branch_nt(name, dproj, wb, out_dtype):
    T = dproj.shape[0]
    _, NB, W, n = wb.shape
    tm = min(T, 1024)
    return _mm(name, dproj, wb, _sds((T, NB * W), out_dtype), (T // tm, NB),
               pl.BlockSpec((tm, N_DEV * n), lambda i, b: (i, b)),
               pl.BlockSpec((N_DEV, None, W, n), lambda i, b: (0, b, 0, 0)),
               pl.BlockSpec((tm, W), lambda i, b: (i, b)),
               lambda a_ref, w_ref: _dot(a_ref[...], _branch_weight(w_ref), _NT))


def _branch_tn(name, br, dproj, n, out_dtype):
    T = br.shape[0]
    NB = 4
    W = br.shape[1] // NB
    tt = min(T, 2048)

    def store(o_ref, acc):
        for d in range(N_DEV):
            o_ref[d] = acc[:, d * n:(d + 1) * n].astype(o_ref.dtype)

    return _mm(name, br, dproj, _sds((N_DEV, NB, W, n), out_dtype), (NB, T // tt),
               pl.BlockSpec((tt, W), lambda b, t: (t, b)),
               pl.BlockSpec((tt, N_DEV * n), lambda b, t: (t, b)),
               pl.BlockSpec((N_DEV, None, W, n), lambda b, t: (0, b, 0, 0)),
               lambda a_ref, w_ref: _dot(a_ref[...], w_ref[...], _TN), red_axis=1,
               acc_shape=(W, N_DEV * n), store=store)


class _Att:
    def __init__(self, L, R, H, G, qcol, kcol, vcol, qstr, kstr, vstr, mode, reach=0, pad=0, rows=0):
        self.L, self.R, self.H, self.G = L, R, H, G
        self.qcol, self.kcol, self.vcol = qcol, kcol, vcol
        self.qstr, self.kstr, self.vstr = qstr, kstr, vstr
        self.mode, self.reach, self.rows = mode, reach, rows
        self.t = min(ATT_TILE, L)
        self.nb = L // self.t
        self.pieces = mode != "dense" and self.nb > 1
        self.pad = pad if self.pieces else 0
        self.span = self.t + 2 * self.pad
        self.ppb = self.t // pad if self.pieces else 0

    def qc(self, r, h):
        return r * self.qstr + self.qcol + h

    def kc(self, r, h):
        return r * self.kstr + self.kcol + h // self.G

    def vc(self, r, h):
        return r * self.vstr + self.vcol + h // self.G

    def oc(self, r, h):
        return r * self.H + h

    def mask(self, qpos, kpos, spos):
        if self.mode == "band":
            ok = jnp.abs(qpos - kpos) <= self.reach
        else:
            rq, cq = qpos >> 6, qpos & (GRID_W - 1)
            rk, ck = kpos >> 6, kpos & (GRID_W - 1)
            rs = jnp.clip(rq - NA_ROWS // 2, 0, self.rows - NA_ROWS)
            cs = jnp.clip(cq - NA_COLS // 2, 0, GRID_W - NA_COLS)
            ok = (rk >= rs) & (rk < rs + NA_ROWS) & (ck >= cs) & (ck < cs + NA_COLS)
        if self.pieces:
            ok = ok & (spos >= 0) & (spos < self.L)
        return ok

    def span_specs(self, col):
        t, pad, ppb = self.t, self.pad, self.ppb
        cur = pl.BlockSpec((t, HD), lambda r, h, i: (i, col(r, h)))
        if not self.pieces:
            return [cur]
        last = self.L // pad - 1
        prev = pl.BlockSpec((pad, HD), lambda r, h, i: (jnp.maximum(i * ppb - 1, 0), col(r, h)))
        nxt = pl.BlockSpec((pad, HD), lambda r, h, i: (jnp.minimum((i + 1) * ppb, last), col(r, h)))
        return [prev, cur, nxt]

    def positions(self, i):
        own = i * self.t + lax.broadcasted_iota(jnp.int32, (self.t, 1), 0)
        spn = i * self.t - self.pad + lax.broadcasted_iota(jnp.int32, (1, self.span), 1)
        return own, spn


def _cat(refs):
    return refs[0][...] if len(refs) == 1 else jnp.concatenate([r[...] for r in refs], axis=0)


def _with_scores(att, raw, i, own_is_query, bias_ref, itab_ref, rest):
    def edge():
        s = raw if bias_ref is None else raw + bias_ref[...]
        own, spn = att.positions(i)
        qpos, kpos = (own, spn) if own_is_query else (spn, own)
        rest(jnp.where(att.mask(qpos, kpos, spn), s, NEG_INF))

    if itab_ref is None:
        edge()
        return
    inner = (i >= 1) & (i <= att.nb - 2)
    pl.when(inner)(lambda: rest(raw + itab_ref[...]))
    pl.when(jnp.logical_not(inner))(edge)


def _tab_spec(att, tab):
    if tab.shape[0] == 1:
        return pl.BlockSpec((None, att.t, att.span), lambda r, h, i: (0, 0, 0))
    return pl.BlockSpec((None, att.t, att.span), lambda r, h, i: (h, 0, 0))


def _band_fwd(name, att, q, k, v, sink=None, bias=None, itab=None):
    t, ns = att.t, (3 if att.pieces else 1)

    def body(*refs):
        q_ref, k_refs, v_refs = refs[0], refs[1:1 + ns], refs[1 + ns:1 + 2 * ns]
        pos = 1 + 2 * ns
        sink_ref = bias_ref = itab_ref = None
        if sink is not None:
            sink_ref = refs[pos]
            pos += 1
        if bias is not None:
            bias_ref = refs[pos]
            pos += 1
        if itab is not None:
            itab_ref = refs[pos]
            pos += 1
        o_ref, lse_ref = refs[pos:]
        ks, vs = _cat(k_refs), _cat(v_refs)

        def rest(s):
            m = jnp.max(s, axis=1, keepdims=True)
            if sink_ref is not None:
                sk = sink_ref[...][:, :1] * LOG2E
                m = jnp.maximum(m, sk)
            p = jnp.exp2(s - m)
            den = jnp.sum(p, axis=1, keepdims=True)
            if sink_ref is not None:
                den = den + jnp.exp2(sk - m)
            o = jnp.dot(p.astype(BF16), vs, preferred_element_type=F32)
            o_ref[...] = (o / den).astype(o_ref.dtype)
            lse_ref[...] = jnp.broadcast_to((m + jnp.log2(den)) * LN2, (t, HD))

        _with_scores(att, _dot(q_ref[...], ks, _NT), pl.program_id(2), True, bias_ref, itab_ref, rest)

    in_specs = [pl.BlockSpec((t, HD), lambda r, h, i: (i, att.qc(r, h)))] + att.span_specs(att.kc) + att.span_specs(att.vc)
    args = [q] + [k] * ns + [v] * ns
    if sink is not None:
        in_specs.append(pl.BlockSpec((None, 1, HD), lambda r, h, i: (h, 0, 0)))
        args.append(sink)
    for tab in (bias, itab):
        if tab is not None:
            in_specs.append(_tab_spec(att, tab))
            args.append(tab)
    o_spec = pl.BlockSpec((t, HD), lambda r, h, i: (i, att.oc(r, h)))
    cols = att.R * att.H * HD
    return _pc(body, name=name, grid=(att.R, att.H, att.nb),
               out_shape=[_sds((att.L, cols), BF16), _sds((att.L, cols), F32)],
               in_specs=in_specs, out_specs=[o_spec, o_spec],
               compiler_params=_cparams(("parallel", "parallel", "parallel")))(*args)


def _band_dq(name, att, q, k, v, do, lse, delta, bias=None, itab=None):
    t, ns = att.t, (3 if att.pieces else 1)

    def body(*refs):
        q_ref, k_refs, v_refs = refs[0], refs[1:1 + ns], refs[1 + ns:1 + 2 * ns]
        do_ref, lse_ref, dl_ref = refs[1 + 2 * ns:4 + 2 * ns]
        pos = 4 + 2 * ns
        bias_ref = itab_ref = dtab_ref = None
        if bias is not None:
            bias_ref = refs[pos]
            pos += 1
        if itab is not None:
            itab_ref = refs[pos]
            pos += 1
        dq_ref = refs[pos]
        if bias is not None:
            dtab_ref = refs[pos + 1]
        i = pl.program_id(2)
        ks, vs = _cat(k_refs), _cat(v_refs)

        def rest(s):
            p = jnp.exp2(s - lse_ref[...][:, :1] * LOG2E)
            dp = _dot(do_ref[...], vs, _NT)
            ds = p * (dp - dl_ref[...][:, :1])
            if dtab_ref is not None:
                @pl.when(i == 0)
                def _():
                    dtab_ref[...] = ds

                @pl.when(i > 0)
                def _():
                    dtab_ref[...] += ds
            dq_ref[...] = jnp.dot(ds.astype(BF16), ks, preferred_element_type=F32) * SCALE

        _with_scores(att, _dot(q_ref[...], ks, _NT), i, True, bias_ref, itab_ref, rest)

    row = pl.BlockSpec((t, HD), lambda r, h, i: (i, att.oc(r, h)))
    in_specs = [pl.BlockSpec((t, HD), lambda r, h, i: (i, att.qc(r, h)))] + att.span_specs(att.kc) + att.span_specs(att.vc)
    in_specs += [row, row, row]
    args = [q] + [k] * ns + [v] * ns + [do, lse, delta]
    for tab in (bias, itab):
        if tab is not None:
            in_specs.append(_tab_spec(att, tab))
            args.append(tab)
    out_shape = [_sds((att.L, att.R * att.H * HD), F32)]
    out_specs = [row]
    sem = ("parallel", "parallel", "parallel")
    if bias is not None:
        out_shape.append(_sds((att.H, t, att.span), F32))
        out_specs.append(_tab_spec(att, bias))
        sem = ("parallel", "parallel", "arbitrary")
    return _pc(body, name=name, grid=(att.R, att.H, att.nb), out_shape=out_shape, in_specs=in_specs,
               out_specs=out_specs, compiler_params=_cparams(sem))(*args)


def _band_dkv(name, att, q, k, v, do, lse_sp, dl_sp, bias_t=None, itab_t=None):
    t, ns = att.t, (3 if att.pieces else 1)

    def body(*refs):
        k_ref, v_ref = refs[:2]
        q_refs, do_refs = refs[2:2 + ns], refs[2 + ns:2 + 2 * ns]
        lse_ref, dl_ref = refs[2 + 2 * ns:4 + 2 * ns]
        pos = 4 + 2 * ns
        bias_ref = itab_ref = None
        if bias_t is not None:
            bias_ref = refs[pos]
            pos += 1
        if itab_t is not None:
            itab_ref = refs[pos]
            pos += 1
        dk_ref, dv_ref = refs[pos:]
        qs, dos = _cat(q_refs), _cat(do_refs)

        def rest(st):
            pt = jnp.exp2(st - lse_ref[...] * LOG2E)
            dv_ref[...] = jnp.dot(pt.astype(BF16), dos, preferred_element_type=F32)
            dpt = _dot(v_ref[...], dos, _NT)
            dst = pt * (dpt - dl_ref[...])
            dk_ref[...] = jnp.dot(dst.astype(BF16), qs, preferred_element_type=F32) * LN2

        _with_scores(att, _dot(k_ref[...], qs, _NT), pl.program_id(2), False, bias_ref, itab_ref, rest)

    stat = pl.BlockSpec((None, None, 1, att.span), lambda r, h, i: (r * att.H + h, i, 0, 0))
    in_specs = [pl.BlockSpec((t, HD), lambda r, h, i: (i, att.kc(r, h))),
                pl.BlockSpec((t, HD), lambda r, h, i: (i, att.vc(r, h)))]
    in_specs += att.span_specs(att.qc) + att.span_specs(att.oc) + [stat, stat]
    args = [k, v] + [q] * ns + [do] * ns + [lse_sp, dl_sp]
    for tab in (bias_t, itab_t):
        if tab is not None:
            in_specs.append(_tab_spec(att, tab))
            args.append(tab)
    o_spec = pl.BlockSpec((t, HD), lambda r, h, i: (i, att.oc(r, h)))
    cols = att.R * att.H * HD
    return _pc(body, name=name, grid=(att.R, att.H, att.nb), out_shape=[_sds((att.L, cols), F32)] * 2,
               in_specs=in_specs, out_specs=[o_spec, o_spec],
               compiler_params=_cparams(("parallel", "parallel", "parallel")))(*args)


def _dense_fwd(name, att, q, k, v, rider=None):
    t, L, nb = att.t, att.L, att.nb
    grid = (att.H, nb)
    nr = rider.n if rider is not None else 0

    def body(*refs):
        q_ref, k_ref, v_ref = refs[:3]
        o_ref, lse_ref = refs[3 + nr:5 + nr]
        comm = (refs[3:3 + nr], refs[5 + nr:5 + 2 * nr], refs[5 + 2 * nr:])
        if rider is not None:
            rider.start(*comm, grid)
        qq = q_ref[...]

        def step(j, carry):
            m, den, acc = carry
            rows = pl.ds(pl.multiple_of(j * t, t), t)
            s = lax.dot_general(qq, k_ref[rows, :], _NT, preferred_element_type=F32)
            m_new = jnp.maximum(m, jnp.max(s, axis=1, keepdims=True))
            alpha = jnp.exp2(m - m_new)
            p = jnp.exp2(s - m_new)
            den = alpha * den + jnp.sum(p, axis=1, keepdims=True)
            acc = alpha * acc + jnp.dot(p.astype(BF16), v_ref[rows, :], preferred_element_type=F32)
            return m_new, den, acc

        init = (jnp.full((t, 1), NEG_INF, F32), jnp.zeros((t, 1), F32), jnp.zeros((t, HD), F32))
        m, den, acc = lax.fori_loop(0, nb, step, init, unroll=2 if nb % 2 == 0 else 1)
        o_ref[...] = (acc / den).astype(o_ref.dtype)
        lse_ref[...] = jnp.broadcast_to((m + jnp.log2(den)) * LN2, (t, HD))
        if rider is not None:
            rider.finish(*comm, grid)

    o_spec = pl.BlockSpec((t, HD), lambda h, i: (i, att.oc(0, h)))
    r_in, r_out, r_outspec, r_sems = rider.specs() if rider is not None else ([], [], [], [])
    sem = ("arbitrary", "arbitrary") if rider is not None else ("parallel", "parallel")
    return _pc(body, name=name, grid=grid,
               out_shape=[_sds((L, att.H * HD), BF16), _sds((L, att.H * HD), F32)] + r_out,
               in_specs=[pl.BlockSpec((t, HD), lambda h, i: (i, att.qc(0, h))),
                         pl.BlockSpec((L, HD), lambda h, i: (0, att.kc(0, h))),
                         pl.BlockSpec((L, HD), lambda h, i: (0, att.vc(0, h)))] + r_in,
               out_specs=[o_spec, o_spec] + r_outspec, scratch_shapes=r_sems,
               compiler_params=_cparams(sem))(q, k, v, *(rider.ins if rider is not None else []))


def _dense_dq(name, att, q, k, v, do, lse, delta):
    t, L, nb = att.t, att.L, att.nb

    def body(q_ref, k_ref, v_ref, do_ref, lse_ref, dl_ref, dq_ref):
        qq, dd = q_ref[...], do_ref[...]
        lse2 = lse_ref[...][:, :1] * LOG2E
        dl = dl_ref[...][:, :1]

        def step(j, acc):
            rows = pl.ds(pl.multiple_of(j * t, t), t)
            kk = k_ref[rows, :]
            s = lax.dot_general(qq, kk, _NT, preferred_element_type=F32)
            p = jnp.exp2(s - lse2)
            dp = lax.dot_general(dd, v_ref[rows, :], _NT, preferred_element_type=F32)
            ds = p * (dp - dl)
            return acc + jnp.dot(ds.astype(BF16), kk, preferred_element_type=F32)

        acc = lax.fori_loop(0, nb, step, jnp.zeros((t, HD), F32), unroll=2 if nb % 2 == 0 else 1)
        dq_ref[...] = acc * SCALE

    row = pl.BlockSpec((t, HD), lambda h, i: (i, att.oc(0, h)))
    return _pc(body, name=name, grid=(att.H, nb), out_shape=_sds((L, att.H * HD), F32),
               in_specs=[pl.BlockSpec((t, HD), lambda h, i: (i, att.qc(0, h))),
                         pl.BlockSpec((L, HD), lambda h, i: (0, att.kc(0, h))),
                         pl.BlockSpec((L, HD), lambda h, i: (0, att.vc(0, h))), row, row, row],
               out_specs=row, compiler_params=_cparams(("parallel", "parallel")))(q, k, v, do, lse, delta)


def _dense_dkv(name, att, q, k, v, do, lse_rows, dl_rows, rider=None):
    t, L, nb = att.t, att.L, att.nb
    grid = (att.H, nb)
    nr = rider.n if rider is not None else 0

    def body(*refs):
        k_ref, v_ref, q_ref, do_ref, lse_ref, dl_ref = refs[:6]
        dk_ref, dv_ref = refs[6 + nr:8 + nr]
        comm = (refs[6:6 + nr], refs[8 + nr:8 + 2 * nr], refs[8 + 2 * nr:])
        if rider is not None:
            rider.start(*comm, grid)
        kk, vv = k_ref[...], v_ref[...]

        def step(i, carry):
            dk, dv = carry
            rows = pl.ds(pl.multiple_of(i * t, t), t)
            qq, dd = q_ref[rows, :], do_ref[rows, :]
            st = lax.dot_general(kk, qq, _NT, preferred_element_type=F32)
            pt = jnp.exp2(st - lse_ref[i] * LOG2E)
            dv = dv + jnp.dot(pt.astype(BF16), dd, preferred_element_type=F32)
            dpt = lax.dot_general(vv, dd, _NT, preferred_element_type=F32)
            dst = pt * (dpt - dl_ref[i])
            dk = dk + jnp.dot(dst.astype(BF16), qq, preferred_element_type=F32)
            return dk, dv

        zero = jnp.zeros((t, HD), F32)
        dk, dv = lax.fori_loop(0, nb, step, (zero, zero), unroll=2 if nb % 2 == 0 else 1)
        dk_ref[...] = dk * LN2
        dv_ref[...] = dv
        if rider is not None:
            rider.finish(*comm, grid)

    stat = pl.BlockSpec((None, nb, 1, t), lambda h, j: (h, 0, 0, 0))
    o_spec = pl.BlockSpec((t, HD), lambda h, j: (j, att.oc(0, h)))
    r_in, r_out, r_outspec, r_sems = rider.specs() if rider is not None else ([], [], [], [])
    sem = ("arbitrary", "arbitrary") if rider is not None else ("parallel", "parallel")
    return _pc(body, name=name, grid=grid, out_shape=[_sds((L, att.H * HD), F32)] * 2 + r_out,
               in_specs=[pl.BlockSpec((t, HD), lambda h, j: (j, att.kc(0, h))),
                         pl.BlockSpec((t, HD), lambda h, j: (j, att.vc(0, h))),
                         pl.BlockSpec((L, HD), lambda h, j: (0, att.qc(0, h))),
                         pl.BlockSpec((L, HD), lambda h, j: (0, att.oc(0, h))), stat, stat] + r_in,
               out_specs=[o_spec, o_spec] + r_outspec, scratch_shapes=r_sems,
               compiler_params=_cparams(sem))(k, v, q, do, lse_rows, dl_rows, *(rider.ins if rider is not None else []))


def _stat_rows(att, col):
    RH = att.R * att.H
    rows = col.reshape(att.L, RH, HD)[:, :, 0].T
    if not att.pieces:
        return rows.reshape(RH, att.nb, 1, att.t)
    padded = jnp.pad(rows, ((0, 0), (att.pad, att.pad)))
    return jnp.stack([padded[:, j * att.t:j * att.t + att.span] for j in range(att.nb)], axis=1)[:, :, None, :]


def _interior_table(att, transposed, bias=None):
    if not (att.pieces and att.nb > 2):
        return None
    own = np.arange(att.t)[:, None]
    spn = np.arange(att.span)[None, :] - att.pad
    qpos, kpos = (spn, own) if transposed else (own, spn)
    if att.mode == "band":
        ok = np.abs(qpos - kpos) <= att.reach
    else:
        rq, cq = qpos // GRID_W, qpos % GRID_W
        rk, ck = kpos // GRID_W, kpos % GRID_W
        rs = rq - NA_ROWS // 2
        cs = np.clip(cq - NA_COLS // 2, 0, GRID_W - NA_COLS)
        ok = (rk >= rs) & (rk < rs + NA_ROWS) & (ck >= cs) & (ck < cs + NA_COLS)
    if bias is None:
        return jnp.asarray(np.where(ok, 0.0, NEG_INF)[None], F32)
    return jnp.where(jnp.asarray(ok)[None], bias, NEG_INF)


def _attn_fwd(name, att, q, k, v, sink=None, bias=None, rider=None):
    if att.mode == "dense":
        return _dense_fwd(name, att, q, k, v, rider=rider)
    return _band_fwd(name, att, q, k, v, sink=sink, bias=bias, itab=_interior_table(att, False, bias))


def _attn_bwd(name, att, q, k, v, do, lse, delta, bias=None, bias_t=None, rider=None):
    lse_r, dl_r = _stat_rows(att, lse), _stat_rows(att, delta)
    if att.mode == "dense":
        dq = _dense_dq(name + "_dq", att, q, k, v, do, lse, delta)
        res = _dense_dkv(name + "_dkv", att, q, k, v, do, lse_r, dl_r, rider=rider)
        return dq, res[0], res[1], list(res[2:])
    res = _band_dq(name + "_dq", att, q, k, v, do, lse, delta, bias=bias, itab=_interior_table(att, False, bias))
    dk, dv = _band_dkv(name + "_dkv", att, q, k, v, do, lse_r, dl_r, bias_t=bias_t,
                       itab_t=_interior_table(att, True, bias_t))
    return res[0], dk, dv, (res[1] if bias is not None else None)


def _row_call(name, body, T, tt, ins, outs, acc_outs=()):
    in_specs, args = [], []
    for item in ins:
        a = item[0]
        if item[1] is None:
            in_specs.append(pl.BlockSpec(a.shape, lambda i, nd=a.ndim: (0,) * nd))
        else:
            in_specs.append(pl.BlockSpec((tt, item[1]), lambda i, cb=item[2]: (i, cb)))
        args.append(a)
    out_shape = [_sds((T, c), d) for c, d in outs] + [_sds(s, F32) for s in acc_outs]
    out_specs = [pl.BlockSpec((tt, c), lambda i: (i, 0)) for c, _ in outs]
    out_specs += [pl.BlockSpec(s, lambda i, nd=len(s): (0,) * nd) for s in acc_outs]
    sem = ("arbitrary",) if acc_outs else ("parallel",)
    return _pc(body, name=name, grid=(T // tt,), out_shape=out_shape, in_specs=in_specs, out_specs=out_specs,
               compiler_params=_cparams(sem))(*args)


def _acc(ref, val):
    @pl.when(pl.program_id(0) == 0)
    def _():
        ref[...] = val

    @pl.when(pl.program_id(0) > 0)
    def _():
        ref[...] += val


def _prenorm_fwd(name, x, g, scale, shift):
    T, D = x.shape

    def body(x_ref, g_ref, sc_ref, sh_ref, h_ref):
        xf = x_ref[...]
        r = lax.rsqrt(jnp.mean(xf * xf, axis=1, keepdims=True) + EPS)
        h_ref[...] = ((xf * r) * g_ref[...] * (1.0 + sc_ref[...]) + sh_ref[...]).astype(BF16)

    return _row_call(name, body, T, min(T, 256), [(x, D, 0), (g, None), (scale, None), (shift, None)], [(D, BF16)])[0]


def _prenorm_bwd(name, x, dh, dxn, g, scale):
    T, D = x.shape

    def body(x_ref, dh_ref, dxn_ref, g_ref, sc_ref, dx_ref, dsh_ref, dsc_ref, dg_ref):
        xf, dh_, gg = x_ref[...], dh_ref[...], g_ref[...]
        r = lax.rsqrt(jnp.mean(xf * xf, axis=1, keepdims=True) + EPS)
        u = xf * r
        dn = dh_ * (1.0 + sc_ref[...])
        du = dn * gg
        dx_ref[...] = dxn_ref[...] + r * (du - u * jnp.mean(du * u, axis=1, keepdims=True))
        _acc(dsh_ref, jnp.sum(dh_, axis=0, keepdims=True))
        _acc(dsc_ref, jnp.sum(dh_ * (u * gg), axis=0, keepdims=True))
        _acc(dg_ref, jnp.sum(dn * u, axis=0, keepdims=True))

    return _row_call(name, body, T, min(T, 256), [(x, D, 0), (dh, D, 0), (dxn, D, 0), (g, None), (scale, None)],
                     [(D, F32)], [(1, D)] * 3)


def _resid_fwd(name, x, out, gate):
    T, D = x.shape

    def body(x_ref, o_ref, g_ref, y_ref):
        y_ref[...] = x_ref[...] + g_ref[...] * o_ref[...]

    return _row_call(name, body, T, min(T, 256), [(x, D, 0), (out, D, 0), (gate, None)], [(D, F32)])[0]


def _resid_bwd(name, dx, out, gate):
    T, D = dx.shape

    def body(dx_ref, o_ref, g_ref, do_ref, dg_ref):
        d = dx_ref[...]
        do_ref[...] = (d * g_ref[...]).astype(BF16)
        _acc(dg_ref, jnp.sum(d * o_ref[...], axis=0, keepdims=True))

    return _row_call(name, body, T, min(T, 256), [(dx, D, 0), (out, D, 0), (gate, None)], [(D, BF16)], [(1, D)])


def _final(name, x, tgt, g):
    T, D = x.shape

    def body(x_ref, t_ref, g_ref, dx_ref, loss_ref, dg_ref):
        xf, gg = x_ref[...], g_ref[...]
        r = lax.rsqrt(jnp.mean(xf * xf, axis=1, keepdims=True) + EPS)
        u = xf * r
        err = u * gg - t_ref[...]
        part = 0.5 * jnp.sum(jnp.mean(err * err, axis=1, keepdims=True), axis=0, keepdims=True)
        _acc(loss_ref, jnp.broadcast_to(part, (1, HD)))
        dy = err * (1.0 / D)
        _acc(dg_ref, jnp.sum(dy * u, axis=0, keepdims=True))
        du = dy * gg
        dx_ref[...] = r * (du - u * jnp.mean(du * u, axis=1, keepdims=True))

    return _row_call(name, body, T, min(T, 256), [(x, D, 0), (tgt, D, 0), (g, None)], [(D, F32)], [(1, HD), (1, D)])


def _roll_pair(x, shift):
    lanes = lax.broadcasted_iota(jnp.int32, x.shape, 1)
    r1 = pltpu.roll(x, shift, 1)
    r2 = pltpu.roll(x, HD - shift, 1)
    src = pltpu.roll(lanes, shift, 1)
    is_plus = src == ((lanes + shift) & (HD - 1))
    return jnp.where(is_plus, r1, r2), jnp.where(is_plus, r2, r1)


def _rope1(x, cos, ss):
    xp, _ = _roll_pair(x, 64)
    return x * cos + xp * ss


def _rope1_t(d, cos, ss):
    dp, _ = _roll_pair(d * ss, 64)
    return d * cos + dp


def _ropex(x, cos, sa, sb):
    xp, xm = _roll_pair(x, 32)
    return x * cos + xp * sa + xm * sb


def _ropex_t(d, cos, sa, sb):
    _, am = _roll_pair(d * sa, 32)
    bp, _ = _roll_pair(d * sb, 32)
    return d * cos + am + bp


def _qk_prep(name, qkvg, tabs, cqn, ckn):
    T = qkvg.shape[0]
    cos1, ss1, cosx, sax, sbx = tabs

    def body(aq, ak, bq, bk, bv, cq, ck, dq, c1, s1, cx, ax, bx, gq, gk, oaq, oak, obq, obk, obv, ocq, ock, odq):
        c1v, s1v = c1[...], s1[...]
        for src, dst, nh, mul in ((aq, oaq, 4, SCALE2), (ak, oak, 2, None), (bq, obq, 4, SCALE2), (bk, obk, 4, None)):
            for h in range(nh):
                sl = slice(h * HD, (h + 1) * HD)
                y = _rope1(src[:, sl].astype(F32), c1v, s1v)
                dst[:, sl] = (y if mul is None else y * mul).astype(BF16)
        obv[...] = bv[...]
        odq[...] = (dq[...].astype(F32) * SCALE2).astype(BF16)
        cxv, axv, bxv = cx[...], ax[...], bx[...]
        for src, dst, nh, gref, mul in ((cq, ocq, 4, gq, SCALE2), (ck, ock, 2, gk, None)):
            for h in range(nh):
                sl = slice(h * HD, (h + 1) * HD)
                xf = src[:, sl].astype(F32)
                r = lax.rsqrt(jnp.mean(xf * xf, axis=1, keepdims=True) + EPS)
                y = _ropex(xf * r * gref[...], cxv, axv, bxv)
                dst[:, sl] = (y if mul is None else y * mul).astype(BF16)

    ins = [(qkvg, 512, 0), (qkvg, 256, 2), (qkvg, 512, 3), (qkvg, 512, 4), (qkvg, 512, 5), (qkvg, 512, 7),
           (qkvg, 256, 16), (qkvg, 512, 10),
           (cos1, HD, 0), (ss1, HD, 0), (cosx, HD, 0), (sax, HD, 0), (sbx, HD, 0), (cqn, None), (ckn, None)]
    outs = [(512, BF16), (256, BF16), (512, BF16), (512, BF16), (512, BF16), (512, BF16), (256, BF16), (512, BF16)]
    return _row_call(name, body, T, min(T, 512), ins, outs)


def _silu_parts(g):
    sig = 1.0 / (1.0 + jnp.exp(-g))
    return g * sig, sig * (1.0 + g * (1.0 - sig))


def _mix_weights(l0, l1, l2):
    mx = jnp.maximum(jnp.maximum(l0, l1), l2)
    e0, e1, e2 = jnp.exp(l0 - mx), jnp.exp(l1 - mx), jnp.exp(l2 - mx)
    inv = 1.0 / (e0 + e1 + e2)
    return e0 * inv, e1 * inv, e2 * inv


def _gate_fwd(name, qkvg, oa, ob, lb, oc, od):
    T = qkvg.shape[0]

    def body(ga, gb, gc, gd, oa_r, ob0, ob1, ob2, lb0, lb1, lb2, oc_r, od_r, br):
        w0, w1, w2 = _mix_weights(lb0[...], lb1[...], lb2[...])
        yb = w0 * ob0[...].astype(F32) + w1 * ob1[...].astype(F32) + w2 * ob2[...].astype(F32)
        ys = (oa_r[...].astype(F32), yb, oc_r[...].astype(F32), od_r[...].astype(F32))
        for n, (y, g) in enumerate(zip(ys, (ga, gb, gc, gd))):
            act, _ = _silu_parts(g[...].astype(F32))
            br[:, n * 512:(n + 1) * 512] = (y * act).astype(BF16)

    ins = [(qkvg, 512, 2), (qkvg, 512, 6), (qkvg, 512, 9), (qkvg, 512, 13), (oa, 512, 0)]
    ins += [(o, 512, 0) for o in ob] + [(l_, 512, 0) for l_ in lb] + [(oc, 512, 0), (od, 512, 0)]
    return _row_call(name, body, T, min(T, 256), ins, [(2048, BF16)])[0]


def _head_rowsum(x):
    parts = []
    for h in range(x.shape[1] // HD):
        s = jnp.sum(x[:, h * HD:(h + 1) * HD], axis=1, keepdims=True)
        parts.append(jnp.broadcast_to(s, (x.shape[0], HD)))
    return jnp.concatenate(parts, axis=1)


def _gate_bwd(name, dbr, qkvg, oa, la, sink_row, ob, lb, oc, od):
    T = qkvg.shape[0]

    def body(dbr_r, ga, gb, gc, gd, oa_r, la_r, sk, ob0, ob1, ob2, lb0, lb1, lb2, oc_r, od_r,
             doa, dla, dob0, dob1, dob2, dlb0, dlb1, dlb2, doc, dlc, dod, dld, dg, dsk):
        def one(n, g_ref, y):
            act, dact = _silu_parts(g_ref[...].astype(F32))
            d = dbr_r[:, n * 512:(n + 1) * 512]
            dg[:, n * 512:(n + 1) * 512] = (d * y * dact).astype(BF16)
            return d * act

        ya = oa_r[...].astype(F32)
        dya = one(0, ga, ya)
        doa[...] = dya.astype(BF16)
        dl_a = _head_rowsum(dya * ya)
        dla[...] = dl_a
        _acc(dsk, -jnp.sum(jnp.exp(sk[...] - la_r[...]) * dl_a, axis=0, keepdims=True))

        w = _mix_weights(lb0[...], lb1[...], lb2[...])
        obs = (ob0[...].astype(F32), ob1[...].astype(F32), ob2[...].astype(F32))
        yb = w[0] * obs[0] + w[1] * obs[1] + w[2] * obs[2]
        dyb = one(1, gb, yb)
        rs = _head_rowsum(dyb * yb)
        for wp, do_ref, dl_ref in zip(w, (dob0, dob1, dob2), (dlb0, dlb1, dlb2)):
            do_ref[...] = (wp * dyb).astype(BF16)
            dl_ref[...] = wp * rs

        for n, g_ref, o_r, do_ref, dl_ref in ((2, gc, oc_r, doc, dlc), (3, gd, od_r, dod, dld)):
            y = o_r[...].astype(F32)
            dy = one(n, g_ref, y)
            do_ref[...] = dy.astype(BF16)
            dl_ref[...] = _head_rowsum(dy * y)

    ins = [(dbr, 2048, 0), (qkvg, 512, 2), (qkvg, 512, 6), (qkvg, 512, 9), (qkvg, 512, 13),
           (oa, 512, 0), (la, 512, 0), (sink_row, None)]
    ins += [(o, 512, 0) for o in ob] + [(l_, 512, 0) for l_ in lb] + [(oc, 512, 0), (od, 512, 0)]
    outs = [(512, BF16), (512, F32)] + [(512, BF16)] * 3 + [(512, F32)] * 3 + [(512, BF16), (512, F32)] * 2
    outs += [(2048, BF16)]
    return _row_call(name, body, T, min(T, 256), ins, outs, [(1, 512)])


def _merge_fwd(name, mgl, proj, D):
    T = mgl.shape[0]

    def body(m_ref, p_ref, o_ref):
        acc = None
        for n in range(4):
            sl = slice(n * D, (n + 1) * D)
            sig = 1.0 / (1.0 + jnp.exp(-m_ref[:, sl].astype(F32)))
            term = sig * p_ref[:, sl].astype(F32)
            acc = term if acc is None else acc + term
        o_ref[...] = acc.astype(BF16)

    return _row_call(name, body, T, min(T, 256), [(mgl, 4 * D, 0), (proj, 4 * D, 0)], [(D, BF16)])[0]


def _merge_bwd(name, dm, mgl, proj, D):
    T = mgl.shape[0]

    def body(d_ref, m_ref, p_ref, dp_ref, dl_ref):
        d = d_ref[...]
        for n in range(4):
            sl = slice(n * D, (n + 1) * D)
            sig = 1.0 / (1.0 + jnp.exp(-m_ref[:, sl].astype(F32)))
            dp_ref[:, sl] = (d * sig).astype(BF16)
            dl_ref[:, sl] = (d * p_ref[:, sl].astype(F32) * sig * (1.0 - sig)).astype(BF16)

    return _row_call(name, body, T, min(T, 128), [(dm, D, 0), (mgl, 4 * D, 0), (proj, 4 * D, 0)],
                     [(4 * D, BF16), (4 * D, BF16)])


def _qkvg_bwd(name, qkvg, tabs, cqn, ckn, dA, dB, dC, dD, dg):
    T = qkvg.shape[0]
    cos1, ss1, cosx, sax, sbx = tabs

    def body(cq_r, ck_r, c1, s1, cx, ax, bx, gq, gk,
             dqa, dka, dva, dqb0, dqb1, dqb2, dkb0, dkb1, dkb2, dvb0, dvb1, dvb2,
             dqc, dkc, dvc, dqd, dkd, dvd, dg_r, out, dgq, dgk):
        c1v, s1v = c1[...], s1[...]

        def put(col, val):
            out[:, col * HD:(col + 1) * HD] = val.astype(BF16)

        def pair(ref, kv):
            return ref[:, 2 * kv * HD:(2 * kv + 1) * HD] + ref[:, (2 * kv + 1) * HD:(2 * kv + 2) * HD]

        for h in range(4):
            sl = slice(h * HD, (h + 1) * HD)
            put(A_Q + h, _rope1_t(dqa[:, sl], c1v, s1v))
            put(B_Q + h, _rope1_t(dqb0[:, sl] + dqb1[:, sl] + dqb2[:, sl], c1v, s1v))
            put(B_K + h, _rope1_t(dkb0[:, sl] + dkb1[:, sl] + dkb2[:, sl], c1v, s1v))
            put(B_V + h, dvb0[:, sl] + dvb1[:, sl] + dvb2[:, sl])
            put(D_Q + h, dqd[:, sl])
            put(D_K + h, dkd[:, sl])
            put(D_V + h, dvd[:, sl])
        for kv in range(2):
            put(A_K + kv, _rope1_t(pair(dka, kv), c1v, s1v))
            put(A_V + kv, pair(dva, kv))
            put(C_V + kv, pair(dvc, kv))
        cxv, axv, bxv = cx[...], ax[...], bx[...]
        for src, dref, col, nh, gref, dgref in ((cq_r, dqc, C_Q, 4, gq, dgq), (ck_r, dkc, C_K, 2, gk, dgk)):
            gsum = None
            for h in range(nh):
                sl = slice(h * HD, (h + 1) * HD)
                xf = src[:, sl].astype(F32)
                r = lax.rsqrt(jnp.mean(xf * xf, axis=1, keepdims=True) + EPS)
                u = xf * r
                dy = _ropex_t(dref[:, sl] if nh == 4 else pair(dref, h), cxv, axv, bxv)
                du = dy * gref[...]
                put(col + h, r * (du - u * jnp.mean(du * u, axis=1, keepdims=True)))
                part = jnp.sum(dy * u, axis=0, keepdims=True)
                gsum = part if gsum is None else gsum + part
            _acc(dgref, gsum)
        for n, col in enumerate((A_G, B_G, C_G, D_G)):
            out[:, col * HD:(col + 4) * HD] = dg_r[:, n * 512:(n + 1) * 512]

    ins = [(qkvg, 512, 7), (qkvg, 256, 16), (cos1, HD, 0), (ss1, HD, 0), (cosx, HD, 0), (sax, HD, 0), (sbx, HD, 0),
           (cqn, None), (ckn, None)]
    ins += [(a, 512, 0) for a in dA]
    ins += [(a, 512, 0) for a in dB[0]] + [(a, 512, 0) for a in dB[1]] + [(a, 512, 0) for a in dB[2]]
    ins += [(a, 512, 0) for a in dC] + [(a, 512, 0) for a in dD] + [(dg, 2048, 0)]
    return _row_call(name, body, T, min(T, 256), ins, [(IN_HEADS * HD, BF16)], [(1, HD), (1, HD)])


def _ada_fwd(name, c_pad, w_ada):
    L, D, n = w_ada.shape
    P = c_pad.shape[0]

    def body(c_ref, w_ref, o_ref):
        cc = c_ref[...]
        cond = cc / (1.0 + jnp.exp(-cc))
        o_ref[...] = jnp.dot(cond.astype(BF16), w_ref[...].astype(BF16), preferred_element_type=F32)

    return _pc(body, name=name, grid=(L,), out_shape=_sds((L, P, n), F32),
               in_specs=[pl.BlockSpec((P, D), lambda l: (0, 0)), pl.BlockSpec((None, D, n), lambda l: (l, 0, 0))],
               out_specs=pl.BlockSpec((None, P, n), lambda l: (l, 0, 0)),
               compiler_params=_cparams(("parallel",)))(c_pad, w_ada)


def _ada_grad(name, c_pad, d_ada):
    L, P, n = d_ada.shape
    D = c_pad.shape[1]

    def body(c_ref, d_ref, o_ref):
        cc = c_ref[...]
        cond = cc / (1.0 + jnp.exp(-cc))
        o_ref[...] = lax.dot_general(cond.astype(BF16), d_ref[...].astype(BF16), _TN, preferred_element_type=F32)

    return _pc(body, name=name, grid=(L,), out_shape=_sds((L, D, n), F32),
               in_specs=[pl.BlockSpec((P, D), lambda l: (0, 0)), pl.BlockSpec((None, P, n), lambda l: (l, 0, 0))],
               out_specs=pl.BlockSpec((None, D, n), lambda l: (l, 0, 0)),
               compiler_params=_cparams(("parallel",)))(c_pad, d_ada)


def _adam_rows(rows, cols):
    cap = max(8, (2 ** 20) // (4 * cols))
    if rows <= cap:
        return rows
    tr = 8
    while tr * 2 <= cap and rows % (tr * 2) == 0:
        tr *= 2
    return tr


def _adam_update(g_ref, slots, w_ref, m_ref, v_ref, go_ref, d_ref, mo_ref, vo_ref):
    if slots:
        gg = g_ref[0].astype(F32)
        for k in range(1, slots):
            gg = gg + g_ref[k].astype(F32)
    else:
        gg = g_ref[...]
    c1 = 1.0 / (1.0 - ADAM_B1 ** ADAM_STEP)
    c2 = 1.0 / (1.0 - ADAM_B2 ** ADAM_STEP)
    mn = ADAM_B1 * m_ref[...] + (1.0 - ADAM_B1) * gg
    vn = ADAM_B2 * v_ref[...] + (1.0 - ADAM_B2) * (gg * gg)
    go_ref[...] = gg
    mo_ref[...] = mn
    vo_ref[...] = vn
    d_ref[...] = -ADAM_LR * ((mn * c1) / (jnp.sqrt(vn * c2) + ADAM_EPS) + ADAM_WD * w_ref[...])


def _adamw(name, w, m, v, g, slots):
    rows, cols = w.shape
    tr = _adam_rows(rows, cols)

    def body(w_ref, m_ref, v_ref, g_ref, *outs):
        _adam_update(g_ref, slots, w_ref, m_ref, v_ref, *outs)

    blk = pl.BlockSpec((tr, cols), lambda i: (i, 0))
    gspec = pl.BlockSpec((slots, tr, cols), lambda i: (0, i, 0)) if slots else blk
    return _pc(body, name=name, grid=(rows // tr,), out_shape=[_sds((rows, cols), F32)] * 4,
               in_specs=[blk, blk, blk, gspec], out_specs=[blk] * 4,
               compiler_params=_cparams(("parallel",)))(w, m, v, g)


def _adamw_layers(name, w, m, v, slot_list):
    nl = len(slot_list)
    slots, rows, cols = slot_list[0].shape
    tr = _adam_rows(rows, cols)
    nblk = rows // tr

    def body(w_ref, m_ref, v_ref, *rest):
        layer = pl.program_id(0)
        for ll in range(nl):
            @pl.when(layer == ll)
            def _(g_ref=rest[ll]):
                _adam_update(g_ref, slots, w_ref, m_ref, v_ref, *rest[nl:])

    def gspec(ll):
        return pl.BlockSpec((slots, tr, cols),
                            lambda l, i: (0, jnp.where(l == ll, i, jnp.where(l > ll, nblk - 1, 0)), 0))

    blk = pl.BlockSpec((tr, cols), lambda l, i: (l * nblk + i, 0))
    return _pc(body, name=name, grid=(nl, nblk), out_shape=[_sds((nl * rows, cols), F32)] * 4,
               in_specs=[blk, blk, blk] + [gspec(ll) for ll in range(nl)], out_specs=[blk] * 4,
               compiler_params=_cparams(("parallel", "parallel")))(w, m, v, *slot_list)


def _rope_tables(T):
    pos = np.arange(T)
    lane = np.arange(HD)
    inv = ROPE_THETA ** (-np.arange(0, HD, 2, dtype=np.float32) / HD)
    ang = pos.astype(np.float32)[:, None] * inv[None, :]
    ang = np.concatenate([ang, ang], axis=-1)
    cos1 = np.cos(ang)
    ss1 = np.sin(ang) * np.where(lane < HD // 2, -1.0, 1.0)[None, :]
    half = HD // 2
    invh = ROPE_THETA ** (-np.arange(0, half, 2, dtype=np.float32) / half)

    def tab(p):
        a = p.astype(np.float32)[:, None] * invh[None, :]
        return np.concatenate([a, a], axis=-1)

    angx = np.concatenate([tab(pos // GRID_W), tab(pos % GRID_W)], axis=-1)
    cosx, sinx = np.cos(angx), np.sin(angx)
    first = (lane % half) < half // 2
    sax = np.where(first[None, :], -sinx, 0.0)
    sbx = np.where(first[None, :], 0.0, sinx)
    return tuple(jnp.asarray(a, F32) for a in (cos1, ss1, cosx, sax, sbx))


def _na_selectors(att, transposed):
    own_r = np.arange(att.t // GRID_W)[:, None]
    span_r = np.arange(att.span // GRID_W)[None, :] - att.pad // GRID_W
    dr = own_r - span_r if transposed else span_r - own_r
    row_sel = (np.clip(dr, -(NA_ROWS - 1), NA_ROWS - 1)[..., None] + NA_ROWS - 1 == np.arange(2 * NA_ROWS - 1))
    col = np.arange(GRID_W)
    dc = col[:, None] - col[None, :] if transposed else col[None, :] - col[:, None]
    col_sel = (np.clip(dc, -(NA_COLS - 1), NA_COLS - 1)[..., None] + NA_COLS - 1 == np.arange(2 * NA_COLS - 1))
    return jnp.asarray(row_sel.astype(np.float32)), jnp.asarray(col_sel.astype(np.float32))


def _na_bias_table(att, rel_bias, transposed):
    row_sel, col_sel = _na_selectors(att, transposed)
    tab = jnp.einsum("xyi,hij,cdj->hxcyd", row_sel, rel_bias * LOG2E, col_sel, precision=lax.Precision.HIGHEST)
    return tab.reshape(rel_bias.shape[0], att.t, att.span)


def _na_bias_grad(att, dtab):
    row_sel, col_sel = _na_selectors(att, False)
    d5 = dtab.reshape(dtab.shape[0], att.t // GRID_W, GRID_W, att.span // GRID_W, GRID_W)
    return jnp.einsum("xyi,hxcyd,cdj->hij", row_sel, d5, col_sel, precision=lax.Precision.HIGHEST)


def _att_specs(T):
    specs = {
        "A": _Att(T, 1, 4, 2, 0, 0, A_V, 0, 0, 0, "band", reach=A_REACH, pad=128),
        "C": _Att(T, 1, 4, 2, 0, 0, C_V, 0, 0, 0, "dense"),
        "D": _Att(T, 1, 4, 1, 0, D_K, D_V, 0, 0, 0, "na", pad=256, rows=T // GRID_W),
    }
    for p, (window, dil) in enumerate(B_PATTERNS):
        specs["B%d" % p] = _Att(T // dil, dil, 4, 1, 0, 0, 0, 4, 4, 4, "band", reach=(window // 2) // dil, pad=64)
    return specs


def kernel(x, c, norm_g, w_ada, b_ada, w_in, a_sink, c_q_norm, c_k_norm, d_rel_bias, w_gate_merge, w_branch, w_out, final_g, loss_target, m_norm_g, m_w_ada, m_b_ada, m_w_in, m_a_sink, m_c_q_norm, m_c_k_norm, m_d_rel_bias, m_w_gate_merge, m_w_branch, m_w_out, m_final_g, v_norm_g, v_w_ada, v_b_ada, v_w_in, v_a_sink, v_c_q_norm, v_c_k_norm, v_d_rel_bias, v_w_gate_merge, v_w_branch, v_w_out, v_final_g):
    T, D = x.shape[1], x.shape[2]
    NL = norm_g.shape[0]
    x0 = x.reshape(T, D)
    tgt = loss_target.reshape(T, D)
    me = 4 * lax.axis_index("x") + 2 * lax.axis_index("y") + lax.axis_index("c")
    att = _att_specs(T)
    tabs = _rope_tables(T)
    n_ada = w_ada.shape[2]

    (c_all,) = _exchange("gather_c", [c], gather=True)
    c_pad = jnp.pad(c_all.reshape(N_DEV, D), ((0, HD - N_DEV), (0, 0)))
    ada_part = _ada_fwd("ada_fwd", c_pad, w_ada)[:, :N_DEV]
    (ada_all,) = _exchange("gather_ada", [ada_part], gather=True)
    ada_mine = lax.dynamic_index_in_dim(ada_all, me, axis=2, keepdims=False)
    ada = jnp.transpose(ada_mine, (1, 0, 2)).reshape(NL, N_DEV * n_ada) + b_ada
    shift, scale, gate = ada[:, :D], ada[:, D:2 * D], ada[:, 2 * D:]

    shards = [[w[l].astype(BF16) for w in (w_in, w_gate_merge, w_branch, w_out)] for l in range(NL)]
    weights = [_exchange("gather_w0", shards[0], gather=True)] + [None] * (NL - 1)

    def views(a_, arrs):
        return [a.reshape(a_.L, -1) for a in arrs]

    saved = []
    xl = x0
    for l in range(NL):
        Wi, Wg, Wb, Wo = weights[l]
        Wo = Wo.reshape(1, D, D)
        sc_l, sh_l, gt_l = scale[l][None], shift[l][None], gate[l][None]
        ng_l = norm_g[l][None]
        cqn, ckn = c_q_norm[l][None], c_k_norm[l][None]
        sink3 = jnp.broadcast_to(a_sink[l][:, None, None], (4, 1, HD))
        btab = _na_bias_table(att["D"], d_rel_bias[l], False)

        h = _prenorm_fwd("prenorm_fwd", xl, ng_l, sc_l, sh_l)
        qkvg = _mm_nn("mm_in", h, Wi, BF16)
        mgl = _mm_nn("mm_gm", h, Wg, BF16)
        qa, ka, qb, kb, vb, qc, kc, qd = _qk_prep("qk_prep", qkvg, tabs, cqn, ckn)
        oa, la = _attn_fwd("attn_a_fwd", att["A"], qa, ka, qkvg, sink=sink3)
        ob, lb = [], []
        for p in range(len(B_PATTERNS)):
            a_ = att["B%d" % p]
            o_, l_ = _attn_fwd("attn_b%d_fwd" % p, a_, *views(a_, (qb, kb, vb)))
            ob.append(o_.reshape(T, 512))
            lb.append(l_.reshape(T, 512))
        if l + 1 < NL:
            oc, lc, *nxt = _attn_fwd("attn_c_fwd_gather", att["C"], qc, kc, qkvg, rider=_Rider(shards[l + 1], True))
            weights[l + 1] = nxt
        else:
            oc, lc = _attn_fwd("attn_c_fwd", att["C"], qc, kc, qkvg)
        od, ld = _attn_fwd("attn_d_fwd", att["D"], qd, qkvg, qkvg, bias=btab)
        br = _gate_fwd("gate_fwd", qkvg, oa, ob, lb, oc, od)
        proj = _branch_nn("mm_branch", br, Wb, BF16)
        merged = _merge_fwd("merge_fwd", mgl, proj, D)
        out = _mm_nn("mm_out", merged, Wo, F32)
        xn = _resid_fwd("resid_fwd", xl, out, gt_l)
        saved.append(dict(x=xl, h=h, qkvg=qkvg, mgl=mgl, qa=qa, ka=ka, qb=qb, kb=kb, vb=vb, qc=qc, kc=kc, qd=qd, oa=oa,
                          la=la, ob=ob, lb=lb, oc=oc, lc=lc, od=od, ld=ld, br=br, proj=proj, merged=merged, out=out,
                          btab=btab))
        xl = xn

    dx, loss_part, d_final_g = _final("final_loss", xl, tgt, final_g[None])

    grads, slots = [None] * NL, [None] * NL
    d_norm_g, d_ada, d_sink, d_cq, d_ck, d_bias = [None] * NL, [None] * NL, [None] * NL, [None] * NL, [None] * NL, [None] * NL
    for l in reversed(range(NL)):
        S = saved[l]
        Wi, Wg, Wb, Wo = weights[l]
        Wo = Wo.reshape(1, D, D)
        sc_l, gt_l, ng_l = scale[l][None], gate[l][None], norm_g[l][None]
        cqn, ckn = c_q_norm[l][None], c_k_norm[l][None]
        sink_row = jnp.repeat(a_sink[l], HD)[None]
        qkvg = S["qkvg"]

        dout, dgate = _resid_bwd("resid_bwd", dx, S["out"], gt_l)
        g_out = _mm_tn("mm_out_dw", S["merged"], dout, 1, BF16).reshape(N_DEV, D // N_DEV, D)
        dm = _mm_nt("mm_out_dx", dout, Wo, F32)
        dproj, dmgl = _merge_bwd("merge_bwd", dm, S["mgl"], S["proj"], D)
        g_br = _branch_tn("mm_branch_dw", S["br"], dproj, D // N_DEV, BF16)
        dbr = _branch_nt("mm_branch_dx", dproj, Wb, F32)
        (doa, dla, dob0, dob1, dob2, dlb0, dlb1, dlb2, doc, dlc, dod, dld, dg, dsk) = _gate_bwd(
            "gate_bwd", dbr, qkvg, S["oa"], S["la"], sink_row, S["ob"], S["lb"], S["oc"], S["od"])
        d_sink[l] = dsk.reshape(4, HD)[:, 0]

        dqa, dka, dva, _ = _attn_bwd("attn_a", att["A"], S["qa"], S["ka"], qkvg, doa, S["la"], dla)
        dqb, dkb, dvb = [], [], []
        for p, (dob, dlb) in enumerate(((dob0, dlb0), (dob1, dlb1), (dob2, dlb2))):
            a_ = att["B%d" % p]
            dq_, dk_, dv_, _ = _attn_bwd("attn_b%d" % p, a_, *views(a_, (S["qb"], S["kb"], S["vb"], dob, S["lb"][p], dlb)))
            dqb.append(dq_.reshape(T, 512))
            dkb.append(dk_.reshape(T, 512))
            dvb.append(dv_.reshape(T, 512))
        if l + 1 < NL:
            dqc, dkc, dvc, slots[l + 1] = _attn_bwd("attn_c_scatter", att["C"], S["qc"], S["kc"], qkvg, doc, S["lc"], dlc,
                                                    rider=_Rider(grads[l + 1], False))
        else:
            dqc, dkc, dvc, _ = _attn_bwd("attn_c", att["C"], S["qc"], S["kc"], qkvg, doc, S["lc"], dlc)
        dqd, dkd, dvd, dtab = _attn_bwd("attn_d", att["D"], S["qd"], qkvg, qkvg, dod, S["ld"], dld, bias=S["btab"],
                                        bias_t=_na_bias_table(att["D"], d_rel_bias[l], True))
        d_bias[l] = _na_bias_grad(att["D"], dtab)

        dqkvg, dcq, dck = _qkvg_bwd("qkvg_bwd", qkvg, tabs, cqn, ckn, (dqa, dka, dva), (dqb, dkb, dvb),
                                    (dqc, dkc, dvc), (dqd, dkd, dvd), dg)
        d_cq[l], d_ck[l] = dcq[0], dck[0]
        g_in = _mm_tn("mm_in_dw", S["h"], dqkvg, N_DEV, BF16)
        g_gm = _mm_tn("mm_gm_dw", S["h"], dmgl, N_DEV, BF16)
        grads[l] = [g_in, g_gm, g_br, g_out]
        if l == 0:
            dh, s_in, s_br, s_out = _mm_nt("mm_in_dx_scatter", dqkvg, Wi, F32, rider=_Rider([g_in, g_br, g_out], False))
            dh, s_gm = _mm_nt("mm_gm_dx_scatter", dmgl, Wg, F32, add=dh, rider=_Rider([g_gm], False))
            slots[0] = [s_in, s_gm, s_br, s_out]
        else:
            dh = _mm_nt("mm_in_dx", dqkvg, Wi, F32)
            dh = _mm_nt("mm_gm_dx", dmgl, Wg, F32, add=dh)
        dx, dshift, dscale, dng = _prenorm_bwd("prenorm_bwd", S["x"], dh, dx, ng_l, sc_l)
        d_norm_g[l] = dng[0]
        d_ada[l] = jnp.concatenate([dshift[0], dscale[0], dgate[0]])

    grad_x = dx.reshape(1, T, D)

    def big(name, w, m, v, k):
        cols = w.shape[-1]
        res = _adamw_layers(name, w.reshape(-1, cols), m.reshape(-1, cols), v.reshape(-1, cols),
                            [slots[l][k].reshape(N_DEV, -1, cols) for l in range(NL)])
        return [r.reshape(w.shape) for r in res]

    r_in = big("adamw_in", w_in, m_w_in, v_w_in, 0)
    r_gm = big("adamw_gm", w_gate_merge, m_w_gate_merge, v_w_gate_merge, 1)
    r_br = big("adamw_branch", w_branch, m_w_branch, v_w_branch, 2)
    r_out = big("adamw_out", w_out, m_w_out, v_w_out, 3)

    small_w = [norm_g, b_ada, a_sink, c_q_norm, c_k_norm, d_rel_bias, final_g]
    small_m = [m_norm_g, m_b_ada, m_a_sink, m_c_q_norm, m_c_k_norm, m_d_rel_bias, m_final_g]
    small_v = [v_norm_g, v_b_ada, v_a_sink, v_c_q_norm, v_c_k_norm, v_d_rel_bias, v_final_g]
    small_g = [jnp.stack(d_norm_g), jnp.stack(d_ada), jnp.stack(d_sink), jnp.stack(d_cq), jnp.stack(d_ck),
               jnp.stack(d_bias), d_final_g[0]]
    sizes = [int(np.prod(w.shape)) for w in small_w]
    total = sum(sizes) + HD
    rows = -(-total // (8 * HD)) * 8

    def pack(parts, extra):
        flat = jnp.concatenate([p.reshape(-1).astype(F32) for p in parts] + [extra])
        return jnp.pad(flat, (0, rows * HD - flat.shape[0])).reshape(rows, HD)

    zeros = jnp.zeros((HD,), F32)
    (g_slots,) = _exchange("gather_small", [pack(small_g, loss_part[0])], gather=True)
    gs, ds, ms, vs = _adamw("adamw_small", pack(small_w, zeros), pack(small_m, zeros), pack(small_v, zeros), g_slots, N_DEV)

    def unpack(flat2d):
        flat = flat2d.reshape(-1)
        res, o = [], 0
        for w, n in zip(small_w, sizes):
            res.append(flat[o:o + n].reshape(w.shape))
            o += n
        return res, flat[o]

    sg, loss = unpack(gs)
    sd, _ = unpack(ds)
    sm, _ = unpack(ms)
    sv, _ = unpack(vs)

    o_b = sizes[0]
    d_ada_all = g_slots.reshape(N_DEV, -1)[:, o_b:o_b + sizes[1]].reshape(N_DEV, NL, N_DEV, n_ada)
    d_ada_mine = jnp.transpose(lax.dynamic_index_in_dim(d_ada_all, me, axis=2, keepdims=False), (1, 0, 2))
    g_ada = _ada_grad("ada_grad", c_pad, jnp.pad(d_ada_mine, ((0, 0), (0, HD - N_DEV), (0, 0))))
    r_ada = _adamw("adamw_ada", w_ada.reshape(-1, n_ada), m_w_ada.reshape(-1, n_ada), v_w_ada.reshape(-1, n_ada),
                   g_ada.reshape(-1, n_ada), 0)
    r_ada = [r.reshape(w_ada.shape) for r in r_ada]

    def kind(k):
        sm_ = (sg, sd, sm, sv)[k]
        return [sm_[0], r_ada[k], sm_[1], r_in[k], sm_[2], sm_[3], sm_[4], sm_[5], r_gm[k], r_br[k], r_out[k], sm_[6]]

    return (loss, grad_x, *kind(0), *kind(1), *kind(2), *kind(3))
```

```python
import jax
import jax.numpy as jnp
import numpy as np
from jax import lax
from jax.experimental import pallas as pl
from jax.experimental.pallas import tpu as pltpu

F32 = jnp.float32
BF16 = jnp.bfloat16

N_DEV = 8
HD = 128
GRID_W = 64
EPS = 1e-6
NEG_INF = -1e30
ROPE_THETA = 10000.0
A_REACH = 128
B_PATTERNS = ((128, 1), (512, 4), (2048, 16))
NA_ROWS = 8
NA_COLS = 16
SCALE = HD ** -0.5
LOG2E = 1.4426950408889634
LN2 = 0.6931471805599453
SCALE2 = SCALE * LOG2E
A_Q, A_K, A_V, A_G = 0, 4, 6, 8
B_Q, B_K, B_V, B_G = 12, 16, 20, 24
C_Q, C_K, C_V, C_G = 28, 32, 34, 36
D_Q, D_K, D_V, D_G = 40, 44, 48, 52
IN_HEADS = 56

ADAM_LR, ADAM_B1, ADAM_B2, ADAM_EPS, ADAM_WD, ADAM_STEP = 0.001, 0.9, 0.999, 1e-08, 0.01, 10

V7X_VMEM_LIMIT = 56 * 2 ** 20
ATT_TILE = 512
BAND_TILE = 1024
DENSE_TILE = 2048


def _pc(body, **kw):
    return pl.pallas_call(body, **kw)


def _cparams(sem=None):
    if sem is None:
        return pltpu.CompilerParams(vmem_limit_bytes=V7X_VMEM_LIMIT)
    return pltpu.CompilerParams(dimension_semantics=sem, vmem_limit_bytes=V7X_VMEM_LIMIT)


def _tile(n, cap):
    if n <= cap:
        return n
    t = (cap // 128) * 128
    while n % t:
        t -= 128
    return t


def _sds(shape, dtype):
    return jax.ShapeDtypeStruct(tuple(shape), dtype)


def _exchange_out(ins, gather):
    return [_sds((N_DEV,) + (a.shape if gather else a.shape[1:]), a.dtype) for a in ins]


def _exchange_sems(n):
    return [pltpu.SemaphoreType.DMA((n, N_DEV - 1)), pltpu.SemaphoreType.DMA((n, N_DEV - 1)), pltpu.SemaphoreType.DMA((n,))]


def _exchange_copies(in_refs, out_refs, send_sems, recv_sems, loc_sems, gather):
    x, y, c = lax.axis_index("x"), lax.axis_index("y"), lax.axis_index("c")
    me = 4 * x + 2 * y + c
    copies = []
    for t, (src, dst) in enumerate(zip(in_refs, out_refs)):
        copies.append(pltpu.make_async_copy(src if gather else src.at[me], dst.at[me], loc_sems.at[t]))
    for p in range(1, N_DEV):
        tx = 1 - x if p & 4 else x
        ty = 1 - y if p & 2 else y
        tc = 1 - c if p & 1 else c
        peer = 4 * tx + 2 * ty + tc
        for t, (src, dst) in enumerate(zip(in_refs, out_refs)):
            copies.append(pltpu.make_async_remote_copy(
                src_ref=src if gather else src.at[peer], dst_ref=dst.at[me],
                send_sem=send_sems.at[t, p - 1], recv_sem=recv_sems.at[t, p - 1],
                device_id=(tx, ty, tc), device_id_type=pl.DeviceIdType.MESH))
    return copies


def _exchange(name, ins, gather):
    n = len(ins)

    def body(*refs):
        copies = _exchange_copies(refs[:n], refs[n:2 * n], *refs[2 * n:], gather)
        for cp in copies:
            cp.start()
        for cp in copies:
            cp.wait()

    res = _pc(body, name=name, out_shape=_exchange_out(ins, gather),
              in_specs=[pl.BlockSpec(memory_space=pl.ANY)] * n, out_specs=[pl.BlockSpec(memory_space=pl.ANY)] * n,
              scratch_shapes=_exchange_sems(n))(*ins)
    return list(res)


class _Rider:
    def __init__(self, ins, gather):
        self.ins, self.gather, self.n = list(ins), gather, len(ins)

    def specs(self):
        any_ = [pl.BlockSpec(memory_space=pl.ANY)] * self.n
        return any_, _exchange_out(self.ins, self.gather), any_, _exchange_sems(self.n)

    def start(self, in_refs, out_refs, sems, grid):
        first = pl.program_id(0) == 0
        for ax in range(1, len(grid)):
            first = first & (pl.program_id(ax) == 0)

        @pl.when(first)
        def _():
            for cp in _exchange_copies(in_refs, out_refs, *sems, self.gather):
                cp.start()

    def finish(self, in_refs, out_refs, sems, grid):
        last = pl.program_id(0) == grid[0] - 1
        for ax in range(1, len(grid)):
            last = last & (pl.program_id(ax) == grid[ax] - 1)

        @pl.when(last)
        def _():
            for cp in _exchange_copies(in_refs, out_refs, *sems, self.gather):
                cp.wait()


_NN = (((1,), (0,)), ((), ()))
_NT = (((1,), (1,)), ((), ()))
_TN = (((0,), (0,)), ((), ()))


def _dot(a, b, dims):
    return lax.dot_general(a, b, dims, preferred_element_type=F32)


def _mm(name, a, w, out_sds, grid, a_spec, w_spec, o_spec, prod, red_axis=None, add=None, acc_shape=None,
        store=None, rider=None):
    nred = grid[red_axis] if red_axis is not None else 1
    if acc_shape is None:
        acc_shape = tuple(d for d in o_spec.block_shape if d is not None)
    nin = 3 if add is not None else 2
    nr = rider.n if rider is not None else 0

    def body(*refs):
        a_ref, w_ref = refs[:2]
        add_ref = refs[2] if add is not None else None
        o_ref = refs[nin + nr]
        scr = refs[nin + nr + 1 + nr:]
        comm = (refs[nin:nin + nr], refs[nin + nr + 1:nin + nr + 1 + nr], scr[-3:])
        if rider is not None:
            rider.start(*comm, grid)
        part = prod(a_ref, w_ref)

        def finish(acc):
            if add_ref is not None:
                acc = acc + add_ref[...].astype(F32)
            if store is not None:
                store(o_ref, acc)
            else:
                o_ref[...] = acc.astype(o_ref.dtype)

        if nred == 1:
            finish(part)
        else:
            acc_ref = scr[0]
            k = pl.program_id(red_axis)

            @pl.when(k == 0)
            def _():
                acc_ref[...] = part

            @pl.when(k > 0)
            def _():
                acc_ref[...] += part

            @pl.when(k == nred - 1)
            def _():
                finish(acc_ref[...])
        if rider is not None:
            rider.finish(*comm, grid)

    r_in, r_out, r_outspec, r_sems = rider.specs() if rider is not None else ([], [], [], [])
    if rider is not None:
        sem = ("arbitrary",) * len(grid)
    else:
        sem = tuple("arbitrary" if i == red_axis else "parallel" for i in range(len(grid)))
    in_specs = [a_spec, w_spec] + ([o_spec] if add is not None else []) + r_in
    args = (a, w) + ((add,) if add is not None else ()) + tuple(rider.ins if rider is not None else ())
    res = _pc(body, name=name, out_shape=[out_sds] + r_out, grid=grid, in_specs=in_specs,
              out_specs=[o_spec] + r_outspec,
              scratch_shapes=([pltpu.VMEM(acc_shape, F32)] if nred > 1 else []) + r_sems,
              compiler_params=_cparams(sem))(*args)
    return res[0] if rider is None else res


def _mm_nn(name, a, w, out_dtype, rider=None):
    T, K = a.shape
    G, _, n = w.shape
    tm, tn = min(T, 1024), _tile(n, 1024)
    nj = n // tn
    return _mm(name, a, w, _sds((T, G * n), out_dtype), (T // tm, G, nj),
               pl.BlockSpec((tm, K), lambda i, g, j: (i, 0)),
               pl.BlockSpec((None, K, tn), lambda i, g, j: (g, 0, j)),
               pl.BlockSpec((tm, tn), lambda i, g, j: (i, g * nj + j)),
               lambda a_ref, w_ref: _dot(a_ref[...], w_ref[...], _NN), rider=rider)


def _mm_nt(name, dy, w, out_dtype, add=None, rider=None):
    T = dy.shape[0]
    G, K, n = w.shape
    tm, tk = min(T, 1024), _tile(K, 1024)
    cps = 2 if G % 2 == 0 and n <= 1024 else 1

    def prod(a_ref, w_ref):
        acc = _dot(a_ref[:, 0:n], w_ref[0], _NT)
        for c in range(1, cps):
            acc = acc + _dot(a_ref[:, c * n:(c + 1) * n], w_ref[c], _NT)
        return acc

    return _mm(name, dy, w, _sds((T, K), out_dtype), (T // tm, K // tk, G // cps),
               pl.BlockSpec((tm, cps * n), lambda i, k, r: (i, r)),
               pl.BlockSpec((cps, tk, n), lambda i, k, r: (r, k, 0)),
               pl.BlockSpec((tm, tk), lambda i, k, r: (i, k)), prod, red_axis=2, add=add, rider=rider)


def _mm_tn(name, a, dy, G, out_dtype):
    T, K = a.shape
    n = dy.shape[1] // G
    tt, tk, tn = min(T, 2048), _tile(K, 1024), _tile(n, 1024)
    nj = n // tn
    return _mm(name, a, dy, _sds((G, K, n), out_dtype), (K // tk, G, nj, T // tt),
               pl.BlockSpec((tt, tk), lambda k, g, j, t: (t, k)),
               pl.BlockSpec((tt, tn), lambda k, g, j, t: (t, g * nj + j)),
               pl.BlockSpec((None, tk, tn), lambda k, g, j, t: (g, k, j)),
               lambda a_ref, w_ref: _dot(a_ref[...], w_ref[...], _TN), red_axis=3)


def _branch_weight(w_ref):
    return jnp.concatenate([w_ref[d] for d in range(N_DEV)], axis=1)


def _branch_nn(name, br, wb, out_dtype):
    T = br.shape[0]
    _, NB, W, n = wb.shape
    tm = min(T, 1024)
    return _mm(name, br, wb, _sds((T, NB * N_DEV * n), out_dtype), (T // tm, NB),
               pl.BlockSpec((tm, W), lambda i, b: (i, b)),
               pl.BlockSpec((N_DEV, None, W, n), lambda i, b: (0, b, 0, 0)),
               pl.BlockSpec((tm, N_DEV * n), lambda i, b: (i, b)),
               lambda a_ref, w_ref: _dot(a_ref[...], _branch_weight(w_ref), _NN))


def _branch_nt(name, dproj, wb, out_dtype):
    T = dproj.shape[0]
    _, NB, W, n = wb.shape
    tm = min(T, 1024)
    return _mm(name, dproj, wb, _sds((T, NB * W), out_dtype), (T // tm, NB),
               pl.BlockSpec((tm, N_DEV * n), lambda i, b: (i, b)),
               pl.BlockSpec((N_DEV, None, W, n), lambda i, b: (0, b, 0, 0)),
               pl.BlockSpec((tm, W), lambda i, b: (i, b)),
               lambda a_ref, w_ref: _dot(a_ref[...], _branch_weight(w_ref), _NT))


def _branch_tn(name, br, dproj, n, out_dtype):
    T = br.shape[0]
    NB = 4
    W = br.shape[1] // NB
    tt = min(T, 2048)

    def store(o_ref, acc):
        for d in range(N_DEV):
            o_ref[d] = acc[:, d * n:(d + 1) * n].astype(o_ref.dtype)

    return _mm(name, br, dproj, _sds((N_DEV, NB, W, n), out_dtype), (NB, T // tt),
               pl.BlockSpec((tt, W), lambda b, t: (t, b)),
               pl.BlockSpec((tt, N_DEV * n), lambda b, t: (t, b)),
               pl.BlockSpec((N_DEV, None, W, n), lambda b, t: (0, b, 0, 0)),
               lambda a_ref, w_ref: _dot(a_ref[...], w_ref[...], _TN), red_axis=1,
               acc_shape=(W, N_DEV * n), store=store)


class _Att:
    def __init__(self, L, R, H, G, qcol, kcol, vcol, qstr, kstr, vstr, mode, reach=0, pad=0, rows=0, tile=ATT_TILE):
        self.L, self.R, self.H, self.G = L, R, H, G
        self.qcol, self.kcol, self.vcol = qcol, kcol, vcol
        self.qstr, self.kstr, self.vstr = qstr, kstr, vstr
        self.mode, self.reach, self.rows = mode, reach, rows
        self.t = min(tile, L)
        self.nb = L // self.t
        self.tc = min(ATT_TILE, L)
        self.nc = L // self.tc
        self.pieces = mode != "dense" and self.nb > 1
        self.pad = pad if self.pieces else 0
        self.span = self.t + 2 * self.pad
        self.ppb = self.t // pad if self.pieces else 0

    def qc(self, r, h):
        return r * self.qstr + self.qcol + h

    def kc(self, r, h):
        return r * self.kstr + self.kcol + h // self.G

    def vc(self, r, h):
        return r * self.vstr + self.vcol + h // self.G

    def oc(self, r, h):
        return r * self.H + h

    def mask(self, qpos, kpos, spos):
        if self.mode == "band":
            ok = jnp.abs(qpos - kpos) <= self.reach
        else:
            rq, cq = qpos >> 6, qpos & (GRID_W - 1)
            rk, ck = kpos >> 6, kpos & (GRID_W - 1)
            rs = jnp.clip(rq - NA_ROWS // 2, 0, self.rows - NA_ROWS)
            cs = jnp.clip(cq - NA_COLS // 2, 0, GRID_W - NA_COLS)
            ok = (rk >= rs) & (rk < rs + NA_ROWS) & (ck >= cs) & (ck < cs + NA_COLS)
        if self.pieces:
            ok = ok & (spos >= 0) & (spos < self.L)
        return ok

    def span_specs(self, col):
        t, pad, ppb = self.t, self.pad, self.ppb
        cur = pl.BlockSpec((t, HD), lambda r, h, i: (i, col(r, h)))
        if not self.pieces:
            return [cur]
        last = self.L // pad - 1
        prev = pl.BlockSpec((pad, HD), lambda r, h, i: (jnp.maximum(i * ppb - 1, 0), col(r, h)))
        nxt = pl.BlockSpec((pad, HD), lambda r, h, i: (jnp.minimum((i + 1) * ppb, last), col(r, h)))
        return [prev, cur, nxt]

    def positions(self, i):
        own = i * self.t + lax.broadcasted_iota(jnp.int32, (self.t, 1), 0)
        spn = i * self.t - self.pad + lax.broadcasted_iota(jnp.int32, (1, self.span), 1)
        return own, spn


def _cat(refs):
    return refs[0][...] if len(refs) == 1 else jnp.concatenate([r[...] for r in refs], axis=0)


def _with_scores(att, raw, i, own_is_query, bias_ref, itab_ref, rest):
    def edge():
        s = raw if bias_ref is None else raw + bias_ref[...]
        own, spn = att.positions(i)
        qpos, kpos = (own, spn) if own_is_query else (spn, own)
        rest(jnp.where(att.mask(qpos, kpos, spn), s, NEG_INF))

    if itab_ref is None:
        edge()
        return
    inner = (i >= 1) & (i <= att.nb - 2)
    pl.when(inner)(lambda: rest(raw + itab_ref[...]))
    pl.when(jnp.logical_not(inner))(edge)


def _tab_spec(att, tab):
    if tab.shape[0] == 1:
        return pl.BlockSpec((None, att.t, att.span), lambda r, h, i: (0, 0, 0))
    return pl.BlockSpec((None, att.t, att.span), lambda r, h, i: (h, 0, 0))


def _band_fwd(name, att, q, k, v, sink=None, bias=None, itab=None):
    t, ns = att.t, (3 if att.pieces else 1)

    def body(*refs):
        q_ref, k_refs, v_refs = refs[0], refs[1:1 + ns], refs[1 + ns:1 + 2 * ns]
        pos = 1 + 2 * ns
        sink_ref = bias_ref = itab_ref = None
        if sink is not None:
            sink_ref = refs[pos]
            pos += 1
        if bias is not None:
            bias_ref = refs[pos]
            pos += 1
        if itab is not None:
            itab_ref = refs[pos]
            pos += 1
        o_ref, lse_ref = refs[pos:]
        ks, vs = _cat(k_refs), _cat(v_refs)

        def rest(s):
            m = jnp.max(s, axis=1, keepdims=True)
            if sink_ref is not None:
                sk = sink_ref[...][:, :1] * LOG2E
                m = jnp.maximum(m, sk)
            p = jnp.exp2(s - m)
            den = jnp.sum(p, axis=1, keepdims=True)
            if sink_ref is not None:
                den = den + jnp.exp2(sk - m)
            o = jnp.dot(p.astype(BF16), vs, preferred_element_type=F32)
            o_ref[...] = (o / den).astype(o_ref.dtype)
            lse_ref[...] = jnp.broadcast_to((m + jnp.log2(den)) * LN2, (t, HD))

        _with_scores(att, _dot(q_ref[...], ks, _NT), pl.program_id(2), True, bias_ref, itab_ref, rest)

    in_specs = [pl.BlockSpec((t, HD), lambda r, h, i: (i, att.qc(r, h)))] + att.span_specs(att.kc) + att.span_specs(att.vc)
    args = [q] + [k] * ns + [v] * ns
    if sink is not None:
        in_specs.append(pl.BlockSpec((None, 1, HD), lambda r, h, i: (h, 0, 0)))
        args.append(sink)
    for tab in (bias, itab):
        if tab is not None:
            in_specs.append(_tab_spec(att, tab))
            args.append(tab)
    o_spec = pl.BlockSpec((t, HD), lambda r, h, i: (i, att.oc(r, h)))
    cols = att.R * att.H * HD
    return _pc(body, name=name, grid=(att.R, att.H, att.nb),
               out_shape=[_sds((att.L, cols), BF16), _sds((att.L, cols), F32)],
               in_specs=in_specs, out_specs=[o_spec, o_spec],
               compiler_params=_cparams(("parallel", "parallel", "parallel")))(*args)


def _band_dq(name, att, q, k, v, do, lse, delta, bias=None, itab=None):
    t, ns = att.t, (3 if att.pieces else 1)

    def body(*refs):
        q_ref, k_refs, v_refs = refs[0], refs[1:1 + ns], refs[1 + ns:1 + 2 * ns]
        do_ref, lse_ref, dl_ref = refs[1 + 2 * ns:4 + 2 * ns]
        pos = 4 + 2 * ns
        bias_ref = itab_ref = dtab_ref = None
        if bias is not None:
            bias_ref = refs[pos]
            pos += 1
        if itab is not None:
            itab_ref = refs[pos]
            pos += 1
        dq_ref = refs[pos]
        if bias is not None:
            dtab_ref = refs[pos + 1]
        i = pl.program_id(2)
        ks, vs = _cat(k_refs), _cat(v_refs)

        def rest(s):
            p = jnp.exp2(s - lse_ref[...][:, :1] * LOG2E)
            dp = _dot(do_ref[...], vs, _NT)
            ds = p * (dp - dl_ref[...][:, :1])
            if dtab_ref is not None:
                @pl.when(i == 0)
                def _():
                    dtab_ref[...] = ds

                @pl.when(i > 0)
                def _():
                    dtab_ref[...] += ds
            dq_ref[...] = jnp.dot(ds.astype(BF16), ks, preferred_element_type=F32) * SCALE

        _with_scores(att, _dot(q_ref[...], ks, _NT), i, True, bias_ref, itab_ref, rest)

    row = pl.BlockSpec((t, HD), lambda r, h, i: (i, att.oc(r, h)))
    in_specs = [pl.BlockSpec((t, HD), lambda r, h, i: (i, att.qc(r, h)))] + att.span_specs(att.kc) + att.span_specs(att.vc)
    in_specs += [row, row, row]
    args = [q] + [k] * ns + [v] * ns + [do, lse, delta]
    for tab in (bias, itab):
        if tab is not None:
            in_specs.append(_tab_spec(att, tab))
            args.append(tab)
    out_shape = [_sds((att.L, att.R * att.H * HD), F32)]
    out_specs = [row]
    sem = ("parallel", "parallel", "parallel")
    if bias is not None:
        out_shape.append(_sds((att.H, t, att.span), F32))
        out_specs.append(_tab_spec(att, bias))
        sem = ("parallel", "parallel", "arbitrary")
    return _pc(body, name=name, grid=(att.R, att.H, att.nb), out_shape=out_shape, in_specs=in_specs,
               out_specs=out_specs, compiler_params=_cparams(sem))(*args)


def _band_dkv(name, att, q, k, v, do, lse_sp, dl_sp, bias_t=None, itab_t=None):
    t, ns = att.t, (3 if att.pieces else 1)

    def body(*refs):
        k_ref, v_ref = refs[:2]
        q_refs, do_refs = refs[2:2 + ns], refs[2 + ns:2 + 2 * ns]
        lse_ref, dl_ref = refs[2 + 2 * ns:4 + 2 * ns]
        pos = 4 + 2 * ns
        bias_ref = itab_ref = None
        if bias_t is not None:
            bias_ref = refs[pos]
            pos += 1
        if itab_t is not None:
            itab_ref = refs[pos]
            pos += 1
        dk_ref, dv_ref = refs[pos:]
        qs, dos = _cat(q_refs), _cat(do_refs)

        def rest(st):
            pt = jnp.exp2(st - lse_ref[...] * LOG2E)
            dv_ref[...] = jnp.dot(pt.astype(BF16), dos, preferred_element_type=F32)
            dpt = _dot(v_ref[...], dos, _NT)
            dst = pt * (dpt - dl_ref[...])
            dk_ref[...] = jnp.dot(dst.astype(BF16), qs, preferred_element_type=F32) * LN2

        _with_scores(att, _dot(k_ref[...], qs, _NT), pl.program_id(2), False, bias_ref, itab_ref, rest)

    stat = pl.BlockSpec((None, None, 1, att.span), lambda r, h, i: (r * att.H + h, i, 0, 0))
    in_specs = [pl.BlockSpec((t, HD), lambda r, h, i: (i, att.kc(r, h))),
                pl.BlockSpec((t, HD), lambda r, h, i: (i, att.vc(r, h)))]
    in_specs += att.span_specs(att.qc) + att.span_specs(att.oc) + [stat, stat]
    args = [k, v] + [q] * ns + [do] * ns + [lse_sp, dl_sp]
    for tab in (bias_t, itab_t):
        if tab is not None:
            in_specs.append(_tab_spec(att, tab))
            args.append(tab)
    o_spec = pl.BlockSpec((t, HD), lambda r, h, i: (i, att.oc(r, h)))
    cols = att.R * att.H * HD
    return _pc(body, name=name, grid=(att.R, att.H, att.nb), out_shape=[_sds((att.L, cols), F32)] * 2,
               in_specs=in_specs, out_specs=[o_spec, o_spec],
               compiler_params=_cparams(("parallel", "parallel", "parallel")))(*args)


def _dense_fwd(name, att, q, k, v, rider=None):
    t, L, nb = att.t, att.L, att.nb
    grid = (att.H, nb)
    nr = rider.n if rider is not None else 0

    def body(*refs):
        q_ref, k_ref, v_ref = refs[:3]
        o_ref, lse_ref = refs[3 + nr:5 + nr]
        comm = (refs[3:3 + nr], refs[5 + nr:5 + 2 * nr], refs[5 + 2 * nr:])
        if rider is not None:
            rider.start(*comm, grid)
        qq = q_ref[...]

        def step(j, carry):
            m, den, acc = carry
            rows = pl.ds(pl.multiple_of(j * att.tc, att.tc), att.tc)
            s = lax.dot_general(qq, k_ref[rows, :], _NT, preferred_element_type=F32)
            m_new = jnp.maximum(m, jnp.max(s, axis=1, keepdims=True))
            alpha = jnp.exp2(m - m_new)
            p = jnp.exp2(s - m_new)
            den = alpha * den + jnp.sum(p, axis=1, keepdims=True)
            acc = alpha * acc + jnp.dot(p.astype(BF16), v_ref[rows, :], preferred_element_type=F32)
            return m_new, den, acc

        init = (jnp.full((t, 1), NEG_INF, F32), jnp.zeros((t, 1), F32), jnp.zeros((t, HD), F32))
        m, den, acc = lax.fori_loop(0, att.nc, step, init, unroll=2 if att.nc % 2 == 0 else 1)
        o_ref[...] = (acc / den).astype(o_ref.dtype)
        lse_ref[...] = jnp.broadcast_to((m + jnp.log2(den)) * LN2, (t, HD))
        if rider is not None:
            rider.finish(*comm, grid)

    o_spec = pl.BlockSpec((t, HD), lambda h, i: (i, att.oc(0, h)))
    r_in, r_out, r_outspec, r_sems = rider.specs() if rider is not None else ([], [], [], [])
    sem = ("arbitrary", "arbitrary") if rider is not None else ("parallel", "parallel")
    return _pc(body, name=name, grid=grid,
               out_shape=[_sds((L, att.H * HD), BF16), _sds((L, att.H * HD), F32)] + r_out,
               in_specs=[pl.BlockSpec((t, HD), lambda h, i: (i, att.qc(0, h))),
                         pl.BlockSpec((L, HD), lambda h, i: (0, att.kc(0, h))),
                         pl.BlockSpec((L, HD), lambda h, i: (0, att.vc(0, h)))] + r_in,
               out_specs=[o_spec, o_spec] + r_outspec, scratch_shapes=r_sems,
               compiler_params=_cparams(sem))(q, k, v, *(rider.ins if rider is not None else []))


def _dense_dq(name, att, q, k, v, do, lse, delta, rider=None):
    t, L, nb = att.t, att.L, att.nb
    grid = (att.H, nb)
    nr = rider.n if rider is not None else 0

    def body(*refs):
        q_ref, k_ref, v_ref, do_ref, lse_ref, dl_ref = refs[:6]
        dq_ref = refs[6 + nr]
        comm = (refs[6:6 + nr], refs[7 + nr:7 + 2 * nr], refs[7 + 2 * nr:])
        if rider is not None:
            rider.start(*comm, grid)
        qq, dd = q_ref[...], do_ref[...]
        lse2 = lse_ref[...][:, :1] * LOG2E
        dl = dl_ref[...][:, :1]

        def step(j, acc):
            rows = pl.ds(pl.multiple_of(j * att.tc, att.tc), att.tc)
            kk = k_ref[rows, :]
            s = lax.dot_general(qq, kk, _NT, preferred_element_type=F32)
            p = jnp.exp2(s - lse2)
            dp = lax.dot_general(dd, v_ref[rows, :], _NT, preferred_element_type=F32)
            ds = p * (dp - dl)
            return acc + jnp.dot(ds.astype(BF16), kk, preferred_element_type=F32)

        acc = lax.fori_loop(0, att.nc, step, jnp.zeros((t, HD), F32), unroll=2 if att.nc % 2 == 0 else 1)
        dq_ref[...] = acc * SCALE
        if rider is not None:
            rider.finish(*comm, grid)

    row = pl.BlockSpec((t, HD), lambda h, i: (i, att.oc(0, h)))
    r_in, r_out, r_outspec, r_sems = rider.specs() if rider is not None else ([], [], [], [])
    sem = ("arbitrary", "arbitrary") if rider is not None else ("parallel", "parallel")
    res = _pc(body, name=name, grid=grid, out_shape=[_sds((L, att.H * HD), F32)] + r_out,
              in_specs=[pl.BlockSpec((t, HD), lambda h, i: (i, att.qc(0, h))),
                        pl.BlockSpec((L, HD), lambda h, i: (0, att.kc(0, h))),
                        pl.BlockSpec((L, HD), lambda h, i: (0, att.vc(0, h))), row, row, row] + r_in,
              out_specs=[row] + r_outspec, scratch_shapes=r_sems,
              compiler_params=_cparams(sem))(q, k, v, do, lse, delta, *(rider.ins if rider is not None else []))
    return res[0], list(res[1:])


def _dense_dkv(name, att, q, k, v, do, lse_rows, dl_rows, rider=None):
    t, L, nb = att.t, att.L, att.nb
    grid = (att.H, nb)
    nr = rider.n if rider is not None else 0

    def body(*refs):
        k_ref, v_ref, q_ref, do_ref, lse_ref, dl_ref = refs[:6]
        dk_ref, dv_ref = refs[6 + nr:8 + nr]
        comm = (refs[6:6 + nr], refs[8 + nr:8 + 2 * nr], refs[8 + 2 * nr:])
        if rider is not None:
            rider.start(*comm, grid)
        kk, vv = k_ref[...], v_ref[...]

        def step(i, carry):
            dk, dv = carry
            rows = pl.ds(pl.multiple_of(i * att.tc, att.tc), att.tc)
            qq, dd = q_ref[rows, :], do_ref[rows, :]
            st = lax.dot_general(kk, qq, _NT, preferred_element_type=F32)
            pt = jnp.exp2(st - lse_ref[i] * LOG2E)
            dv = dv + jnp.dot(pt.astype(BF16), dd, preferred_element_type=F32)
            dpt = lax.dot_general(vv, dd, _NT, preferred_element_type=F32)
            dst = pt * (dpt - dl_ref[i])
            dk = dk + jnp.dot(dst.astype(BF16), qq, preferred_element_type=F32)
            return dk, dv

        zero = jnp.zeros((t, HD), F32)
        dk, dv = lax.fori_loop(0, att.nc, step, (zero, zero), unroll=2 if att.nc % 2 == 0 else 1)
        dk_ref[...] = dk * LN2
        dv_ref[...] = dv
        if rider is not None:
            rider.finish(*comm, grid)

    stat = pl.BlockSpec((None, att.nc, 1, att.tc), lambda h, j: (h, 0, 0, 0))
    o_spec = pl.BlockSpec((t, HD), lambda h, j: (j, att.oc(0, h)))
    r_in, r_out, r_outspec, r_sems = rider.specs() if rider is not None else ([], [], [], [])
    sem = ("arbitrary", "arbitrary") if rider is not None else ("parallel", "parallel")
    return _pc(body, name=name, grid=grid, out_shape=[_sds((L, att.H * HD), F32)] * 2 + r_out,
               in_specs=[pl.BlockSpec((t, HD), lambda h, j: (j, att.kc(0, h))),
                         pl.BlockSpec((t, HD), lambda h, j: (j, att.vc(0, h))),
                         pl.BlockSpec((L, HD), lambda h, j: (0, att.qc(0, h))),
                         pl.BlockSpec((L, HD), lambda h, j: (0, att.oc(0, h))), stat, stat] + r_in,
               out_specs=[o_spec, o_spec] + r_outspec, scratch_shapes=r_sems,
               compiler_params=_cparams(sem))(k, v, q, do, lse_rows, dl_rows, *(rider.ins if rider is not None else []))


def _stat_rows(att, col):
    RH = att.R * att.H
    rows = col.reshape(att.L, RH, HD)[:, :, 0].T
    if att.mode == "dense":
        return rows.reshape(RH, att.nc, 1, att.tc)
    if not att.pieces:
        return rows.reshape(RH, att.nb, 1, att.t)
    padded = jnp.pad(rows, ((0, 0), (att.pad, att.pad)))
    return jnp.stack([padded[:, j * att.t:j * att.t + att.span] for j in range(att.nb)], axis=1)[:, :, None, :]


def _interior_table(att, transposed, bias=None):
    if not (att.pieces and att.nb > 2):
        return None
    own = np.arange(att.t)[:, None]
    spn = np.arange(att.span)[None, :] - att.pad
    qpos, kpos = (spn, own) if transposed else (own, spn)
    if att.mode == "band":
        ok = np.abs(qpos - kpos) <= att.reach
    else:
        rq, cq = qpos // GRID_W, qpos % GRID_W
        rk, ck = kpos // GRID_W, kpos % GRID_W
        rs = rq - NA_ROWS // 2
        cs = np.clip(cq - NA_COLS // 2, 0, GRID_W - NA_COLS)
        ok = (rk >= rs) & (rk < rs + NA_ROWS) & (ck >= cs) & (ck < cs + NA_COLS)
    if bias is None:
        return jnp.asarray(np.where(ok, 0.0, NEG_INF)[None], F32)
    return jnp.where(jnp.asarray(ok)[None], bias, NEG_INF)


def _attn_fwd(name, att, q, k, v, sink=None, bias=None, rider=None):
    if att.mode == "dense":
        return _dense_fwd(name, att, q, k, v, rider=rider)
    return _band_fwd(name, att, q, k, v, sink=sink, bias=bias, itab=_interior_table(att, False, bias))


def _attn_bwd(name, att, q, k, v, do, lse, delta, bias=None, bias_t=None, riders=(None, None)):
    lse_r, dl_r = _stat_rows(att, lse), _stat_rows(att, delta)
    if att.mode == "dense":
        dq, got_q = _dense_dq(name + "_dq", att, q, k, v, do, lse, delta, rider=riders[0])
        res = _dense_dkv(name + "_dkv", att, q, k, v, do, lse_r, dl_r, rider=riders[1])
        return dq, res[0], res[1], (got_q, list(res[2:]))
    res = _band_dq(name + "_dq", att, q, k, v, do, lse, delta, bias=bias, itab=_interior_table(att, False, bias))
    dk, dv = _band_dkv(name + "_dkv", att, q, k, v, do, lse_r, dl_r, bias_t=bias_t,
                       itab_t=_interior_table(att, True, bias_t))
    return res[0], dk, dv, (res[1] if bias is not None else None)


def _row_call(name, body, T, tt, ins, outs, acc_outs=()):
    in_specs, args = [], []
    for item in ins:
        a = item[0]
        if item[1] is None:
            in_specs.append(pl.BlockSpec(a.shape, lambda i, nd=a.ndim: (0,) * nd))
        else:
            in_specs.append(pl.BlockSpec((tt, item[1]), lambda i, cb=item[2]: (i, cb)))
        args.append(a)
    out_shape = [_sds((T, c), d) for c, d in outs] + [_sds(s, F32) for s in acc_outs]
    out_specs = [pl.BlockSpec((tt, c), lambda i: (i, 0)) for c, _ in outs]
    out_specs += [pl.BlockSpec(s, lambda i, nd=len(s): (0,) * nd) for s in acc_outs]
    sem = ("arbitrary",) if acc_outs else ("parallel",)
    return _pc(body, name=name, grid=(T // tt,), out_shape=out_shape, in_specs=in_specs, out_specs=out_specs,
               compiler_params=_cparams(sem))(*args)


def _acc(ref, val):
    @pl.when(pl.program_id(0) == 0)
    def _():
        ref[...] = val

    @pl.when(pl.program_id(0) > 0)
    def _():
        ref[...] += val


def _prenorm_fwd(name, x, g, scale, shift):
    T, D = x.shape

    def body(x_ref, g_ref, sc_ref, sh_ref, h_ref):
        xf = x_ref[...]
        r = lax.rsqrt(jnp.mean(xf * xf, axis=1, keepdims=True) + EPS)
        h_ref[...] = ((xf * r) * g_ref[...] * (1.0 + sc_ref[...]) + sh_ref[...]).astype(BF16)

    return _row_call(name, body, T, min(T, 256), [(x, D, 0), (g, None), (scale, None), (shift, None)], [(D, BF16)])[0]


def _prenorm_bwd(name, x, dh, dxn, g, scale):
    T, D = x.shape

    def body(x_ref, dh_ref, dxn_ref, g_ref, sc_ref, dx_ref, dsh_ref, dsc_ref, dg_ref):
        xf, dh_, gg = x_ref[...], dh_ref[...], g_ref[...]
        r = lax.rsqrt(jnp.mean(xf * xf, axis=1, keepdims=True) + EPS)
        u = xf * r
        dn = dh_ * (1.0 + sc_ref[...])
        du = dn * gg
        dx_ref[...] = dxn_ref[...] + r * (du - u * jnp.mean(du * u, axis=1, keepdims=True))
        _acc(dsh_ref, jnp.sum(dh_, axis=0, keepdims=True))
        _acc(dsc_ref, jnp.sum(dh_ * (u * gg), axis=0, keepdims=True))
        _acc(dg_ref, jnp.sum(dn * u, axis=0, keepdims=True))

    return _row_call(name, body, T, min(T, 256), [(x, D, 0), (dh, D, 0), (dxn, D, 0), (g, None), (scale, None)],
                     [(D, F32)], [(1, D)] * 3)


def _resid_fwd(name, x, out, gate):
    T, D = x.shape

    def body(x_ref, o_ref, g_ref, y_ref):
        y_ref[...] = x_ref[...] + g_ref[...] * o_ref[...]

    return _row_call(name, body, T, min(T, 256), [(x, D, 0), (out, D, 0), (gate, None)], [(D, F32)])[0]


def _resid_bwd(name, dx, out, gate):
    T, D = dx.shape

    def body(dx_ref, o_ref, g_ref, do_ref, dg_ref):
        d = dx_ref[...]
        do_ref[...] = (d * g_ref[...]).astype(BF16)
        _acc(dg_ref, jnp.sum(d * o_ref[...], axis=0, keepdims=True))

    return _row_call(name, body, T, min(T, 256), [(dx, D, 0), (out, D, 0), (gate, None)], [(D, BF16)], [(1, D)])


def _final(name, x, tgt, g):
    T, D = x.shape

    def body(x_ref, t_ref, g_ref, dx_ref, loss_ref, dg_ref):
        xf, gg = x_ref[...], g_ref[...]
        r = lax.rsqrt(jnp.mean(xf * xf, axis=1, keepdims=True) + EPS)
        u = xf * r
        err = u * gg - t_ref[...]
        part = 0.5 * jnp.sum(jnp.mean(err * err, axis=1, keepdims=True), axis=0, keepdims=True)
        _acc(loss_ref, jnp.broadcast_to(part, (1, HD)))
        dy = err * (1.0 / D)
        _acc(dg_ref, jnp.sum(dy * u, axis=0, keepdims=True))
        du = dy * gg
        dx_ref[...] = r * (du - u * jnp.mean(du * u, axis=1, keepdims=True))

    return _row_call(name, body, T, min(T, 256), [(x, D, 0), (tgt, D, 0), (g, None)], [(D, F32)], [(1, HD), (1, D)])


def _roll_pair(x, shift):
    lanes = lax.broadcasted_iota(jnp.int32, x.shape, 1)
    r1 = pltpu.roll(x, shift, 1)
    r2 = pltpu.roll(x, HD - shift, 1)
    src = pltpu.roll(lanes, shift, 1)
    is_plus = src == ((lanes + shift) & (HD - 1))
    return jnp.where(is_plus, r1, r2), jnp.where(is_plus, r2, r1)


def _rope1(x, cos, ss):
    xp, _ = _roll_pair(x, 64)
    return x * cos + xp * ss


def _rope1_t(d, cos, ss):
    dp, _ = _roll_pair(d * ss, 64)
    return d * cos + dp


def _ropex(x, cos, sa, sb):
    xp, xm = _roll_pair(x, 32)
    return x * cos + xp * sa + xm * sb


def _ropex_t(d, cos, sa, sb):
    _, am = _roll_pair(d * sa, 32)
    bp, _ = _roll_pair(d * sb, 32)
    return d * cos + am + bp


def _qk_prep(name, qkvg, tabs, cqn, ckn):
    T = qkvg.shape[0]
    cos1, ss1, cosx, sax, sbx = tabs

    def body(aq, ak, bq, bk, bv, cq, ck, dq, c1, s1, cx, ax, bx, gq, gk, oaq, oak, obq, obk, obv, ocq, ock, odq):
        c1v, s1v = c1[...], s1[...]
        for src, dst, nh, mul in ((aq, oaq, 4, SCALE2), (ak, oak, 2, None), (bq, obq, 4, SCALE2), (bk, obk, 4, None)):
            for h in range(nh):
                sl = slice(h * HD, (h + 1) * HD)
                y = _rope1(src[:, sl].astype(F32), c1v, s1v)
                dst[:, sl] = (y if mul is None else y * mul).astype(BF16)
        obv[...] = bv[...]
        odq[...] = (dq[...].astype(F32) * SCALE2).astype(BF16)
        cxv, axv, bxv = cx[...], ax[...], bx[...]
        for src, dst, nh, gref, mul in ((cq, ocq, 4, gq, SCALE2), (ck, ock, 2, gk, None)):
            for h in range(nh):
                sl = slice(h * HD, (h + 1) * HD)
                xf = src[:, sl].astype(F32)
                r = lax.rsqrt(jnp.mean(xf * xf, axis=1, keepdims=True) + EPS)
                y = _ropex(xf * r * gref[...], cxv, axv, bxv)
                dst[:, sl] = (y if mul is None else y * mul).astype(BF16)

    ins = [(qkvg, 512, 0), (qkvg, 256, 2), (qkvg, 512, 3), (qkvg, 512, 4), (qkvg, 512, 5), (qkvg, 512, 7),
           (qkvg, 256, 16), (qkvg, 512, 10),
           (cos1, HD, 0), (ss1, HD, 0), (cosx, HD, 0), (sax, HD, 0), (sbx, HD, 0), (cqn, None), (ckn, None)]
    outs = [(512, BF16), (256, BF16), (512, BF16), (512, BF16), (512, BF16), (512, BF16), (256, BF16), (512, BF16)]
    return _row_call(name, body, T, min(T, 512), ins, outs)


def _silu_parts(g):
    sig = 1.0 / (1.0 + jnp.exp(-g))
    return g * sig, sig * (1.0 + g * (1.0 - sig))


def _mix_weights(l0, l1, l2):
    mx = jnp.maximum(jnp.maximum(l0, l1), l2)
    e0, e1, e2 = jnp.exp(l0 - mx), jnp.exp(l1 - mx), jnp.exp(l2 - mx)
    inv = 1.0 / (e0 + e1 + e2)
    return e0 * inv, e1 * inv, e2 * inv


def _gate_fwd(name, qkvg, oa, ob, lb, oc, od):
    T = qkvg.shape[0]

    def body(ga, gb, gc, gd, oa_r, ob0, ob1, ob2, lb0, lb1, lb2, oc_r, od_r, br):
        w0, w1, w2 = _mix_weights(lb0[...], lb1[...], lb2[...])
        yb = w0 * ob0[...].astype(F32) + w1 * ob1[...].astype(F32) + w2 * ob2[...].astype(F32)
        ys = (oa_r[...].astype(F32), yb, oc_r[...].astype(F32), od_r[...].astype(F32))
        for n, (y, g) in enumerate(zip(ys, (ga, gb, gc, gd))):
            act, _ = _silu_parts(g[...].astype(F32))
            br[:, n * 512:(n + 1) * 512] = (y * act).astype(BF16)

    ins = [(qkvg, 512, 2), (qkvg, 512, 6), (qkvg, 512, 9), (qkvg, 512, 13), (oa, 512, 0)]
    ins += [(o, 512, 0) for o in ob] + [(l_, 512, 0) for l_ in lb] + [(oc, 512, 0), (od, 512, 0)]
    return _row_call(name, body, T, min(T, 256), ins, [(2048, BF16)])[0]


def _head_rowsum(x):
    parts = []
    for h in range(x.shape[1] // HD):
        s = jnp.sum(x[:, h * HD:(h + 1) * HD], axis=1, keepdims=True)
        parts.append(jnp.broadcast_to(s, (x.shape[0], HD)))
    return jnp.concatenate(parts, axis=1)


def _gate_bwd(name, dbr, qkvg, oa, la, sink_row, ob, lb, oc, od):
    T = qkvg.shape[0]

    def body(dbr_r, ga, gb, gc, gd, oa_r, la_r, sk, ob0, ob1, ob2, lb0, lb1, lb2, oc_r, od_r,
             doa, dla, dob0, dob1, dob2, dlb0, dlb1, dlb2, doc, dlc, dod, dld, dg, dsk):
        def one(n, g_ref, y):
            act, dact = _silu_parts(g_ref[...].astype(F32))
            d = dbr_r[:, n * 512:(n + 1) * 512]
            dg[:, n * 512:(n + 1) * 512] = (d * y * dact).astype(BF16)
            return d * act

        ya = oa_r[...].astype(F32)
        dya = one(0, ga, ya)
        doa[...] = dya.astype(BF16)
        dl_a = _head_rowsum(dya * ya)
        dla[...] = dl_a
        _acc(dsk, -jnp.sum(jnp.exp(sk[...] - la_r[...]) * dl_a, axis=0, keepdims=True))

        w = _mix_weights(lb0[...], lb1[...], lb2[...])
        obs = (ob0[...].astype(F32), ob1[...].astype(F32), ob2[...].astype(F32))
        yb = w[0] * obs[0] + w[1] * obs[1] + w[2] * obs[2]
        dyb = one(1, gb, yb)
        rs = _head_rowsum(dyb * yb)
        for wp, do_ref, dl_ref in zip(w, (dob0, dob1, dob2), (dlb0, dlb1, dlb2)):
            do_ref[...] = (wp * dyb).astype(BF16)
            dl_ref[...] = wp * rs

        for n, g_ref, o_r, do_ref, dl_ref in ((2, gc, oc_r, doc, dlc), (3, gd, od_r, dod, dld)):
            y = o_r[...].astype(F32)
            dy = one(n, g_ref, y)
            do_ref[...] = dy.astype(BF16)
            dl_ref[...] = _head_rowsum(dy * y)

    ins = [(dbr, 2048, 0), (qkvg, 512, 2), (qkvg, 512, 6), (qkvg, 512, 9), (qkvg, 512, 13),
           (oa, 512, 0), (la, 512, 0), (sink_row, None)]
    ins += [(o, 512, 0) for o in ob] + [(l_, 512, 0) for l_ in lb] + [(oc, 512, 0), (od, 512, 0)]
    outs = [(512, BF16), (512, F32)] + [(512, BF16)] * 3 + [(512, F32)] * 3 + [(512, BF16), (512, F32)] * 2
    outs += [(2048, BF16)]
    return _row_call(name, body, T, min(T, 256), ins, outs, [(1, 512)])


def _merge_fwd(name, mgl, proj, D):
    T = mgl.shape[0]

    def body(m_ref, p_ref, o_ref):
        acc = None
        for n in range(4):
            sl = slice(n * D, (n + 1) * D)
            sig = 1.0 / (1.0 + jnp.exp(-m_ref[:, sl].astype(F32)))
            term = sig * p_ref[:, sl].astype(F32)
            acc = term if acc is None else acc + term
        o_ref[...] = acc.astype(BF16)

    return _row_call(name, body, T, min(T, 256), [(mgl, 4 * D, 0), (proj, 4 * D, 0)], [(D, BF16)])[0]


def _merge_bwd(name, dm, mgl, proj, D):
    T = mgl.shape[0]

    def body(d_ref, m_ref, p_ref, dp_ref, dl_ref):
        d = d_ref[...]
        for n in range(4):
            sl = slice(n * D, (n + 1) * D)
            sig = 1.0 / (1.0 + jnp.exp(-m_ref[:, sl].astype(F32)))
            dp_ref[:, sl] = (d * sig).astype(BF16)
            dl_ref[:, sl] = (d * p_ref[:, sl].astype(F32) * sig * (1.0 - sig)).astype(BF16)

    return _row_call(name, body, T, min(T, 128), [(dm, D, 0), (mgl, 4 * D, 0), (proj, 4 * D, 0)],
                     [(4 * D, BF16), (4 * D, BF16)])


def _qkvg_bwd(name, qkvg, tabs, cqn, ckn, dA, dB, dC, dD, dg):
    T = qkvg.shape[0]
    cos1, ss1, cosx, sax, sbx = tabs

    def body(cq_r, ck_r, c1, s1, cx, ax, bx, gq, gk,
             dqa, dka, dva, dqb0, dqb1, dqb2, dkb0, dkb1, dkb2, dvb0, dvb1, dvb2,
             dqc, dkc, dvc, dqd, dkd, dvd, dg_r, out, dgq, dgk):
        c1v, s1v = c1[...], s1[...]

        def put(col, val):
            out[:, col * HD:(col + 1) * HD] = val.astype(BF16)

        def pair(ref, kv):
            return ref[:, 2 * kv * HD:(2 * kv + 1) * HD] + ref[:, (2 * kv + 1) * HD:(2 * kv + 2) * HD]

        for h in range(4):
            sl = slice(h * HD, (h + 1) * HD)
            put(A_Q + h, _rope1_t(dqa[:, sl], c1v, s1v))
            put(B_Q + h, _rope1_t(dqb0[:, sl] + dqb1[:, sl] + dqb2[:, sl], c1v, s1v))
            put(B_K + h, _rope1_t(dkb0[:, sl] + dkb1[:, sl] + dkb2[:, sl], c1v, s1v))
            put(B_V + h, dvb0[:, sl] + dvb1[:, sl] + dvb2[:, sl])
            put(D_Q + h, dqd[:, sl])
            put(D_K + h, dkd[:, sl])
            put(D_V + h, dvd[:, sl])
        for kv in range(2):
            put(A_K + kv, _rope1_t(pair(dka, kv), c1v, s1v))
            put(A_V + kv, pair(dva, kv))
            put(C_V + kv, pair(dvc, kv))
        cxv, axv, bxv = cx[...], ax[...], bx[...]
        for src, dref, col, nh, gref, dgref in ((cq_r, dqc, C_Q, 4, gq, dgq), (ck_r, dkc, C_K, 2, gk, dgk)):
            gsum = None
            for h in range(nh):
                sl = slice(h * HD, (h + 1) * HD)
                xf = src[:, sl].astype(F32)
                r = lax.rsqrt(jnp.mean(xf * xf, axis=1, keepdims=True) + EPS)
                u = xf * r
                dy = _ropex_t(dref[:, sl] if nh == 4 else pair(dref, h), cxv, axv, bxv)
                du = dy * gref[...]
                put(col + h, r * (du - u * jnp.mean(du * u, axis=1, keepdims=True)))
                part = jnp.sum(dy * u, axis=0, keepdims=True)
                gsum = part if gsum is None else gsum + part
            _acc(dgref, gsum)
        for n, col in enumerate((A_G, B_G, C_G, D_G)):
            out[:, col * HD:(col + 4) * HD] = dg_r[:, n * 512:(n + 1) * 512]

    ins = [(qkvg, 512, 7), (qkvg, 256, 16), (cos1, HD, 0), (ss1, HD, 0), (cosx, HD, 0), (sax, HD, 0), (sbx, HD, 0),
           (cqn, None), (ckn, None)]
    ins += [(a, 512, 0) for a in dA]
    ins += [(a, 512, 0) for a in dB[0]] + [(a, 512, 0) for a in dB[1]] + [(a, 512, 0) for a in dB[2]]
    ins += [(a, 512, 0) for a in dC] + [(a, 512, 0) for a in dD] + [(dg, 2048, 0)]
    return _row_call(name, body, T, min(T, 256), ins, [(IN_HEADS * HD, BF16)], [(1, HD), (1, HD)])


def _ada_fwd(name, c_pad, w_ada):
    L, D, n = w_ada.shape
    P = c_pad.shape[0]

    def body(c_ref, w_ref, o_ref):
        cc = c_ref[...]
        cond = cc / (1.0 + jnp.exp(-cc))
        o_ref[...] = jnp.dot(cond.astype(BF16), w_ref[...].astype(BF16), preferred_element_type=F32)

    return _pc(body, name=name, grid=(L,), out_shape=_sds((L, P, n), F32),
               in_specs=[pl.BlockSpec((P, D), lambda l: (0, 0)), pl.BlockSpec((None, D, n), lambda l: (l, 0, 0))],
               out_specs=pl.BlockSpec((None, P, n), lambda l: (l, 0, 0)),
               compiler_params=_cparams(("parallel",)))(c_pad, w_ada)


def _ada_grad(name, c_pad, d_ada):
    L, P, n = d_ada.shape
    D = c_pad.shape[1]

    def body(c_ref, d_ref, o_ref):
        cc = c_ref[...]
        cond = cc / (1.0 + jnp.exp(-cc))
        o_ref[...] = lax.dot_general(cond.astype(BF16), d_ref[...].astype(BF16), _TN, preferred_element_type=F32)

    return _pc(body, name=name, grid=(L,), out_shape=_sds((L, D, n), F32),
               in_specs=[pl.BlockSpec((P, D), lambda l: (0, 0)), pl.BlockSpec((None, P, n), lambda l: (l, 0, 0))],
               out_specs=pl.BlockSpec((None, D, n), lambda l: (l, 0, 0)),
               compiler_params=_cparams(("parallel",)))(c_pad, d_ada)


def _adam_rows(rows, cols):
    cap = max(8, (2 ** 20) // (4 * cols))
    if rows <= cap:
        return rows
    tr = 8
    while tr * 2 <= cap and rows % (tr * 2) == 0:
        tr *= 2
    return tr


def _adam_update(g_ref, slots, w_ref, m_ref, v_ref, go_ref, d_ref, mo_ref, vo_ref):
    if slots:
        gg = g_ref[0].astype(F32)
        for k in range(1, slots):
            gg = gg + g_ref[k].astype(F32)
    else:
        gg = g_ref[...]
    c1 = 1.0 / (1.0 - ADAM_B1 ** ADAM_STEP)
    c2 = 1.0 / (1.0 - ADAM_B2 ** ADAM_STEP)
    mn = ADAM_B1 * m_ref[...] + (1.0 - ADAM_B1) * gg
    vn = ADAM_B2 * v_ref[...] + (1.0 - ADAM_B2) * (gg * gg)
    go_ref[...] = gg
    mo_ref[...] = mn
    vo_ref[...] = vn
    d_ref[...] = -ADAM_LR * ((mn * c1) / (jnp.sqrt(vn * c2) + ADAM_EPS) + ADAM_WD * w_ref[...])


def _adamw(name, w, m, v, g, slots):
    rows, cols = w.shape
    tr = _adam_rows(rows, cols)

    def body(w_ref, m_ref, v_ref, g_ref, *outs):
        _adam_update(g_ref, slots, w_ref, m_ref, v_ref, *outs)

    blk = pl.BlockSpec((tr, cols), lambda i: (i, 0))
    gspec = pl.BlockSpec((slots, tr, cols), lambda i: (0, i, 0)) if slots else blk
    return _pc(body, name=name, grid=(rows // tr,), out_shape=[_sds((rows, cols), F32)] * 4,
               in_specs=[blk, blk, blk, gspec], out_specs=[blk] * 4,
               compiler_params=_cparams(("parallel",)))(w, m, v, g)


def _adamw_layers(name, w, m, v, slot_list):
    nl = len(slot_list)
    slots, rows, cols = slot_list[0].shape
    tr = _adam_rows(rows, cols)
    nblk = rows // tr

    def body(w_ref, m_ref, v_ref, *rest):
        layer = pl.program_id(0)
        for ll in range(nl):
            @pl.when(layer == ll)
            def _(g_ref=rest[ll]):
                _adam_update(g_ref, slots, w_ref, m_ref, v_ref, *rest[nl:])

    def gspec(ll):
        return pl.BlockSpec((slots, tr, cols),
                            lambda l, i: (0, jnp.where(l == ll, i, jnp.where(l > ll, nblk - 1, 0)), 0))

    blk = pl.BlockSpec((tr, cols), lambda l, i: (l * nblk + i, 0))
    return _pc(body, name=name, grid=(nl, nblk), out_shape=[_sds((nl * rows, cols), F32)] * 4,
               in_specs=[blk, blk, blk] + [gspec(ll) for ll in range(nl)], out_specs=[blk] * 4,
               compiler_params=_cparams(("parallel", "parallel")))(w, m, v, *slot_list)


def _rope_tables(T):
    pos = np.arange(T)
    lane = np.arange(HD)
    inv = ROPE_THETA ** (-np.arange(0, HD, 2, dtype=np.float32) / HD)
    ang = pos.astype(np.float32)[:, None] * inv[None, :]
    ang = np.concatenate([ang, ang], axis=-1)
    cos1 = np.cos(ang)
    ss1 = np.sin(ang) * np.where(lane < HD // 2, -1.0, 1.0)[None, :]
    half = HD // 2
    invh = ROPE_THETA ** (-np.arange(0, half, 2, dtype=np.float32) / half)

    def tab(p):
        a = p.astype(np.float32)[:, None] * invh[None, :]
        return np.concatenate([a, a], axis=-1)

    angx = np.concatenate([tab(pos // GRID_W), tab(pos % GRID_W)], axis=-1)
    cosx, sinx = np.cos(angx), np.sin(angx)
    first = (lane % half) < half // 2
    sax = np.where(first[None, :], -sinx, 0.0)
    sbx = np.where(first[None, :], 0.0, sinx)
    return tuple(jnp.asarray(a, F32) for a in (cos1, ss1, cosx, sax, sbx))


def _na_selectors(att, transposed):
    own_r = np.arange(att.t // GRID_W)[:, None]
    span_r = np.arange(att.span // GRID_W)[None, :] - att.pad // GRID_W
    dr = own_r - span_r if transposed else span_r - own_r
    row_sel = (np.clip(dr, -(NA_ROWS - 1), NA_ROWS - 1)[..., None] + NA_ROWS - 1 == np.arange(2 * NA_ROWS - 1))
    col = np.arange(GRID_W)
    dc = col[:, None] - col[None, :] if transposed else col[None, :] - col[:, None]
    col_sel = (np.clip(dc, -(NA_COLS - 1), NA_COLS - 1)[..., None] + NA_COLS - 1 == np.arange(2 * NA_COLS - 1))
    return jnp.asarray(row_sel.astype(np.float32)), jnp.asarray(col_sel.astype(np.float32))


def _na_bias_table(att, rel_bias, transposed):
    row_sel, col_sel = _na_selectors(att, transposed)
    tab = jnp.einsum("xyi,hij,cdj->hxcyd", row_sel, rel_bias * LOG2E, col_sel, precision=lax.Precision.HIGHEST)
    return tab.reshape(rel_bias.shape[0], att.t, att.span)


def _na_bias_grad(att, dtab):
    row_sel, col_sel = _na_selectors(att, False)
    d5 = dtab.reshape(dtab.shape[0], att.t // GRID_W, GRID_W, att.span // GRID_W, GRID_W)
    return jnp.einsum("xyi,hxcyd,cdj->hij", row_sel, d5, col_sel, precision=lax.Precision.HIGHEST)


def _att_specs(T):
    specs = {
        "A": _Att(T, 1, 4, 2, 0, 0, A_V, 0, 0, 0, "band", reach=A_REACH, pad=128, tile=BAND_TILE),
        "C": _Att(T, 1, 4, 2, 0, 0, C_V, 0, 0, 0, "dense", tile=DENSE_TILE),
        "D": _Att(T, 1, 4, 1, 0, D_K, D_V, 0, 0, 0, "na", pad=256, rows=T // GRID_W),
    }
    for p, (window, dil) in enumerate(B_PATTERNS):
        specs["B%d" % p] = _Att(T // dil, dil, 4, 1, 0, 0, 0, 4, 4, 4, "band", reach=(window // 2) // dil, pad=64,
                                tile=BAND_TILE)
    return specs


def kernel(x, c, norm_g, w_ada, b_ada, w_in, a_sink, c_q_norm, c_k_norm, d_rel_bias, w_gate_merge, w_branch, w_out, final_g, loss_target, m_norm_g, m_w_ada, m_b_ada, m_w_in, m_a_sink, m_c_q_norm, m_c_k_norm, m_d_rel_bias, m_w_gate_merge, m_w_branch, m_w_out, m_final_g, v_norm_g, v_w_ada, v_b_ada, v_w_in, v_a_sink, v_c_q_norm, v_c_k_norm, v_d_rel_bias, v_w_gate_merge, v_w_branch, v_w_out, v_final_g):
    T, D = x.shape[1], x.shape[2]
    NL = norm_g.shape[0]
    x0 = x.reshape(T, D)
    tgt = loss_target.reshape(T, D)
    me = 4 * lax.axis_index("x") + 2 * lax.axis_index("y") + lax.axis_index("c")
    att = _att_specs(T)
    tabs = _rope_tables(T)
    n_ada = w_ada.shape[2]

    (c_all,) = _exchange("gather_c", [c], gather=True)
    c_pad = jnp.pad(c_all.reshape(N_DEV, D), ((0, HD - N_DEV), (0, 0)))
    ada_part = _ada_fwd("ada_fwd", c_pad, w_ada)[:, :N_DEV]
    (ada_all,) = _exchange("gather_ada", [ada_part], gather=True)
    ada_mine = lax.dynamic_index_in_dim(ada_all, me, axis=2, keepdims=False)
    ada = jnp.transpose(ada_mine, (1, 0, 2)).reshape(NL, N_DEV * n_ada) + b_ada
    shift, scale, gate = ada[:, :D], ada[:, D:2 * D], ada[:, 2 * D:]

    shards = [[w[l].astype(BF16) for w in (w_in, w_gate_merge, w_branch, w_out)] for l in range(NL)]
    weights = [[None] * 4 for _ in range(NL)]
    weights[0][0] = _exchange("gather_w0", shards[0][:1], gather=True)[0]

    def views(a_, arrs):
        return [a.reshape(a_.L, -1) for a in arrs]

    saved = []
    xl = x0
    for l in range(NL):
        sc_l, sh_l, gt_l = scale[l][None], shift[l][None], gate[l][None]
        ng_l = norm_g[l][None]
        cqn, ckn = c_q_norm[l][None], c_k_norm[l][None]
        sink3 = jnp.broadcast_to(a_sink[l][:, None, None], (4, 1, HD))
        btab = _na_bias_table(att["D"], d_rel_bias[l], False)

        h = _prenorm_fwd("prenorm_fwd", xl, ng_l, sc_l, sh_l)
        if l == 0:
            qkvg, *weights[0][1:] = _mm_nn("mm_in_gather", h, weights[0][0], BF16, rider=_Rider(shards[0][1:], True))
        else:
            qkvg = _mm_nn("mm_in", h, weights[l][0], BF16)
        if l + 1 < NL:
            mgl, *weights[l + 1][2:] = _mm_nn("mm_gm_gather", h, weights[l][1], BF16, rider=_Rider(shards[l + 1][2:], True))
        else:
            mgl = _mm_nn("mm_gm", h, weights[l][1], BF16)
        Wb, Wo = weights[l][2], weights[l][3].reshape(1, D, D)
        qa, ka, qb, kb, vb, qc, kc, qd = _qk_prep("qk_prep", qkvg, tabs, cqn, ckn)
        oa, la = _attn_fwd("attn_a_fwd", att["A"], qa, ka, qkvg, sink=sink3)
        ob, lb = [], []
        for p in range(len(B_PATTERNS)):
            a_ = att["B%d" % p]
            o_, l_ = _attn_fwd("attn_b%d_fwd" % p, a_, *views(a_, (qb, kb, vb)))
            ob.append(o_.reshape(T, 512))
            lb.append(l_.reshape(T, 512))
        if l + 1 < NL:
            oc, lc, *weights[l + 1][:2] = _attn_fwd("attn_c_fwd_gather", att["C"], qc, kc, qkvg,
                                                    rider=_Rider(shards[l + 1][:2], True))
        else:
            oc, lc = _attn_fwd("attn_c_fwd", att["C"], qc, kc, qkvg)
        od, ld = _attn_fwd("attn_d_fwd", att["D"], qd, qkvg, qkvg, bias=btab)
        br = _gate_fwd("gate_fwd", qkvg, oa, ob, lb, oc, od)
        proj = _branch_nn("mm_branch", br, Wb, BF16)
        merged = _merge_fwd("merge_fwd", mgl, proj, D)
        out = _mm_nn("mm_out", merged, Wo, F32)
        xn = _resid_fwd("resid_fwd", xl, out, gt_l)
        saved.append(dict(x=xl, h=h, qkvg=qkvg, mgl=mgl, qa=qa, ka=ka, qb=qb, kb=kb, vb=vb, qc=qc, kc=kc, qd=qd, oa=oa,
                          la=la, ob=ob, lb=lb, oc=oc, lc=lc, od=od, ld=ld, br=br, proj=proj, merged=merged, out=out,
                          btab=btab))
        xl = xn

    dx, loss_part, d_final_g = _final("final_loss", xl, tgt, final_g[None])

    grads, slots = [None] * NL, [None] * NL
    d_norm_g, d_ada, d_sink, d_cq, d_ck, d_bias = [None] * NL, [None] * NL, [None] * NL, [None] * NL, [None] * NL, [None] * NL
    for l in reversed(range(NL)):
        S = saved[l]
        Wi, Wg, Wb, Wo = weights[l]
        Wo = Wo.reshape(1, D, D)
        sc_l, gt_l, ng_l = scale[l][None], gate[l][None], norm_g[l][None]
        cqn, ckn = c_q_norm[l][None], c_k_norm[l][None]
        sink_row = jnp.repeat(a_sink[l], HD)[None]
        qkvg = S["qkvg"]

        dout, dgate = _resid_bwd("resid_bwd", dx, S["out"], gt_l)
        g_out = _mm_tn("mm_out_dw", S["merged"], dout, 1, BF16).reshape(N_DEV, D // N_DEV, D)
        dm = _mm_nt("mm_out_dx", dout, Wo, F32)
        dproj, dmgl = _merge_bwd("merge_bwd", dm, S["mgl"], S["proj"], D)
        g_br = _branch_tn("mm_branch_dw", S["br"], dproj, D // N_DEV, BF16)
        dbr = _branch_nt("mm_branch_dx", dproj, Wb, F32)
        (doa, dla, dob0, dob1, dob2, dlb0, dlb1, dlb2, doc, dlc, dod, dld, dg, dsk) = _gate_bwd(
            "gate_bwd", dbr, qkvg, S["oa"], S["la"], sink_row, S["ob"], S["lb"], S["oc"], S["od"])
        d_sink[l] = dsk.reshape(4, HD)[:, 0]

        dqa, dka, dva, _ = _attn_bwd("attn_a", att["A"], S["qa"], S["ka"], qkvg, doa, S["la"], dla)
        dqb, dkb, dvb = [], [], []
        for p, (dob, dlb) in enumerate(((dob0, dlb0), (dob1, dlb1), (dob2, dlb2))):
            a_ = att["B%d" % p]
            dq_, dk_, dv_, _ = _attn_bwd("attn_b%d" % p, a_, *views(a_, (S["qb"], S["kb"], S["vb"], dob, S["lb"][p], dlb)))
            dqb.append(dq_.reshape(T, 512))
            dkb.append(dk_.reshape(T, 512))
            dvb.append(dv_.reshape(T, 512))
        if l + 1 < NL:
            dqc, dkc, dvc, (s_small, s_big) = _attn_bwd(
                "attn_c_scatter", att["C"], S["qc"], S["kc"], qkvg, doc, S["lc"], dlc,
                riders=(_Rider(grads[l + 1][2:], False), _Rider(grads[l + 1][:2], False)))
            slots[l + 1] = s_big + s_small
        else:
            dqc, dkc, dvc, _ = _attn_bwd("attn_c", att["C"], S["qc"], S["kc"], qkvg, doc, S["lc"], dlc)
        dqd, dkd, dvd, dtab = _attn_bwd("attn_d", att["D"], S["qd"], qkvg, qkvg, dod, S["ld"], dld, bias=S["btab"],
                                        bias_t=_na_bias_table(att["D"], d_rel_bias[l], True))
        d_bias[l] = _na_bias_grad(att["D"], dtab)

        dqkvg, dcq, dck = _qkvg_bwd("qkvg_bwd", qkvg, tabs, cqn, ckn, (dqa, dka, dva), (dqb, dkb, dvb),
                                    (dqc, dkc, dvc), (dqd, dkd, dvd), dg)
        d_cq[l], d_ck[l] = dcq[0], dck[0]
        g_in = _mm_tn("mm_in_dw", S["h"], dqkvg, N_DEV, BF16)
        g_gm = _mm_tn("mm_gm_dw", S["h"], dmgl, N_DEV, BF16)
        grads[l] = [g_in, g_gm, g_br, g_out]
        if l == 0:
            dh, s_in, s_br, s_out = _mm_nt("mm_in_dx_scatter", dqkvg, Wi, F32, rider=_Rider([g_in, g_br, g_out], False))
            dh, s_gm = _mm_nt("mm_gm_dx_scatter", dmgl, Wg, F32, add=dh, rider=_Rider([g_gm], False))
            slots[0] = [s_in, s_gm, s_br, s_out]
        else:
            dh = _mm_nt("mm_in_dx", dqkvg, Wi, F32)
            dh = _mm_nt("mm_gm_dx", dmgl, Wg, F32, add=dh)
        dx, dshift, dscale, dng = _prenorm_bwd("prenorm_bwd", S["x"], dh, dx, ng_l, sc_l)
        d_norm_g[l] = dng[0]
        d_ada[l] = jnp.concatenate([dshift[0], dscale[0], dgate[0]])

    grad_x = dx.reshape(1, T, D)

    def big(name, w, m, v, k):
        cols = w.shape[-1]
        res = _adamw_layers(name, w.reshape(-1, cols), m.reshape(-1, cols), v.reshape(-1, cols),
                            [slots[l][k].reshape(N_DEV, -1, cols) for l in range(NL)])
        return [r.reshape(w.shape) for r in res]

    r_in = big("adamw_in", w_in, m_w_in, v_w_in, 0)
    r_gm = big("adamw_gm", w_gate_merge, m_w_gate_merge, v_w_gate_merge, 1)
    r_br = big("adamw_branch", w_branch, m_w_branch, v_w_branch, 2)
    r_out = big("adamw_out", w_out, m_w_out, v_w_out, 3)

    small_w = [norm_g, b_ada, a_sink, c_q_norm, c_k_norm, d_rel_bias, final_g]
    small_m = [m_norm_g, m_b_ada, m_a_sink, m_c_q_norm, m_c_k_norm, m_d_rel_bias, m_final_g]
    small_v = [v_norm_g, v_b_ada, v_a_sink, v_c_q_norm, v_c_k_norm, v_d_rel_bias, v_final_g]
    small_g = [jnp.stack(d_norm_g), jnp.stack(d_ada), jnp.stack(d_sink), jnp.stack(d_cq), jnp.stack(d_ck),
               jnp.stack(d_bias), d_final_g[0]]
    sizes = [int(np.prod(w.shape)) for w in small_w]
    total = sum(sizes) + HD
    rows = -(-total // (8 * HD)) * 8

    def pack(parts, extra):
        flat = jnp.concatenate([p.reshape(-1).astype(F32) for p in parts] + [extra])
        return jnp.pad(flat, (0, rows * HD - flat.shape[0])).reshape(rows, HD)

    zeros = jnp.zeros((HD,), F32)
    (g_slots,) = _exchange("gather_small", [pack(small_g, loss_part[0])], gather=True)
    gs, ds, ms, vs = _adamw("adamw_small", pack(small_w, zeros), pack(small_m, zeros), pack(small_v, zeros), g_slots, N_DEV)

    def unpack(flat2d):
        flat = flat2d.reshape(-1)
        res, o = [], 0
        for w, n in zip(small_w, sizes):
            res.append(flat[o:o + n].reshape(w.shape))
            o += n
        return res, flat[o]

    sg, loss = unpack(gs)
    sd, _ = unpack(ds)
    sm, _ = unpack(ms)
    sv, _ = unpack(vs)

    o_b = sizes[0]
    d_ada_all = g_slots.reshape(N_DEV, -1)[:, o_b:o_b + sizes[1]].reshape(N_DEV, NL, N_DEV, n_ada)
    d_ada_mine = jnp.transpose(lax.dynamic_index_in_dim(d_ada_all, me, axis=2, keepdims=False), (1, 0, 2))
    g_ada = _ada_grad("ada_grad", c_pad, jnp.pad(d_ada_mine, ((0, 0), (0, HD - N_DEV), (0, 0))))
    r_ada = _adamw("adamw_ada", w_ada.reshape(-1, n_ada), m_w_ada.reshape(-1, n_ada), v_w_ada.reshape(-1, n_ada),
                   g_ada.reshape(-1, n_ada), 0)
    r_ada = [r.reshape(w_ada.shape) for r in r_ada]

    def kind(k):
        sm_ = (sg, sd, sm, sv)[k]
        return [sm_[0], r_ada[k], sm_[1], r_in[k], sm_[2], sm_[3], sm_[4], sm_[5], r_gm[k], r_br[k], r_out[k], sm_[6]]

    return (loss, grad_x, *kind(0), *kind(1), *kind(2), *kind(3))
```

```python
import jax
import jax.numpy as jnp
import numpy as np
from jax import lax
from jax.experimental import pallas as pl
from jax.experimental.pallas import tpu as pltpu

F32 = jnp.float32
BF16 = jnp.bfloat16

N_DEV = 8
HD = 128
GRID_W = 64
EPS = 1e-6
NEG_INF = -1e30
ROPE_THETA = 10000.0
A_REACH = 128
B_PATTERNS = ((128, 1), (512, 4), (2048, 16))
NA_ROWS = 8
NA_COLS = 16
SCALE = HD ** -0.5
LOG2E = 1.4426950408889634
LN2 = 0.6931471805599453
SCALE2 = SCALE * LOG2E
A_Q, A_K, A_V, A_G = 0, 4, 6, 8
B_Q, B_K, B_V, B_G = 12, 16, 20, 24
C_Q, C_K, C_V, C_G = 28, 32, 34, 36
D_Q, D_K, D_V, D_G = 40, 44, 48, 52
IN_HEADS = 56

ADAM_LR, ADAM_B1, ADAM_B2, ADAM_EPS, ADAM_WD, ADAM_STEP = 0.001, 0.9, 0.999, 1e-08, 0.01, 10

V7X_VMEM_LIMIT = 56 * 2 ** 20
ATT_TILE = 512
BAND_TILE = 512
DENSE_TILE = 2048


def _pc(body, **kw):
    return pl.pallas_call(body, **kw)


def _cparams(sem=None):
    if sem is None:
        return pltpu.CompilerParams(vmem_limit_bytes=V7X_VMEM_LIMIT)
    return pltpu.CompilerParams(dimension_semantics=sem, vmem_limit_bytes=V7X_VMEM_LIMIT)


def _tile(n, cap):
    if n <= cap:
        return n
    t = (cap // 128) * 128
    while n % t:
        t -= 128
    return t


def _sds(shape, dtype):
    return jax.ShapeDtypeStruct(tuple(shape), dtype)


def _exchange_out(ins, gather):
    return [_sds((N_DEV,) + (a.shape if gather else a.shape[1:]), a.dtype) for a in ins]


def _exchange_sems(n):
    return [pltpu.SemaphoreType.DMA((n, N_DEV - 1)), pltpu.SemaphoreType.DMA((n, N_DEV - 1)), pltpu.SemaphoreType.DMA((n,))]


def _exchange_copies(in_refs, out_refs, send_sems, recv_sems, loc_sems, gather):
    x, y, c = lax.axis_index("x"), lax.axis_index("y"), lax.axis_index("c")
    me = 4 * x + 2 * y + c
    copies = []
    for t, (src, dst) in enumerate(zip(in_refs, out_refs)):
        copies.append(pltpu.make_async_copy(src if gather else src.at[me], dst.at[me], loc_sems.at[t]))
    for p in range(1, N_DEV):
        tx = 1 - x if p & 4 else x
        ty = 1 - y if p & 2 else y
        tc = 1 - c if p & 1 else c
        peer = 4 * tx + 2 * ty + tc
        for t, (src, dst) in enumerate(zip(in_refs, out_refs)):
            copies.append(pltpu.make_async_remote_copy(
                src_ref=src if gather else src.at[peer], dst_ref=dst.at[me],
                send_sem=send_sems.at[t, p - 1], recv_sem=recv_sems.at[t, p - 1],
                device_id=(tx, ty, tc), device_id_type=pl.DeviceIdType.MESH))
    return copies


def _exchange(name, ins, gather):
    n = len(ins)

    def body(*refs):
        copies = _exchange_copies(refs[:n], refs[n:2 * n], *refs[2 * n:], gather)
        for cp in copies:
            cp.start()
        for cp in copies:
            cp.wait()

    res = _pc(body, name=name, out_shape=_exchange_out(ins, gather),
              in_specs=[pl.BlockSpec(memory_space=pl.ANY)] * n, out_specs=[pl.BlockSpec(memory_space=pl.ANY)] * n,
              scratch_shapes=_exchange_sems(n))(*ins)
    return list(res)


class _Rider:
    def __init__(self, ins, gather):
        self.ins, self.gather, self.n = list(ins), gather, len(ins)

    def specs(self):
        any_ = [pl.BlockSpec(memory_space=pl.ANY)] * self.n
        return any_, _exchange_out(self.ins, self.gather), any_, _exchange_sems(self.n)

    def start(self, in_refs, out_refs, sems, grid):
        first = pl.program_id(0) == 0
        for ax in range(1, len(grid)):
            first = first & (pl.program_id(ax) == 0)

        @pl.when(first)
        def _():
            for cp in _exchange_copies(in_refs, out_refs, *sems, self.gather):
                cp.start()

    def finish(self, in_refs, out_refs, sems, grid):
        last = pl.program_id(0) == grid[0] - 1
        for ax in range(1, len(grid)):
            last = last & (pl.program_id(ax) == grid[ax] - 1)

        @pl.when(last)
        def _():
            for cp in _exchange_copies(in_refs, out_refs, *sems, self.gather):
                cp.wait()


_NN = (((1,), (0,)), ((), ()))
_NT = (((1,), (1,)), ((), ()))
_TN = (((0,), (0,)), ((), ()))


def _dot(a, b, dims):
    return lax.dot_general(a, b, dims, preferred_element_type=F32)


def _mm(name, a, w, out_sds, grid, a_spec, w_spec, o_spec, prod, red_axis=None, add=None, acc_shape=None,
        store=None, rider=None):
    nred = grid[red_axis] if red_axis is not None else 1
    if acc_shape is None:
        acc_shape = tuple(d for d in o_spec.block_shape if d is not None)
    nin = 3 if add is not None else 2
    nr = rider.n if rider is not None else 0

    def body(*refs):
        a_ref, w_ref = refs[:2]
        add_ref = refs[2] if add is not None else None
        o_ref = refs[nin + nr]
        scr = refs[nin + nr + 1 + nr:]
        comm = (refs[nin:nin + nr], refs[nin + nr + 1:nin + nr + 1 + nr], scr[-3:])
        if rider is not None:
            rider.start(*comm, grid)
        part = prod(a_ref, w_ref)

        def finish(acc):
            if add_ref is not None:
                acc = acc + add_ref[...].astype(F32)
            if store is not None:
                store(o_ref, acc)
            else:
                o_ref[...] = acc.astype(o_ref.dtype)

        if nred == 1:
            finish(part)
        else:
            acc_ref = scr[0]
            k = pl.program_id(red_axis)

            @pl.when(k == 0)
            def _():
                acc_ref[...] = part

            @pl.when(k > 0)
            def _():
                acc_ref[...] += part

            @pl.when(k == nred - 1)
            def _():
                finish(acc_ref[...])
        if rider is not None:
            rider.finish(*comm, grid)

    r_in, r_out, r_outspec, r_sems = rider.specs() if rider is not None else ([], [], [], [])
    if rider is not None:
        sem = ("arbitrary",) * len(grid)
    else:
        sem = tuple("arbitrary" if i == red_axis else "parallel" for i in range(len(grid)))
    in_specs = [a_spec, w_spec] + ([o_spec] if add is not None else []) + r_in
    args = (a, w) + ((add,) if add is not None else ()) + tuple(rider.ins if rider is not None else ())
    res = _pc(body, name=name, out_shape=[out_sds] + r_out, grid=grid, in_specs=in_specs,
              out_specs=[o_spec] + r_outspec,
              scratch_shapes=([pltpu.VMEM(acc_shape, F32)] if nred > 1 else []) + r_sems,
              compiler_params=_cparams(sem))(*args)
    return res[0] if rider is None else res


def _mm_nn(name, a, w, out_dtype, rider=None):
    T, K = a.shape
    G, _, n = w.shape
    tm, tn = min(T, 1024), _tile(n, 1024)
    nj = n // tn
    return _mm(name, a, w, _sds((T, G * n), out_dtype), (T // tm, G, nj),
               pl.BlockSpec((tm, K), lambda i, g, j: (i, 0)),
               pl.BlockSpec((None, K, tn), lambda i, g, j: (g, 0, j)),
               pl.BlockSpec((tm, tn), lambda i, g, j: (i, g * nj + j)),
               lambda a_ref, w_ref: _dot(a_ref[...], w_ref[...], _NN), rider=rider)


def _mm_nt(name, dy, w, out_dtype, add=None, rider=None):
    T = dy.shape[0]
    G, K, n = w.shape
    tm, tk = min(T, 1024), _tile(K, 1024)
    cps = 2 if G % 2 == 0 and n <= 1024 else 1

    def prod(a_ref, w_ref):
        acc = _dot(a_ref[:, 0:n], w_ref[0], _NT)
        for c in range(1, cps):
            acc = acc + _dot(a_ref[:, c * n:(c + 1) * n], w_ref[c], _NT)
        return acc

    return _mm(name, dy, w, _sds((T, K), out_dtype), (T // tm, K // tk, G // cps),
               pl.BlockSpec((tm, cps * n), lambda i, k, r: (i, r)),
               pl.BlockSpec((cps, tk, n), lambda i, k, r: (r, k, 0)),
               pl.BlockSpec((tm, tk), lambda i, k, r: (i, k)), prod, red_axis=2, add=add, rider=rider)


def _mm_tn(name, a, dy, G, out_dtype):
    T, K = a.shape
    n = dy.shape[1] // G
    tt, tk, tn = min(T, 2048), _tile(K, 1024), _tile(n, 1024)
    nj = n // tn
    return _mm(name, a, dy, _sds((G, K, n), out_dtype), (K // tk, G, nj, T // tt),
               pl.BlockSpec((tt, tk), lambda k, g, j, t: (t, k)),
               pl.BlockSpec((tt, tn), lambda k, g, j, t: (t, g * nj + j)),
               pl.BlockSpec((None, tk, tn), lambda k, g, j, t: (g, k, j)),
               lambda a_ref, w_ref: _dot(a_ref[...], w_ref[...], _TN), red_axis=3)


def _branch_weight(w_ref):
    return jnp.concatenate([w_ref[d] for d in range(N_DEV)], axis=1)


def _branch_nn(name, br, wb, out_dtype):
    T = br.shape[0]
    _, NB, W, n = wb.shape
    tm = min(T, 1024)
    return _mm(name, br, wb, _sds((T, NB * N_DEV * n), out_dtype), (T // tm, NB),
               pl.BlockSpec((tm, W), lambda i, b: (i, b)),
               pl.BlockSpec((N_DEV, None, W, n), lambda i, b: (0, b, 0, 0)),
               pl.BlockSpec((tm, N_DEV * n), lambda i, b: (i, b)),
               lambda a_ref, w_ref: _dot(a_ref[...], _branch_weight(w_ref), _NN))


def _branch_nt(name, dproj, wb, out_dtype):
    T = dproj.shape[0]
    _, NB, W, n = wb.shape
    tm = min(T, 1024)
    return _mm(name, dproj, wb, _sds((T, NB * W), out_dtype), (T // tm, NB),
               pl.BlockSpec((tm, N_DEV * n), lambda i, b: (i, b)),
               pl.BlockSpec((N_DEV, None, W, n), lambda i, b: (0, b, 0, 0)),
               pl.BlockSpec((tm, W), lambda i, b: (i, b)),
               lambda a_ref, w_ref: _dot(a_ref[...], _branch_weight(w_ref), _NT))


def _branch_tn(name, br, dproj, n, out_dtype):
    T = br.shape[0]
    NB = 4
    W = br.shape[1] // NB
    tt = min(T, 2048)

    def store(o_ref, acc):
        for d in range(N_DEV):
            o_ref[d] = acc[:, d * n:(d + 1) * n].astype(o_ref.dtype)

    return _mm(name, br, dproj, _sds((N_DEV, NB, W, n), out_dtype), (NB, T // tt),
               pl.BlockSpec((tt, W), lambda b, t: (t, b)),
               pl.BlockSpec((tt, N_DEV * n), lambda b, t: (t, b)),
               pl.BlockSpec((N_DEV, None, W, n), lambda b, t: (0, b, 0, 0)),
               lambda a_ref, w_ref: _dot(a_ref[...], w_ref[...], _TN), red_axis=1,
               acc_shape=(W, N_DEV * n), store=store)


class _Att:
    def __init__(self, L, R, H, G, qcol, kcol, vcol, qstr, kstr, vstr, mode, reach=0, pad=0, rows=0, tile=ATT_TILE):
        self.L, self.R, self.H, self.G = L, R, H, G
        self.qcol, self.kcol, self.vcol = qcol, kcol, vcol
        self.qstr, self.kstr, self.vstr = qstr, kstr, vstr
        self.mode, self.reach, self.rows = mode, reach, rows
        self.t = min(tile, L)
        self.nb = L // self.t
        self.tc = min(ATT_TILE, L)
        self.nc = L // self.tc
        self.pieces = mode != "dense" and self.nb > 1
        self.pad = pad if self.pieces else 0
        self.span = self.t + 2 * self.pad
        self.ppb = self.t // pad if self.pieces else 0

    def qc(self, r, h):
        return r * self.qstr + self.qcol + h

    def kc(self, r, h):
        return r * self.kstr + self.kcol + h // self.G

    def vc(self, r, h):
        return r * self.vstr + self.vcol + h // self.G

    def oc(self, r, h):
        return r * self.H + h

    def mask(self, qpos, kpos, spos):
        if self.mode == "band":
            ok = jnp.abs(qpos - kpos) <= self.reach
        else:
            rq, cq = qpos >> 6, qpos & (GRID_W - 1)
            rk, ck = kpos >> 6, kpos & (GRID_W - 1)
            rs = jnp.clip(rq - NA_ROWS // 2, 0, self.rows - NA_ROWS)
            cs = jnp.clip(cq - NA_COLS // 2, 0, GRID_W - NA_COLS)
            ok = (rk >= rs) & (rk < rs + NA_ROWS) & (ck >= cs) & (ck < cs + NA_COLS)
        if self.pieces:
            ok = ok & (spos >= 0) & (spos < self.L)
        return ok

    def span_specs(self, col):
        t, pad, ppb = self.t, self.pad, self.ppb
        cur = pl.BlockSpec((t, HD), lambda r, h, i: (i, col(r, h)))
        if not self.pieces:
            return [cur]
        last = self.L // pad - 1
        prev = pl.BlockSpec((pad, HD), lambda r, h, i: (jnp.maximum(i * ppb - 1, 0), col(r, h)))
        nxt = pl.BlockSpec((pad, HD), lambda r, h, i: (jnp.minimum((i + 1) * ppb, last), col(r, h)))
        return [prev, cur, nxt]

    def positions(self, i):
        own = i * self.t + lax.broadcasted_iota(jnp.int32, (self.t, 1), 0)
        spn = i * self.t - self.pad + lax.broadcasted_iota(jnp.int32, (1, self.span), 1)
        return own, spn


def _cat(refs):
    return refs[0][...] if len(refs) == 1 else jnp.concatenate([r[...] for r in refs], axis=0)


def _with_scores(att, raw, i, own_is_query, bias_ref, itab_ref, rest):
    def edge():
        s = raw if bias_ref is None else raw + bias_ref[...]
        own, spn = att.positions(i)
        qpos, kpos = (own, spn) if own_is_query else (spn, own)
        rest(jnp.where(att.mask(qpos, kpos, spn), s, NEG_INF))

    if itab_ref is None:
        edge()
        return
    inner = (i >= 1) & (i <= att.nb - 2)
    pl.when(inner)(lambda: rest(raw + itab_ref[...]))
    pl.when(jnp.logical_not(inner))(edge)


def _tab_spec(att, tab):
    if tab.shape[0] == 1:
        return pl.BlockSpec((None, att.t, att.span), lambda r, h, i: (0, 0, 0))
    return pl.BlockSpec((None, att.t, att.span), lambda r, h, i: (h, 0, 0))


def _band_fwd(name, att, q, k, v, sink=None, bias=None, itab=None):
    t, ns = att.t, (3 if att.pieces else 1)

    def body(*refs):
        q_ref, k_refs, v_refs = refs[0], refs[1:1 + ns], refs[1 + ns:1 + 2 * ns]
        pos = 1 + 2 * ns
        sink_ref = bias_ref = itab_ref = None
        if sink is not None:
            sink_ref = refs[pos]
            pos += 1
        if bias is not None:
            bias_ref = refs[pos]
            pos += 1
        if itab is not None:
            itab_ref = refs[pos]
            pos += 1
        o_ref, lse_ref = refs[pos:]
        ks, vs = _cat(k_refs), _cat(v_refs)

        def rest(s):
            m = jnp.max(s, axis=1, keepdims=True)
            if sink_ref is not None:
                sk = sink_ref[...][:, :1] * LOG2E
                m = jnp.maximum(m, sk)
            p = jnp.exp2(s - m)
            den = jnp.sum(p, axis=1, keepdims=True)
            if sink_ref is not None:
                den = den + jnp.exp2(sk - m)
            o = jnp.dot(p.astype(BF16), vs, preferred_element_type=F32)
            o_ref[...] = (o / den).astype(o_ref.dtype)
            lse_ref[...] = jnp.broadcast_to((m + jnp.log2(den)) * LN2, (t, HD))

        _with_scores(att, _dot(q_ref[...], ks, _NT), pl.program_id(2), True, bias_ref, itab_ref, rest)

    in_specs = [pl.BlockSpec((t, HD), lambda r, h, i: (i, att.qc(r, h)))] + att.span_specs(att.kc) + att.span_specs(att.vc)
    args = [q] + [k] * ns + [v] * ns
    if sink is not None:
        in_specs.append(pl.BlockSpec((None, 1, HD), lambda r, h, i: (h, 0, 0)))
        args.append(sink)
    for tab in (bias, itab):
        if tab is not None:
            in_specs.append(_tab_spec(att, tab))
            args.append(tab)
    o_spec = pl.BlockSpec((t, HD), lambda r, h, i: (i, att.oc(r, h)))
    cols = att.R * att.H * HD
    return _pc(body, name=name, grid=(att.R, att.H, att.nb),
               out_shape=[_sds((att.L, cols), BF16), _sds((att.L, cols), F32)],
               in_specs=in_specs, out_specs=[o_spec, o_spec],
               compiler_params=_cparams(("parallel", "parallel", "parallel")))(*args)


def _band_dq(name, att, q, k, v, do, lse, delta, bias=None, itab=None):
    t, ns = att.t, (3 if att.pieces else 1)

    def body(*refs):
        q_ref, k_refs, v_refs = refs[0], refs[1:1 + ns], refs[1 + ns:1 + 2 * ns]
        do_ref, lse_ref, dl_ref = refs[1 + 2 * ns:4 + 2 * ns]
        pos = 4 + 2 * ns
        bias_ref = itab_ref = dtab_ref = None
        if bias is not None:
            bias_ref = refs[pos]
            pos += 1
        if itab is not None:
            itab_ref = refs[pos]
            pos += 1
        dq_ref = refs[pos]
        if bias is not None:
            dtab_ref = refs[pos + 1]
        i = pl.program_id(2)
        ks, vs = _cat(k_refs), _cat(v_refs)

        def rest(s):
            p = jnp.exp2(s - lse_ref[...][:, :1] * LOG2E)
            dp = _dot(do_ref[...], vs, _NT)
            ds = p * (dp - dl_ref[...][:, :1])
            if dtab_ref is not None:
                @pl.when(i == 0)
                def _():
                    dtab_ref[...] = ds

                @pl.when(i > 0)
                def _():
                    dtab_ref[...] += ds
            dq_ref[...] = jnp.dot(ds.astype(BF16), ks, preferred_element_type=F32) * SCALE

        _with_scores(att, _dot(q_ref[...], ks, _NT), i, True, bias_ref, itab_ref, rest)

    row = pl.BlockSpec((t, HD), lambda r, h, i: (i, att.oc(r, h)))
    in_specs = [pl.BlockSpec((t, HD), lambda r, h, i: (i, att.qc(r, h)))] + att.span_specs(att.kc) + att.span_specs(att.vc)
    in_specs += [row, row, row]
    args = [q] + [k] * ns + [v] * ns + [do, lse, delta]
    for tab in (bias, itab):
        if tab is not None:
            in_specs.append(_tab_spec(att, tab))
            args.append(tab)
    out_shape = [_sds((att.L, att.R * att.H * HD), F32)]
    out_specs = [row]
    sem = ("parallel", "parallel", "parallel")
    if bias is not None:
        out_shape.append(_sds((att.H, t, att.span), F32))
        out_specs.append(_tab_spec(att, bias))
        sem = ("parallel", "parallel", "arbitrary")
    return _pc(body, name=name, grid=(att.R, att.H, att.nb), out_shape=out_shape, in_specs=in_specs,
               out_specs=out_specs, compiler_params=_cparams(sem))(*args)


def _band_dkv(name, att, q, k, v, do, lse_sp, dl_sp, bias_t=None, itab_t=None):
    t, ns = att.t, (3 if att.pieces else 1)

    def body(*refs):
        k_ref, v_ref = refs[:2]
        q_refs, do_refs = refs[2:2 + ns], refs[2 + ns:2 + 2 * ns]
        lse_ref, dl_ref = refs[2 + 2 * ns:4 + 2 * ns]
        pos = 4 + 2 * ns
        bias_ref = itab_ref = None
        if bias_t is not None:
            bias_ref = refs[pos]
            pos += 1
        if itab_t is not None:
            itab_ref = refs[pos]
            pos += 1
        dk_ref, dv_ref = refs[pos:]
        qs, dos = _cat(q_refs), _cat(do_refs)

        def rest(st):
            pt = jnp.exp2(st - lse_ref[...] * LOG2E)
            dv_ref[...] = jnp.dot(pt.astype(BF16), dos, preferred_element_type=F32)
            dpt = _dot(v_ref[...], dos, _NT)
            dst = pt * (dpt - dl_ref[...])
            dk_ref[...] = jnp.dot(dst.astype(BF16), qs, preferred_element_type=F32) * LN2

        _with_scores(att, _dot(k_ref[...], qs, _NT), pl.program_id(2), False, bias_ref, itab_ref, rest)

    stat = pl.BlockSpec((None, None, 1, att.span), lambda r, h, i: (r * att.H + h, i, 0, 0))
    in_specs = [pl.BlockSpec((t, HD), lambda r, h, i: (i, att.kc(r, h))),
                pl.BlockSpec((t, HD), lambda r, h, i: (i, att.vc(r, h)))]
    in_specs += att.span_specs(att.qc) + att.span_specs(att.oc) + [stat, stat]
    args = [k, v] + [q] * ns + [do] * ns + [lse_sp, dl_sp]
    for tab in (bias_t, itab_t):
        if tab is not None:
            in_specs.append(_tab_spec(att, tab))
            args.append(tab)
    o_spec = pl.BlockSpec((t, HD), lambda r, h, i: (i, att.oc(r, h)))
    cols = att.R * att.H * HD
    return _pc(body, name=name, grid=(att.R, att.H, att.nb), out_shape=[_sds((att.L, cols), F32)] * 2,
               in_specs=in_specs, out_specs=[o_spec, o_spec],
               compiler_params=_cparams(("parallel", "parallel", "parallel")))(*args)


def _dense_fwd(name, att, q, k, v, rider=None):
    t, L, nb = att.t, att.L, att.nb
    grid = (att.H, nb)
    nr = rider.n if rider is not None else 0

    def body(*refs):
        q_ref, k_ref, v_ref = refs[:3]
        o_ref, lse_ref = refs[3 + nr:5 + nr]
        comm = (refs[3:3 + nr], refs[5 + nr:5 + 2 * nr], refs[5 + 2 * nr:])
        if rider is not None:
            rider.start(*comm, grid)
        qq = q_ref[...]

        def step(j, carry):
            m, den, acc = carry
            rows = pl.ds(pl.multiple_of(j * att.tc, att.tc), att.tc)
            s = lax.dot_general(qq, k_ref[rows, :], _NT, preferred_element_type=F32)
            m_new = jnp.maximum(m, jnp.max(s, axis=1, keepdims=True))
            alpha = jnp.exp2(m - m_new)
            p = jnp.exp2(s - m_new)
            den = alpha * den + jnp.sum(p, axis=1, keepdims=True)
            acc = alpha * acc + jnp.dot(p.astype(BF16), v_ref[rows, :], preferred_element_type=F32)
            return m_new, den, acc

        init = (jnp.full((t, 1), NEG_INF, F32), jnp.zeros((t, 1), F32), jnp.zeros((t, HD), F32))
        m, den, acc = lax.fori_loop(0, att.nc, step, init, unroll=2 if att.nc % 2 == 0 else 1)
        o_ref[...] = (acc / den).astype(o_ref.dtype)
        lse_ref[...] = jnp.broadcast_to((m + jnp.log2(den)) * LN2, (t, HD))
        if rider is not None:
            rider.finish(*comm, grid)

    o_spec = pl.BlockSpec((t, HD), lambda h, i: (i, att.oc(0, h)))
    r_in, r_out, r_outspec, r_sems = rider.specs() if rider is not None else ([], [], [], [])
    sem = ("arbitrary", "arbitrary") if rider is not None else ("parallel", "parallel")
    return _pc(body, name=name, grid=grid,
               out_shape=[_sds((L, att.H * HD), BF16), _sds((L, att.H * HD), F32)] + r_out,
               in_specs=[pl.BlockSpec((t, HD), lambda h, i: (i, att.qc(0, h))),
                         pl.BlockSpec((L, HD), lambda h, i: (0, att.kc(0, h))),
                         pl.BlockSpec((L, HD), lambda h, i: (0, att.vc(0, h)))] + r_in,
               out_specs=[o_spec, o_spec] + r_outspec, scratch_shapes=r_sems,
               compiler_params=_cparams(sem))(q, k, v, *(rider.ins if rider is not None else []))


def _dense_dq(name, att, q, k, v, do, lse, delta, rider=None):
    t, L, nb = att.t, att.L, att.nb
    grid = (att.H, nb)
    nr = rider.n if rider is not None else 0

    def body(*refs):
        q_ref, k_ref, v_ref, do_ref, lse_ref, dl_ref = refs[:6]
        dq_ref = refs[6 + nr]
        comm = (refs[6:6 + nr], refs[7 + nr:7 + 2 * nr], refs[7 + 2 * nr:])
        if rider is not None:
            rider.start(*comm, grid)
        qq, dd = q_ref[...], do_ref[...]
        lse2 = lse_ref[...][:, :1] * LOG2E
        dl = dl_ref[...][:, :1]

        def step(j, acc):
            rows = pl.ds(pl.multiple_of(j * att.tc, att.tc), att.tc)
            kk = k_ref[rows, :]
            s = lax.dot_general(qq, kk, _NT, preferred_element_type=F32)
            p = jnp.exp2(s - lse2)
            dp = lax.dot_general(dd, v_ref[rows, :], _NT, preferred_element_type=F32)
            ds = p * (dp - dl)
            return acc + jnp.dot(ds.astype(BF16), kk, preferred_element_type=F32)

        acc = lax.fori_loop(0, att.nc, step, jnp.zeros((t, HD), F32), unroll=2 if att.nc % 2 == 0 else 1)
        dq_ref[...] = acc * SCALE
        if rider is not None:
            rider.finish(*comm, grid)

    row = pl.BlockSpec((t, HD), lambda h, i: (i, att.oc(0, h)))
    r_in, r_out, r_outspec, r_sems = rider.specs() if rider is not None else ([], [], [], [])
    sem = ("arbitrary", "arbitrary") if rider is not None else ("parallel", "parallel")
    res = _pc(body, name=name, grid=grid, out_shape=[_sds((L, att.H * HD), F32)] + r_out,
              in_specs=[pl.BlockSpec((t, HD), lambda h, i: (i, att.qc(0, h))),
                        pl.BlockSpec((L, HD), lambda h, i: (0, att.kc(0, h))),
                        pl.BlockSpec((L, HD), lambda h, i: (0, att.vc(0, h))), row, row, row] + r_in,
              out_specs=[row] + r_outspec, scratch_shapes=r_sems,
              compiler_params=_cparams(sem))(q, k, v, do, lse, delta, *(rider.ins if rider is not None else []))
    return res[0], list(res[1:])


def _dense_dkv(name, att, q, k, v, do, lse_rows, dl_rows, rider=None):
    t, L, nb = att.t, att.L, att.nb
    grid = (att.H, nb)
    nr = rider.n if rider is not None else 0

    def body(*refs):
        k_ref, v_ref, q_ref, do_ref, lse_ref, dl_ref = refs[:6]
        dk_ref, dv_ref = refs[6 + nr:8 + nr]
        comm = (refs[6:6 + nr], refs[8 + nr:8 + 2 * nr], refs[8 + 2 * nr:])
        if rider is not None:
            rider.start(*comm, grid)
        kk, vv = k_ref[...], v_ref[...]

        def step(i, carry):
            dk, dv = carry
            rows = pl.ds(pl.multiple_of(i * att.tc, att.tc), att.tc)
            qq, dd = q_ref[rows, :], do_ref[rows, :]
            st = lax.dot_general(kk, qq, _NT, preferred_element_type=F32)
            pt = jnp.exp2(st - lse_ref[i] * LOG2E)
            dv = dv + jnp.dot(pt.astype(BF16), dd, preferred_element_type=F32)
            dpt = lax.dot_general(vv, dd, _NT, preferred_element_type=F32)
            dst = pt * (dpt - dl_ref[i])
            dk = dk + jnp.dot(dst.astype(BF16), qq, preferred_element_type=F32)
            return dk, dv

        zero = jnp.zeros((t, HD), F32)
        dk, dv = lax.fori_loop(0, att.nc, step, (zero, zero), unroll=2 if att.nc % 2 == 0 else 1)
        dk_ref[...] = dk * LN2
        dv_ref[...] = dv
        if rider is not None:
            rider.finish(*comm, grid)

    stat = pl.BlockSpec((None, att.nc, 1, att.tc), lambda h, j: (h, 0, 0, 0))
    o_spec = pl.BlockSpec((t, HD), lambda h, j: (j, att.oc(0, h)))
    r_in, r_out, r_outspec, r_sems = rider.specs() if rider is not None else ([], [], [], [])
    sem = ("arbitrary", "arbitrary") if rider is not None else ("parallel", "parallel")
    return _pc(body, name=name, grid=grid, out_shape=[_sds((L, att.H * HD), F32)] * 2 + r_out,
               in_specs=[pl.BlockSpec((t, HD), lambda h, j: (j, att.kc(0, h))),
                         pl.BlockSpec((t, HD), lambda h, j: (j, att.vc(0, h))),
                         pl.BlockSpec((L, HD), lambda h, j: (0, att.qc(0, h))),
                         pl.BlockSpec((L, HD), lambda h, j: (0, att.oc(0, h))), stat, stat] + r_in,
               out_specs=[o_spec, o_spec] + r_outspec, scratch_shapes=r_sems,
               compiler_params=_cparams(sem))(k, v, q, do, lse_rows, dl_rows, *(rider.ins if rider is not None else []))


def _stat_rows(att, col):
    RH = att.R * att.H
    rows = col.reshape(att.L, RH, HD)[:, :, 0].T
    if att.mode == "dense":
        return rows.reshape(RH, att.nc, 1, att.tc)
    if not att.pieces:
        return rows.reshape(RH, att.nb, 1, att.t)
    padded = jnp.pad(rows, ((0, 0), (att.pad, att.pad)))
    return jnp.stack([padded[:, j * att.t:j * att.t + att.span] for j in range(att.nb)], axis=1)[:, :, None, :]


def _interior_table(att, transposed, bias=None):
    if not (att.pieces and att.nb > 2):
        return None
    own = np.arange(att.t)[:, None]
    spn = np.arange(att.span)[None, :] - att.pad
    qpos, kpos = (spn, own) if transposed else (own, spn)
    if att.mode == "band":
        ok = np.abs(qpos - kpos) <= att.reach
    else:
        rq, cq = qpos // GRID_W, qpos % GRID_W
        rk, ck = kpos // GRID_W, kpos % GRID_W
        rs = rq - NA_ROWS // 2
        cs = np.clip(cq - NA_COLS // 2, 0, GRID_W - NA_COLS)
        ok = (rk >= rs) & (rk < rs + NA_ROWS) & (ck >= cs) & (ck < cs + NA_COLS)
    if bias is None:
        return jnp.asarray(np.where(ok, 0.0, NEG_INF)[None], F32)
    return jnp.where(jnp.asarray(ok)[None], bias, NEG_INF)


def _attn_fwd(name, att, q, k, v, sink=None, bias=None, rider=None):
    if att.mode == "dense":
        return _dense_fwd(name, att, q, k, v, rider=rider)
    return _band_fwd(name, att, q, k, v, sink=sink, bias=bias, itab=_interior_table(att, False, bias))


def _attn_bwd(name, att, q, k, v, do, lse, delta, bias=None, bias_t=None, riders=(None, None)):
    lse_r, dl_r = _stat_rows(att, lse), _stat_rows(att, delta)
    if att.mode == "dense":
        dq, got_q = _dense_dq(name + "_dq", att, q, k, v, do, lse, delta, rider=riders[0])
        res = _dense_dkv(name + "_dkv", att, q, k, v, do, lse_r, dl_r, rider=riders[1])
        return dq, res[0], res[1], (got_q, list(res[2:]))
    res = _band_dq(name + "_dq", att, q, k, v, do, lse, delta, bias=bias, itab=_interior_table(att, False, bias))
    dk, dv = _band_dkv(name + "_dkv", att, q, k, v, do, lse_r, dl_r, bias_t=bias_t,
                       itab_t=_interior_table(att, True, bias_t))
    return res[0], dk, dv, (res[1] if bias is not None else None)


def _row_call(name, body, T, tt, ins, outs, acc_outs=()):
    in_specs, args = [], []
    for item in ins:
        a = item[0]
        if item[1] is None:
            in_specs.append(pl.BlockSpec(a.shape, lambda i, nd=a.ndim: (0,) * nd))
        else:
            in_specs.append(pl.BlockSpec((tt, item[1]), lambda i, cb=item[2]: (i, cb)))
        args.append(a)
    out_shape = [_sds((T, c), d) for c, d in outs] + [_sds(s, F32) for s in acc_outs]
    out_specs = [pl.BlockSpec((tt, c), lambda i: (i, 0)) for c, _ in outs]
    out_specs += [pl.BlockSpec(s, lambda i, nd=len(s): (0,) * nd) for s in acc_outs]
    sem = ("arbitrary",) if acc_outs else ("parallel",)
    return _pc(body, name=name, grid=(T // tt,), out_shape=out_shape, in_specs=in_specs, out_specs=out_specs,
               compiler_params=_cparams(sem))(*args)


def _acc(ref, val):
    @pl.when(pl.program_id(0) == 0)
    def _():
        ref[...] = val

    @pl.when(pl.program_id(0) > 0)
    def _():
        ref[...] += val


def _prenorm_fwd(name, x, g, scale, shift):
    T, D = x.shape

    def body(x_ref, g_ref, sc_ref, sh_ref, h_ref):
        xf = x_ref[...]
        r = lax.rsqrt(jnp.mean(xf * xf, axis=1, keepdims=True) + EPS)
        h_ref[...] = ((xf * r) * g_ref[...] * (1.0 + sc_ref[...]) + sh_ref[...]).astype(BF16)

    return _row_call(name, body, T, min(T, 256), [(x, D, 0), (g, None), (scale, None), (shift, None)], [(D, BF16)])[0]


def _prenorm_bwd(name, x, dh, dxn, g, scale):
    T, D = x.shape

    def body(x_ref, dh_ref, dxn_ref, g_ref, sc_ref, dx_ref, dsh_ref, dsc_ref, dg_ref):
        xf, dh_, gg = x_ref[...], dh_ref[...], g_ref[...]
        r = lax.rsqrt(jnp.mean(xf * xf, axis=1, keepdims=True) + EPS)
        u = xf * r
        dn = dh_ * (1.0 + sc_ref[...])
        du = dn * gg
        dx_ref[...] = dxn_ref[...] + r * (du - u * jnp.mean(du * u, axis=1, keepdims=True))
        _acc(dsh_ref, jnp.sum(dh_, axis=0, keepdims=True))
        _acc(dsc_ref, jnp.sum(dh_ * (u * gg), axis=0, keepdims=True))
        _acc(dg_ref, jnp.sum(dn * u, axis=0, keepdims=True))

    return _row_call(name, body, T, min(T, 256), [(x, D, 0), (dh, D, 0), (dxn, D, 0), (g, None), (scale, None)],
                     [(D, F32)], [(1, D)] * 3)


def _resid_fwd(name, x, out, gate):
    T, D = x.shape

    def body(x_ref, o_ref, g_ref, y_ref):
        y_ref[...] = x_ref[...] + g_ref[...] * o_ref[...]

    return _row_call(name, body, T, min(T, 256), [(x, D, 0), (out, D, 0), (gate, None)], [(D, F32)])[0]


def _resid_bwd(name, dx, out, gate):
    T, D = dx.shape

    def body(dx_ref, o_ref, g_ref, do_ref, dg_ref):
        d = dx_ref[...]
        do_ref[...] = (d * g_ref[...]).astype(BF16)
        _acc(dg_ref, jnp.sum(d * o_ref[...], axis=0, keepdims=True))

    return _row_call(name, body, T, min(T, 256), [(dx, D, 0), (out, D, 0), (gate, None)], [(D, BF16)], [(1, D)])


def _final(name, x, tgt, g):
    T, D = x.shape

    def body(x_ref, t_ref, g_ref, dx_ref, loss_ref, dg_ref):
        xf, gg = x_ref[...], g_ref[...]
        r = lax.rsqrt(jnp.mean(xf * xf, axis=1, keepdims=True) + EPS)
        u = xf * r
        err = u * gg - t_ref[...]
        part = 0.5 * jnp.sum(jnp.mean(err * err, axis=1, keepdims=True), axis=0, keepdims=True)
        _acc(loss_ref, jnp.broadcast_to(part, (1, HD)))
        dy = err * (1.0 / D)
        _acc(dg_ref, jnp.sum(dy * u, axis=0, keepdims=True))
        du = dy * gg
        dx_ref[...] = r * (du - u * jnp.mean(du * u, axis=1, keepdims=True))

    return _row_call(name, body, T, min(T, 256), [(x, D, 0), (tgt, D, 0), (g, None)], [(D, F32)], [(1, HD), (1, D)])


def _roll_pair(x, shift):
    lanes = lax.broadcasted_iota(jnp.int32, x.shape, 1)
    r1 = pltpu.roll(x, shift, 1)
    r2 = pltpu.roll(x, HD - shift, 1)
    src = pltpu.roll(lanes, shift, 1)
    is_plus = src == ((lanes + shift) & (HD - 1))
    return jnp.where(is_plus, r1, r2), jnp.where(is_plus, r2, r1)


def _rope1(x, cos, ss):
    xp, _ = _roll_pair(x, 64)
    return x * cos + xp * ss


def _rope1_t(d, cos, ss):
    dp, _ = _roll_pair(d * ss, 64)
    return d * cos + dp


def _ropex(x, cos, sa, sb):
    xp, xm = _roll_pair(x, 32)
    return x * cos + xp * sa + xm * sb


def _ropex_t(d, cos, sa, sb):
    _, am = _roll_pair(d * sa, 32)
    bp, _ = _roll_pair(d * sb, 32)
    return d * cos + am + bp


def _qk_prep(name, qkvg, tabs, cqn, ckn):
    T = qkvg.shape[0]
    cos1, ss1, cosx, sax, sbx = tabs

    def body(aq, ak, bq, bk, bv, cq, ck, dq, c1, s1, cx, ax, bx, gq, gk, oaq, oak, obq, obk, obv, ocq, ock, odq):
        c1v, s1v = c1[...], s1[...]
        for src, dst, nh, mul in ((aq, oaq, 4, SCALE2), (ak, oak, 2, None), (bq, obq, 4, SCALE2), (bk, obk, 4, None)):
            for h in range(nh):
                sl = slice(h * HD, (h + 1) * HD)
                y = _rope1(src[:, sl].astype(F32), c1v, s1v)
                dst[:, sl] = (y if mul is None else y * mul).astype(BF16)
        obv[...] = bv[...]
        odq[...] = (dq[...].astype(F32) * SCALE2).astype(BF16)
        cxv, axv, bxv = cx[...], ax[...], bx[...]
        for src, dst, nh, gref, mul in ((cq, ocq, 4, gq, SCALE2), (ck, ock, 2, gk, None)):
            for h in range(nh):
                sl = slice(h * HD, (h + 1) * HD)
                xf = src[:, sl].astype(F32)
                r = lax.rsqrt(jnp.mean(xf * xf, axis=1, keepdims=True) + EPS)
                y = _ropex(xf * r * gref[...], cxv, axv, bxv)
                dst[:, sl] = (y if mul is None else y * mul).astype(BF16)

    ins = [(qkvg, 512, 0), (qkvg, 256, 2), (qkvg, 512, 3), (qkvg, 512, 4), (qkvg, 512, 5), (qkvg, 512, 7),
           (qkvg, 256, 16), (qkvg, 512, 10),
           (cos1, HD, 0), (ss1, HD, 0), (cosx, HD, 0), (sax, HD, 0), (sbx, HD, 0), (cqn, None), (ckn, None)]
    outs = [(512, BF16), (256, BF16), (512, BF16), (512, BF16), (512, BF16), (512, BF16), (256, BF16), (512, BF16)]
    return _row_call(name, body, T, min(T, 512), ins, outs)


def _silu_parts(g):
    sig = 1.0 / (1.0 + jnp.exp(-g))
    return g * sig, sig * (1.0 + g * (1.0 - sig))


def _mix_weights(l0, l1, l2):
    mx = jnp.maximum(jnp.maximum(l0, l1), l2)
    e0, e1, e2 = jnp.exp(l0 - mx), jnp.exp(l1 - mx), jnp.exp(l2 - mx)
    inv = 1.0 / (e0 + e1 + e2)
    return e0 * inv, e1 * inv, e2 * inv


def _gate_fwd(name, qkvg, oa, ob, lb, oc, od):
    T = qkvg.shape[0]

    def body(ga, gb, gc, gd, oa_r, ob0, ob1, ob2, lb0, lb1, lb2, oc_r, od_r, br):
        w0, w1, w2 = _mix_weights(lb0[...], lb1[...], lb2[...])
        yb = w0 * ob0[...].astype(F32) + w1 * ob1[...].astype(F32) + w2 * ob2[...].astype(F32)
        ys = (oa_r[...].astype(F32), yb, oc_r[...].astype(F32), od_r[...].astype(F32))
        for n, (y, g) in enumerate(zip(ys, (ga, gb, gc, gd))):
            act, _ = _silu_parts(g[...].astype(F32))
            br[:, n * 512:(n + 1) * 512] = (y * act).astype(BF16)

    ins = [(qkvg, 512, 2), (qkvg, 512, 6), (qkvg, 512, 9), (qkvg, 512, 13), (oa, 512, 0)]
    ins += [(o, 512, 0) for o in ob] + [(l_, 512, 0) for l_ in lb] + [(oc, 512, 0), (od, 512, 0)]
    return _row_call(name, body, T, min(T, 256), ins, [(2048, BF16)])[0]


def _head_rowsum(x):
    parts = []
    for h in range(x.shape[1] // HD):
        s = jnp.sum(x[:, h * HD:(h + 1) * HD], axis=1, keepdims=True)
        parts.append(jnp.broadcast_to(s, (x.shape[0], HD)))
    return jnp.concatenate(parts, axis=1)


def _gate_bwd(name, dbr, qkvg, oa, la, sink_row, ob, lb, oc, od):
    T = qkvg.shape[0]

    def body(dbr_r, ga, gb, gc, gd, oa_r, la_r, sk, ob0, ob1, ob2, lb0, lb1, lb2, oc_r, od_r,
             doa, dla, dob0, dob1, dob2, dlb0, dlb1, dlb2, doc, dlc, dod, dld, dg, dsk):
        def one(n, g_ref, y):
            act, dact = _silu_parts(g_ref[...].astype(F32))
            d = dbr_r[:, n * 512:(n + 1) * 512]
            dg[:, n * 512:(n + 1) * 512] = (d * y * dact).astype(BF16)
            return d * act

        ya = oa_r[...].astype(F32)
        dya = one(0, ga, ya)
        doa[...] = dya.astype(BF16)
        dl_a = _head_rowsum(dya * ya)
        dla[...] = dl_a
        _acc(dsk, -jnp.sum(jnp.exp(sk[...] - la_r[...]) * dl_a, axis=0, keepdims=True))

        w = _mix_weights(lb0[...], lb1[...], lb2[...])
        obs = (ob0[...].astype(F32), ob1[...].astype(F32), ob2[...].astype(F32))
        yb = w[0] * obs[0] + w[1] * obs[1] + w[2] * obs[2]
        dyb = one(1, gb, yb)
        rs = _head_rowsum(dyb * yb)
        for wp, do_ref, dl_ref in zip(w, (dob0, dob1, dob2), (dlb0, dlb1, dlb2)):
            do_ref[...] = (wp * dyb).astype(BF16)
            dl_ref[...] = wp * rs

        for n, g_ref, o_r, do_ref, dl_ref in ((2, gc, oc_r, doc, dlc), (3, gd, od_r, dod, dld)):
            y = o_r[...].astype(F32)
            dy = one(n, g_ref, y)
            do_ref[...] = dy.astype(BF16)
            dl_ref[...] = _head_rowsum(dy * y)

    ins = [(dbr, 2048, 0), (qkvg, 512, 2), (qkvg, 512, 6), (qkvg, 512, 9), (qkvg, 512, 13),
           (oa, 512, 0), (la, 512, 0), (sink_row, None)]
    ins += [(o, 512, 0) for o in ob] + [(l_, 512, 0) for l_ in lb] + [(oc, 512, 0), (od, 512, 0)]
    outs = [(512, BF16), (512, F32)] + [(512, BF16)] * 3 + [(512, F32)] * 3 + [(512, BF16), (512, F32)] * 2
    outs += [(2048, BF16)]
    return _row_call(name, body, T, min(T, 256), ins, outs, [(1, 512)])


def _merge_fwd(name, mgl, proj, D):
    T = mgl.shape[0]

    def body(m_ref, p_ref, o_ref):
        acc = None
        for n in range(4):
            sl = slice(n * D, (n + 1) * D)
            sig = 1.0 / (1.0 + jnp.exp(-m_ref[:, sl].astype(F32)))
            term = sig * p_ref[:, sl].astype(F32)
            acc = term if acc is None else acc + term
        o_ref[...] = acc.astype(BF16)

    return _row_call(name, body, T, min(T, 256), [(mgl, 4 * D, 0), (proj, 4 * D, 0)], [(D, BF16)])[0]


def _merge_bwd(name, dm, mgl, proj, D):
    T = mgl.shape[0]

    def body(d_ref, m_ref, p_ref, dp_ref, dl_ref):
        d = d_ref[...]
        for n in range(4):
            sl = slice(n * D, (n + 1) * D)
            sig = 1.0 / (1.0 + jnp.exp(-m_ref[:, sl].astype(F32)))
            dp_ref[:, sl] = (d * sig).astype(BF16)
            dl_ref[:, sl] = (d * p_ref[:, sl].astype(F32) * sig * (1.0 - sig)).astype(BF16)

    return _row_call(name, body, T, min(T, 128), [(dm, D, 0), (mgl, 4 * D, 0), (proj, 4 * D, 0)],
                     [(4 * D, BF16), (4 * D, BF16)])


def _qkvg_bwd(name, qkvg, tabs, cqn, ckn, dA, dB, dC, dD, dg):
    T = qkvg.shape[0]
    cos1, ss1, cosx, sax, sbx = tabs

    def body(cq_r, ck_r, c1, s1, cx, ax, bx, gq, gk,
             dqa, dka, dva, dqb0, dqb1, dqb2, dkb0, dkb1, dkb2, dvb0, dvb1, dvb2,
             dqc, dkc, dvc, dqd, dkd, dvd, dg_r, out, dgq, dgk):
        c1v, s1v = c1[...], s1[...]

        def put(col, val):
            out[:, col * HD:(col + 1) * HD] = val.astype(BF16)

        def pair(ref, kv):
            return ref[:, 2 * kv * HD:(2 * kv + 1) * HD] + ref[:, (2 * kv + 1) * HD:(2 * kv + 2) * HD]

        for h in range(4):
            sl = slice(h * HD, (h + 1) * HD)
            put(A_Q + h, _rope1_t(dqa[:, sl], c1v, s1v))
            put(B_Q + h, _rope1_t(dqb0[:, sl] + dqb1[:, sl] + dqb2[:, sl], c1v, s1v))
            put(B_K + h, _rope1_t(dkb0[:, sl] + dkb1[:, sl] + dkb2[:, sl], c1v, s1v))
            put(B_V + h, dvb0[:, sl] + dvb1[:, sl] + dvb2[:, sl])
            put(D_Q + h, dqd[:, sl])
            put(D_K + h, dkd[:, sl])
            put(D_V + h, dvd[:, sl])
        for kv in range(2):
            put(A_K + kv, _rope1_t(pair(dka, kv), c1v, s1v))
            put(A_V + kv, pair(dva, kv))
            put(C_V + kv, pair(dvc, kv))
        cxv, axv, bxv = cx[...], ax[...], bx[...]
        for src, dref, col, nh, gref, dgref in ((cq_r, dqc, C_Q, 4, gq, dgq), (ck_r, dkc, C_K, 2, gk, dgk)):
            gsum = None
            for h in range(nh):
                sl = slice(h * HD, (h + 1) * HD)
                xf = src[:, sl].astype(F32)
                r = lax.rsqrt(jnp.mean(xf * xf, axis=1, keepdims=True) + EPS)
                u = xf * r
                dy = _ropex_t(dref[:, sl] if nh == 4 else pair(dref, h), cxv, axv, bxv)
                du = dy * gref[...]
                put(col + h, r * (du - u * jnp.mean(du * u, axis=1, keepdims=True)))
                part = jnp.sum(dy * u, axis=0, keepdims=True)
                gsum = part if gsum is None else gsum + part
            _acc(dgref, gsum)
        for n, col in enumerate((A_G, B_G, C_G, D_G)):
            out[:, col * HD:(col + 4) * HD] = dg_r[:, n * 512:(n + 1) * 512]

    ins = [(qkvg, 512, 7), (qkvg, 256, 16), (cos1, HD, 0), (ss1, HD, 0), (cosx, HD, 0), (sax, HD, 0), (sbx, HD, 0),
           (cqn, None), (ckn, None)]
    ins += [(a, 512, 0) for a in dA]
    ins += [(a, 512, 0) for a in dB[0]] + [(a, 512, 0) for a in dB[1]] + [(a, 512, 0) for a in dB[2]]
    ins += [(a, 512, 0) for a in dC] + [(a, 512, 0) for a in dD] + [(dg, 2048, 0)]
    return _row_call(name, body, T, min(T, 256), ins, [(IN_HEADS * HD, BF16)], [(1, HD), (1, HD)])


def _ada_fwd(name, c_pad, w_ada):
    L, D, n = w_ada.shape
    P = c_pad.shape[0]

    def body(c_ref, w_ref, o_ref):
        cc = c_ref[...]
        cond = cc / (1.0 + jnp.exp(-cc))
        o_ref[...] = jnp.dot(cond.astype(BF16), w_ref[...].astype(BF16), preferred_element_type=F32)

    return _pc(body, name=name, grid=(L,), out_shape=_sds((L, P, n), F32),
               in_specs=[pl.BlockSpec((P, D), lambda l: (0, 0)), pl.BlockSpec((None, D, n), lambda l: (l, 0, 0))],
               out_specs=pl.BlockSpec((None, P, n), lambda l: (l, 0, 0)),
               compiler_params=_cparams(("parallel",)))(c_pad, w_ada)


def _ada_grad(name, c_pad, d_ada):
    L, P, n = d_ada.shape
    D = c_pad.shape[1]

    def body(c_ref, d_ref, o_ref):
        cc = c_ref[...]
        cond = cc / (1.0 + jnp.exp(-cc))
        o_ref[...] = lax.dot_general(cond.astype(BF16), d_ref[...].astype(BF16), _TN, preferred_element_type=F32)

    return _pc(body, name=name, grid=(L,), out_shape=_sds((L, D, n), F32),
               in_specs=[pl.BlockSpec((P, D), lambda l: (0, 0)), pl.BlockSpec((None, P, n), lambda l: (l, 0, 0))],
               out_specs=pl.BlockSpec((None, D, n), lambda l: (l, 0, 0)),
               compiler_params=_cparams(("parallel",)))(c_pad, d_ada)


def _adam_rows(rows, cols):
    cap = max(8, (2 ** 20) // (4 * cols))
    if rows <= cap:
        return rows
    tr = 8
    while tr * 2 <= cap and rows % (tr * 2) == 0:
        tr *= 2
    return tr


def _adam_update(g_ref, slots, w_ref, m_ref, v_ref, go_ref, d_ref, mo_ref, vo_ref):
    if slots:
        gg = g_ref[0].astype(F32)
        for k in range(1, slots):
            gg = gg + g_ref[k].astype(F32)
    else:
        gg = g_ref[...]
    c1 = 1.0 / (1.0 - ADAM_B1 ** ADAM_STEP)
    c2 = 1.0 / (1.0 - ADAM_B2 ** ADAM_STEP)
    mn = ADAM_B1 * m_ref[...] + (1.0 - ADAM_B1) * gg
    vn = ADAM_B2 * v_ref[...] + (1.0 - ADAM_B2) * (gg * gg)
    go_ref[...] = gg
    mo_ref[...] = mn
    vo_ref[...] = vn
    d_ref[...] = -ADAM_LR * ((mn * c1) / (jnp.sqrt(vn * c2) + ADAM_EPS) + ADAM_WD * w_ref[...])


def _adamw(name, w, m, v, g, slots):
    rows, cols = w.shape
    tr = _adam_rows(rows, cols)

    def body(w_ref, m_ref, v_ref, g_ref, *outs):
        _adam_update(g_ref, slots, w_ref, m_ref, v_ref, *outs)

    blk = pl.BlockSpec((tr, cols), lambda i: (i, 0))
    gspec = pl.BlockSpec((slots, tr, cols), lambda i: (0, i, 0)) if slots else blk
    return _pc(body, name=name, grid=(rows // tr,), out_shape=[_sds((rows, cols), F32)] * 4,
               in_specs=[blk, blk, blk, gspec], out_specs=[blk] * 4,
               compiler_params=_cparams(("parallel",)))(w, m, v, g)


def _adamw_layers(name, w, m, v, slot_list):
    nl = len(slot_list)
    slots, rows, cols = slot_list[0].shape
    tr = _adam_rows(rows, cols)
    nblk = rows // tr

    def body(w_ref, m_ref, v_ref, *rest):
        layer = pl.program_id(0)
        for ll in range(nl):
            @pl.when(layer == ll)
            def _(g_ref=rest[ll]):
                _adam_update(g_ref, slots, w_ref, m_ref, v_ref, *rest[nl:])

    def gspec(ll):
        return pl.BlockSpec((slots, tr, cols),
                            lambda l, i: (0, jnp.where(l == ll, i, jnp.where(l > ll, nblk - 1, 0)), 0))

    blk = pl.BlockSpec((tr, cols), lambda l, i: (l * nblk + i, 0))
    return _pc(body, name=name, grid=(nl, nblk), out_shape=[_sds((nl * rows, cols), F32)] * 4,
               in_specs=[blk, blk, blk] + [gspec(ll) for ll in range(nl)], out_specs=[blk] * 4,
               compiler_params=_cparams(("parallel", "parallel")))(w, m, v, *slot_list)


def _rope_tables(T):
    pos = np.arange(T)
    lane = np.arange(HD)
    inv = ROPE_THETA ** (-np.arange(0, HD, 2, dtype=np.float32) / HD)
    ang = pos.astype(np.float32)[:, None] * inv[None, :]
    ang = np.concatenate([ang, ang], axis=-1)
    cos1 = np.cos(ang)
    ss1 = np.sin(ang) * np.where(lane < HD // 2, -1.0, 1.0)[None, :]
    half = HD // 2
    invh = ROPE_THETA ** (-np.arange(0, half, 2, dtype=np.float32) / half)

    def tab(p):
        a = p.astype(np.float32)[:, None] * invh[None, :]
        return np.concatenate([a, a], axis=-1)

    angx = np.concatenate([tab(pos // GRID_W), tab(pos % GRID_W)], axis=-1)
    cosx, sinx = np.cos(angx), np.sin(angx)
    first = (lane % half) < half // 2
    sax = np.where(first[None, :], -sinx, 0.0)
    sbx = np.where(first[None, :], 0.0, sinx)
    return tuple(jnp.asarray(a, F32) for a in (cos1, ss1, cosx, sax, sbx))


def _na_selectors(att, transposed):
    own_r = np.arange(att.t // GRID_W)[:, None]
    span_r = np.arange(att.span // GRID_W)[None, :] - att.pad // GRID_W
    dr = own_r - span_r if transposed else span_r - own_r
    row_sel = (np.clip(dr, -(NA_ROWS - 1), NA_ROWS - 1)[..., None] + NA_ROWS - 1 == np.arange(2 * NA_ROWS - 1))
    col = np.arange(GRID_W)
    dc = col[:, None] - col[None, :] if transposed else col[None, :] - col[:, None]
    col_sel = (np.clip(dc, -(NA_COLS - 1), NA_COLS - 1)[..., None] + NA_COLS - 1 == np.arange(2 * NA_COLS - 1))
    return jnp.asarray(row_sel.astype(np.float32)), jnp.asarray(col_sel.astype(np.float32))


def _na_bias_table(att, rel_bias, transposed):
    row_sel, col_sel = _na_selectors(att, transposed)
    tab = jnp.einsum("xyi,hij,cdj->hxcyd", row_sel, rel_bias * LOG2E, col_sel, precision=lax.Precision.HIGHEST)
    return tab.reshape(rel_bias.shape[0], att.t, att.span)


def _na_bias_grad(att, dtab):
    row_sel, col_sel = _na_selectors(att, False)
    d5 = dtab.reshape(dtab.shape[0], att.t // GRID_W, GRID_W, att.span // GRID_W, GRID_W)
    return jnp.einsum("xyi,hxcyd,cdj->hij", row_sel, d5, col_sel, precision=lax.Precision.HIGHEST)


def _att_specs(T):
    specs = {
        "A": _Att(T, 1, 4, 2, 0, 0, A_V, 0, 0, 0, "band", reach=A_REACH, pad=128, tile=BAND_TILE),
        "C": _Att(T, 1, 4, 2, 0, 0, C_V, 0, 0, 0, "dense", tile=DENSE_TILE),
        "D": _Att(T, 1, 4, 1, 0, D_K, D_V, 0, 0, 0, "na", pad=256, rows=T // GRID_W),
    }
    for p, (window, dil) in enumerate(B_PATTERNS):
        specs["B%d" % p] = _Att(T // dil, dil, 4, 1, 0, 0, 0, 4, 4, 4, "band", reach=(window // 2) // dil, pad=64,
                                tile=BAND_TILE)
    return specs


def kernel(x, c, norm_g, w_ada, b_ada, w_in, a_sink, c_q_norm, c_k_norm, d_rel_bias, w_gate_merge, w_branch, w_out, final_g, loss_target, m_norm_g, m_w_ada, m_b_ada, m_w_in, m_a_sink, m_c_q_norm, m_c_k_norm, m_d_rel_bias, m_w_gate_merge, m_w_branch, m_w_out, m_final_g, v_norm_g, v_w_ada, v_b_ada, v_w_in, v_a_sink, v_c_q_norm, v_c_k_norm, v_d_rel_bias, v_w_gate_merge, v_w_branch, v_w_out, v_final_g):
    T, D = x.shape[1], x.shape[2]
    NL = norm_g.shape[0]
    x0 = x.reshape(T, D)
    tgt = loss_target.reshape(T, D)
    me = 4 * lax.axis_index("x") + 2 * lax.axis_index("y") + lax.axis_index("c")
    att = _att_specs(T)
    tabs = _rope_tables(T)
    n_ada = w_ada.shape[2]

    (c_all,) = _exchange("gather_c", [c], gather=True)
    c_pad = jnp.pad(c_all.reshape(N_DEV, D), ((0, HD - N_DEV), (0, 0)))
    ada_part = _ada_fwd("ada_fwd", c_pad, w_ada)[:, :N_DEV]
    (ada_all,) = _exchange("gather_ada", [ada_part], gather=True)
    ada_mine = lax.dynamic_index_in_dim(ada_all, me, axis=2, keepdims=False)
    ada = jnp.transpose(ada_mine, (1, 0, 2)).reshape(NL, N_DEV * n_ada) + b_ada
    shift, scale, gate = ada[:, :D], ada[:, D:2 * D], ada[:, 2 * D:]

    shards = [[w[l].astype(BF16) for w in (w_in, w_gate_merge, w_branch, w_out)] for l in range(NL)]
    weights = [[None] * 4 for _ in range(NL)]
    weights[0][0] = _exchange("gather_w0", shards[0][:1], gather=True)[0]

    def views(a_, arrs):
        return [a.reshape(a_.L, -1) for a in arrs]

    saved = []
    xl = x0
    for l in range(NL):
        sc_l, sh_l, gt_l = scale[l][None], shift[l][None], gate[l][None]
        ng_l = norm_g[l][None]
        cqn, ckn = c_q_norm[l][None], c_k_norm[l][None]
        sink3 = jnp.broadcast_to(a_sink[l][:, None, None], (4, 1, HD))
        btab = _na_bias_table(att["D"], d_rel_bias[l], False)

        h = _prenorm_fwd("prenorm_fwd", xl, ng_l, sc_l, sh_l)
        if l == 0:
            qkvg, *weights[0][1:] = _mm_nn("mm_in_gather", h, weights[0][0], BF16, rider=_Rider(shards[0][1:], True))
        else:
            qkvg = _mm_nn("mm_in", h, weights[l][0], BF16)
        if l + 1 < NL:
            mgl, *weights[l + 1][2:] = _mm_nn("mm_gm_gather", h, weights[l][1], BF16, rider=_Rider(shards[l + 1][2:], True))
        else:
            mgl = _mm_nn("mm_gm", h, weights[l][1], BF16)
        Wb, Wo = weights[l][2], weights[l][3].reshape(1, D, D)
        qa, ka, qb, kb, vb, qc, kc, qd = _qk_prep("qk_prep", qkvg, tabs, cqn, ckn)
        oa, la = _attn_fwd("attn_a_fwd", att["A"], qa, ka, qkvg, sink=sink3)
        ob, lb = [], []
        for p in range(len(B_PATTERNS)):
            a_ = att["B%d" % p]
            o_, l_ = _attn_fwd("attn_b%d_fwd" % p, a_, *views(a_, (qb, kb, vb)))
            ob.append(o_.reshape(T, 512))
            lb.append(l_.reshape(T, 512))
        if l + 1 < NL:
            oc, lc, *weights[l + 1][:2] = _attn_fwd("attn_c_fwd_gather", att["C"], qc, kc, qkvg,
                                                    rider=_Rider(shards[l + 1][:2], True))
        else:
            oc, lc = _attn_fwd("attn_c_fwd", att["C"], qc, kc, qkvg)
        od, ld = _attn_fwd("attn_d_fwd", att["D"], qd, qkvg, qkvg, bias=btab)
        br = _gate_fwd("gate_fwd", qkvg, oa, ob, lb, oc, od)
        proj = _branch_nn("mm_branch", br, Wb, BF16)
        merged = _merge_fwd("merge_fwd", mgl, proj, D)
        out = _mm_nn("mm_out", merged, Wo, F32)
        xn = _resid_fwd("resid_fwd", xl, out, gt_l)
        saved.append(dict(x=xl, h=h, qkvg=qkvg, mgl=mgl, qa=qa, ka=ka, qb=qb, kb=kb, vb=vb, qc=qc, kc=kc, qd=qd, oa=oa,
                          la=la, ob=ob, lb=lb, oc=oc, lc=lc, od=od, ld=ld, br=br, proj=proj, merged=merged, out=out,
                          btab=btab))
        xl = xn

    dx, loss_part, d_final_g = _final("final_loss", xl, tgt, final_g[None])

    grads, slots = [None] * NL, [None] * NL
    d_norm_g, d_ada, d_sink, d_cq, d_ck, d_bias = [None] * NL, [None] * NL, [None] * NL, [None] * NL, [None] * NL, [None] * NL
    for l in reversed(range(NL)):
        S = saved[l]
        Wi, Wg, Wb, Wo = weights[l]
        Wo = Wo.reshape(1, D, D)
        sc_l, gt_l, ng_l = scale[l][None], gate[l][None], norm_g[l][None]
        cqn, ckn = c_q_norm[l][None], c_k_norm[l][None]
        sink_row = jnp.repeat(a_sink[l], HD)[None]
        qkvg = S["qkvg"]

        dout, dgate = _resid_bwd("resid_bwd", dx, S["out"], gt_l)
        g_out = _mm_tn("mm_out_dw", S["merged"], dout, 1, BF16).reshape(N_DEV, D // N_DEV, D)
        dm = _mm_nt("mm_out_dx", dout, Wo, F32)
        dproj, dmgl = _merge_bwd("merge_bwd", dm, S["mgl"], S["proj"], D)
        g_br = _branch_tn("mm_branch_dw", S["br"], dproj, D // N_DEV, BF16)
        dbr = _branch_nt("mm_branch_dx", dproj, Wb, F32)
        (doa, dla, dob0, dob1, dob2, dlb0, dlb1, dlb2, doc, dlc, dod, dld, dg, dsk) = _gate_bwd(
            "gate_bwd", dbr, qkvg, S["oa"], S["la"], sink_row, S["ob"], S["lb"], S["oc"], S["od"])
        d_sink[l] = dsk.reshape(4, HD)[:, 0]

        dqa, dka, dva, _ = _attn_bwd("attn_a", att["A"], S["qa"], S["ka"], qkvg, doa, S["la"], dla)
        dqb, dkb, dvb = [], [], []
        for p, (dob, dlb) in enumerate(((dob0, dlb0), (dob1, dlb1), (dob2, dlb2))):
            a_ = att["B%d" % p]
            dq_, dk_, dv_, _ = _attn_bwd("attn_b%d" % p, a_, *views(a_, (S["qb"], S["kb"], S["vb"], dob, S["lb"][p], dlb)))
            dqb.append(dq_.reshape(T, 512))
            dkb.append(dk_.reshape(T, 512))
            dvb.append(dv_.reshape(T, 512))
        if l + 1 < NL:
            dqc, dkc, dvc, (s_small, s_big) = _attn_bwd(
                "attn_c_scatter", att["C"], S["qc"], S["kc"], qkvg, doc, S["lc"], dlc,
                riders=(_Rider(grads[l + 1][2:], False), _Rider(grads[l + 1][:2], False)))
            slots[l + 1] = s_big + s_small
        else:
            dqc, dkc, dvc, _ = _attn_bwd("attn_c", att["C"], S["qc"], S["kc"], qkvg, doc, S["lc"], dlc)
        dqd, dkd, dvd, dtab = _attn_bwd("attn_d", att["D"], S["qd"], qkvg, qkvg, dod, S["ld"], dld, bias=S["btab"],
                                        bias_t=_na_bias_table(att["D"], d_rel_bias[l], True))
        d_bias[l] = _na_bias_grad(att["D"], dtab)

        dqkvg, dcq, dck = _qkvg_bwd("qkvg_bwd", qkvg, tabs, cqn, ckn, (dqa, dka, dva), (dqb, dkb, dvb),
                                    (dqc, dkc, dvc), (dqd, dkd, dvd), dg)
        d_cq[l], d_ck[l] = dcq[0], dck[0]
        g_in = _mm_tn("mm_in_dw", S["h"], dqkvg, N_DEV, BF16)
        g_gm = _mm_tn("mm_gm_dw", S["h"], dmgl, N_DEV, BF16)
        grads[l] = [g_in, g_gm, g_br, g_out]
        if l == 0:
            dh, s_in, s_br, s_out = _mm_nt("mm_in_dx_scatter", dqkvg, Wi, F32, rider=_Rider([g_in, g_br, g_out], False))
            dh, s_gm = _mm_nt("mm_gm_dx_scatter", dmgl, Wg, F32, add=dh, rider=_Rider([g_gm], False))
            slots[0] = [s_in, s_gm, s_br, s_out]
        else:
            dh = _mm_nt("mm_in_dx", dqkvg, Wi, F32)
            dh = _mm_nt("mm_gm_dx", dmgl, Wg, F32, add=dh)
        dx, dshift, dscale, dng = _prenorm_bwd("prenorm_bwd", S["x"], dh, dx, ng_l, sc_l)
        d_norm_g[l] = dng[0]
        d_ada[l] = jnp.concatenate([dshift[0], dscale[0], dgate[0]])

    grad_x = dx.reshape(1, T, D)

    def big(name, w, m, v, k):
        cols = w.shape[-1]
        res = _adamw_layers(name, w.reshape(-1, cols), m.reshape(-1, cols), v.reshape(-1, cols),
                            [slots[l][k].reshape(N_DEV, -1, cols) for l in range(NL)])
        return [r.reshape(w.shape) for r in res]

    r_in = big("adamw_in", w_in, m_w_in, v_w_in, 0)
    r_gm = big("adamw_gm", w_gate_merge, m_w_gate_merge, v_w_gate_merge, 1)
    r_br = big("adamw_branch", w_branch, m_w_branch, v_w_branch, 2)
    r_out = big("adamw_out", w_out, m_w_out, v_w_out, 3)

    small_w = [norm_g, b_ada, a_sink, c_q_norm, c_k_norm, d_rel_bias, final_g]
    small_m = [m_norm_g, m_b_ada, m_a_sink, m_c_q_norm, m_c_k_norm, m_d_rel_bias, m_final_g]
    small_v = [v_norm_g, v_b_ada, v_a_sink, v_c_q_norm, v_c_k_norm, v_d_rel_bias, v_final_g]
    small_g = [jnp.stack(d_norm_g), jnp.stack(d_ada), jnp.stack(d_sink), jnp.stack(d_cq), jnp.stack(d_ck),
               jnp.stack(d_bias), d_final_g[0]]
    sizes = [int(np.prod(w.shape)) for w in small_w]
    total = sum(sizes) + HD
    rows = -(-total // (8 * HD)) * 8

    def pack(parts, extra):
        flat = jnp.concatenate([p.reshape(-1).astype(F32) for p in parts] + [extra])
        return jnp.pad(flat, (0, rows * HD - flat.shape[0])).reshape(rows, HD)

    zeros = jnp.zeros((HD,), F32)
    (g_slots,) = _exchange("gather_small", [pack(small_g, loss_part[0])], gather=True)
    gs, ds, ms, vs = _adamw("adamw_small", pack(small_w, zeros), pack(small_m, zeros), pack(small_v, zeros), g_slots, N_DEV)

    def unpack(flat2d):
        flat = flat2d.reshape(-1)
        res, o = [], 0
        for w, n in zip(small_w, sizes):
            res.append(flat[o:o + n].reshape(w.shape))
            o += n
        return res, flat[o]

    sg, loss = unpack(gs)
    sd, _ = unpack(ds)
    sm, _ = unpack(ms)
    sv, _ = unpack(vs)

    o_b = sizes[0]
    d_ada_all = g_slots.reshape(N_DEV, -1)[:, o_b:o_b + sizes[1]].reshape(N_DEV, NL, N_DEV, n_ada)
    d_ada_mine = jnp.transpose(lax.dynamic_index_in_dim(d_ada_all, me, axis=2, keepdims=False), (1, 0, 2))
    g_ada = _ada_grad("ada_grad", c_pad, jnp.pad(d_ada_mine, ((0, 0), (0, HD - N_DEV), (0, 0))))
    r_ada = _adamw("adamw_ada", w_ada.reshape(-1, n_ada), m_w_ada.reshape(-1, n_ada), v_w_ada.reshape(-1, n_ada),
                   g_ada.reshape(-1, n_ada), 0)
    r_ada = [r.reshape(w_ada.shape) for r in r_ada]

    def kind(k):
        sm_ = (sg, sd, sm, sv)[k]
        return [sm_[0], r_ada[k], sm_[1], r_in[k], sm_[2], sm_[3], sm_[4], sm_[5], r_gm[k], r_br[k], r_out[k], sm_[6]]

    return (loss, grad_x, *kind(0), *kind(1), *kind(2), *kind(3))
```
